```python
import math
import jax
import jax.numpy as jnp
from jax import lax
import numpy as np

D_MODEL = 1024
BATCH = 8
SEQ = 2048
DEPTH = 2

GRID_W = 64
CTX_LEN = 256
ROT_DIM = 64
ROPE_BASE = 10000.0
Q_BLOCK = 128
CHUNK = 32

DA_HEADS = 4
DA_DH = ROT_DIM
DA_DV = 2 * DA_DH
GLA_HEADS = 4
GLA_DK = 64
GLA_DV = 128
GLA_LR = 16
GLA_NORMALIZER = 16.0
MLA_HEADS = 4
MLA_Q_RANK = 256
MLA_KV_RANK = 128
MLA_NOPE = 128
MLA_ROPE = ROT_DIM
MLA_DV = 128
MLA_SCALE = (MLA_NOPE + MLA_ROPE) ** -0.5
HG_HEADS = 4
HG_DK = 128
HG_DV = 128
D_FF = 3584
N_EXPERTS = 8
TOP_K = 2

LN_EPS = 1e-5
RMS_EPS = 1e-6
DN_ALPHA = (2 * DEPTH) ** 0.25
DN_BETA = (8 * DEPTH) ** -0.25
N_EVEN = (DEPTH + 1) // 2
N_ODD = DEPTH // 2

A_QK = DA_HEADS * 2 * DA_DH
A_V = DA_HEADS * DA_DV
B_QK = GLA_HEADS * GLA_DK
B_V = GLA_HEADS * GLA_DV
EV_SIZES = (A_QK, A_QK, A_V, B_QK, B_QK, B_V, B_V, GLA_LR, GLA_LR)
EV_IN = sum(EV_SIZES)
EV_MIX = A_V + B_V
HG_K = HG_HEADS * HG_DK
HG_W = HG_HEADS * HG_DV
OD_SIZES = (MLA_Q_RANK, MLA_KV_RANK, MLA_ROPE, HG_K, HG_K, HG_K, HG_W, HG_W)
OD_IN = sum(OD_SIZES)
OD_MIX = MLA_HEADS * MLA_DV + HG_W

kernel_name = "hybrid_diffusion_backbone"


def _layernorm(x, g, b):
    xf = x.astype(jnp.float32)
    mu = jnp.mean(xf, axis=-1, keepdims=True)
    var = jnp.mean(jnp.square(xf - mu), axis=-1, keepdims=True)
    return ((xf - mu) * lax.rsqrt(var + LN_EPS) * g + b).astype(x.dtype)


def _rmsnorm(x, g):
    xf = x.astype(jnp.float32)
    return (xf * lax.rsqrt(jnp.mean(xf * xf, axis=-1, keepdims=True) + RMS_EPS) * g).astype(x.dtype)


def _split_cols(t, sizes):
    return jnp.split(t, np.cumsum(sizes)[:-1].tolist(), axis=-1)


def _heads(t, n):
    b, s, _ = t.shape
    return t.reshape(b, s, n, -1).transpose(0, 2, 1, 3)


def _unheads(t):
    b, n, s, d = t.shape
    return t.transpose(0, 2, 1, 3).reshape(b, s, n * d)


def _axial_rope_table(rows, rot_dim):
    n_freq = rot_dim // 4
    inv = ROPE_BASE ** (-jnp.arange(n_freq, dtype=jnp.float32) / n_freq)
    row = jnp.repeat(jnp.arange(rows, dtype=jnp.float32), GRID_W)
    col = jnp.tile(jnp.arange(GRID_W, dtype=jnp.float32), rows)
    ang = jnp.concatenate([row[:, None] * inv, col[:, None] * inv], axis=-1)
    return jnp.cos(ang), jnp.sin(ang)


def _rope(t, cos, sin):
    tf = t.astype(jnp.float32).reshape(t.shape[:-1] + (-1, 2))
    t0, t1 = tf[..., 0], tf[..., 1]
    out = jnp.stack([t0 * cos - t1 * sin, t0 * sin + t1 * cos], axis=-1)
    return out.reshape(t.shape).astype(t.dtype)


def _split_q_blocks(t):
    b, h, s, d = t.shape
    return t.reshape(b, h, s // Q_BLOCK, Q_BLOCK, d).transpose(2, 0, 1, 3, 4)


def _merge_q_blocks(t):
    nb, b, h, q, d = t.shape
    return t.transpose(1, 2, 0, 3, 4).reshape(b, h, nb * q, d)


def _softmax32(s):
    return jax.nn.softmax(s.astype(jnp.float32), axis=-1)


def _diff_attend(q1, q2, k1, k2, v, lam):
    scale = DA_DH ** -0.5
    p1 = _softmax32(jnp.einsum('bhqd,bhkd->bhqk', q1, k1) * scale)
    p2 = _softmax32(jnp.einsum('bhqd,bhkd->bhqk', q2, k2) * scale)
    return jnp.einsum('bhqk,bhkd->bhqd', (p1 - lam * p2).astype(v.dtype), v)


def _mla_attend(q, k, v):
    p = _softmax32(jnp.einsum('bhqd,bhkd->bhqk', q, k) * MLA_SCALE).astype(v.dtype)
    return jnp.einsum('bhqk,bhkd->bhqd', p, v)


def _chunk_scan(q, k, v, g, s0, need_out):
    dtype = v.dtype
    b_, h_, s_, dk = q.shape
    dv = v.shape[-1]
    n = s_ // CHUNK
    q, k, v, g = (t.astype(jnp.float32).reshape(b_, h_, n, CHUNK, t.shape[-1]) for t in (q, k, v, g))
    bcum = jnp.cumsum(g, axis=3)
    b_last = bcum[:, :, :, -1:, :]
    d_state = jnp.einsum('bhnck,bhncv->bhnkv', k * jnp.exp(b_last - bcum), v)
    chunk_decay = jnp.exp(b_last[:, :, :, 0, :])

    def step(state, inp):
        ds_n, dec_n = inp
        return state * dec_n[..., None] + ds_n, state

    s_fin, s_start = lax.scan(step, s0, (jnp.moveaxis(d_state, 2, 0), jnp.moveaxis(chunk_decay, 2, 0)))
    if not need_out:
        return None, s_fin
    s_start = jnp.moveaxis(s_start, 0, 2)
    b_ref = bcum[:, :, :, CHUNK // 2 - 1:CHUNK // 2, :]
    att = jnp.einsum('bhnik,bhnjk->bhnij', q * jnp.exp(bcum - b_ref), k * jnp.exp(b_ref - bcum))
    att = jnp.where(jnp.tril(jnp.ones((CHUNK, CHUNK), dtype=bool)), att, 0.0)
    o = jnp.einsum('bhnij,bhnjv->bhniv', att, v) + jnp.einsum('bhnik,bhnkv->bhniv', q * jnp.exp(bcum), s_start)
    return o.reshape(b_, h_, s_, dv).astype(dtype), s_fin


def _bidir_scan(q_c, k_c, v_c, g_c, q_l, k_l, v_l, g_l, need_ctx):
    b_, h_, _, dk = q_l.shape
    dv = v_l.shape[-1]
    o_c_sum, o_l_sum = None, None
    for d in range(2):
        fl = (lambda t: jnp.flip(t, axis=2)) if d == 1 else (lambda t: t)
        s0 = jnp.zeros((b_, h_, dk, dv), jnp.float32)
        o_c, s_c = _chunk_scan(fl(q_c), fl(k_c[d]), fl(v_c), fl(g_c[d]), s0, need_ctx)
        o_l, _ = _chunk_scan(fl(q_l), fl(k_l[d]), fl(v_l), fl(g_l[d]), s_c, True)
        o_l_sum = fl(o_l) if d == 0 else o_l_sum + fl(o_l)
        if need_ctx:
            o_c_sum = fl(o_c) if d == 0 else o_c_sum + fl(o_c)
    return o_c_sum, o_l_sum


def _even_mixer(h_c, h_l, cos, sin, w_in, lam_vecs, lam_init, subln_g, gk_w2, gk_b, gla_g, w_out, need_ctx):
    def project(h, rotate):
        qa, ka, va, qb, kb, vb, gb, lr_f, lr_b = _split_cols(h @ w_in, EV_SIZES)
        qa, ka = _heads(qa, DA_HEADS), _heads(ka, DA_HEADS)
        q1, q2, k1, k2 = qa[..., :DA_DH], qa[..., DA_DH:], ka[..., :DA_DH], ka[..., DA_DH:]
        if rotate:
            q1, q2, k1, k2 = (_rope(t, cos, sin) for t in (q1, q2, k1, k2))
        gk = tuple(_heads(jax.nn.log_sigmoid((lr @ gk_w2[d] + gk_b[d]).astype(jnp.float32)) / GLA_NORMALIZER, GLA_HEADS)
                   for d, lr in enumerate((lr_f, lr_b)))
        gla_k = _heads(kb, GLA_HEADS)
        return (q1, q2, k1, k2, _heads(va, DA_HEADS), _heads(qb, GLA_HEADS) * GLA_DK ** -0.5,
                (gla_k, gla_k), _heads(vb, GLA_HEADS), gk, _heads(gb, GLA_HEADS))

    q1c, q2c, k1c, k2c, vac, qbc, kbc, vbc, gkc, gbc = project(h_c, False)
    q1l, q2l, k1l, k2l, val, qbl, kbl, vbl, gkl, gbl = project(h_l, True)
    lv = lam_vecs.astype(jnp.float32)
    lam = jnp.exp(jnp.sum(lv[0] * lv[1])) - jnp.exp(jnp.sum(lv[2] * lv[3])) + lam_init
    k1_all = jnp.concatenate([k1c, k1l], axis=2)
    k2_all = jnp.concatenate([k2c, k2l], axis=2)
    v_all = jnp.concatenate([vac, val], axis=2)
    o_a_l = _merge_q_blocks(lax.map(lambda qs: _diff_attend(qs[0], qs[1], k1_all, k2_all, v_all, lam),
                                    (_split_q_blocks(q1l), _split_q_blocks(q2l))))
    o_b_c, o_b_l = _bidir_scan(qbc, kbc, vbc, gkc, qbl, kbl, vbl, gkl, need_ctx)

    def merge(o_a, o_b, gate):
        o_a = _rmsnorm(o_a, subln_g) * (1.0 - lam_init)
        o_b = _rmsnorm(o_b, gla_g) * jax.nn.silu(gate)
        return jnp.concatenate([_unheads(o_a), _unheads(o_b)], axis=-1) @ w_out

    y_l = merge(o_a_l, o_b_l, gbl)
    y_c = merge(_diff_attend(q1c, q2c, k1c, k2c, vac, lam), o_b_c, gbc) if need_ctx else None
    return y_c, y_l


def _odd_mixer(h_c, h_l, cos, sin, w_in, q_norm_g, kv_norm_g, w_uq, w_ukv, lb, hg_g, w_out, need_ctx):
    def project(h, rotate):
        cq, ckv, kr, hq, hf_f, hf_b, hi, hgate = _split_cols(h @ w_in, OD_SIZES)
        q = _heads(_rmsnorm(cq, q_norm_g) @ w_uq, MLA_HEADS)
        kv = _heads(_rmsnorm(ckv, kv_norm_g) @ w_ukv, MLA_HEADS)
        q_nope, q_rope = q[..., :MLA_NOPE], q[..., MLA_NOPE:]
        k_nope, v = kv[..., :MLA_NOPE], kv[..., MLA_NOPE:]
        k_rope = kr[:, None]
        if rotate:
            q_rope, k_rope = _rope(q_rope, cos, sin), _rope(k_rope, cos, sin)
        q = jnp.concatenate([q_nope, q_rope], axis=-1)
        k = jnp.concatenate([k_nope, jnp.broadcast_to(k_rope, k_nope.shape[:-1] + (MLA_ROPE,))], axis=-1)
        f = [lb + (1.0 - lb) * jax.nn.sigmoid(t.astype(jnp.float32)) for t in (hf_f, hf_b)]
        hk = tuple(_heads(1.0 - t, HG_HEADS) for t in f)
        hlog = tuple(_heads(jnp.log(t), HG_HEADS) for t in f)
        return q, k, v, _heads(hq, HG_HEADS), hk, _heads(hi, HG_HEADS), hlog, _heads(hgate, HG_HEADS)

    qc, kc, vc, hqc, hkc, hic, hgc, gtc = project(h_c, False)
    ql, kl, vl, hql, hkl, hil, hgl, gtl = project(h_l, True)
    k_all = jnp.concatenate([kc, kl], axis=2)
    v_all = jnp.concatenate([vc, vl], axis=2)
    o_m_l = _merge_q_blocks(lax.map(lambda qb: _mla_attend(qb, k_all, v_all), _split_q_blocks(ql)))
    o_h_c, o_h_l = _bidir_scan(hqc, hkc, hic, hgc, hql, hkl, hil, hgl, need_ctx)

    def merge(o_m, o_h, gate):
        o_h = _rmsnorm(o_h, hg_g) * jax.nn.silu(gate)
        return jnp.concatenate([_unheads(o_m), _unheads(o_h)], axis=-1) @ w_out

    y_l = merge(o_m_l, o_h_l, gtl)
    y_c = merge(_mla_attend(qc, kc, vc), o_h_c, gtc) if need_ctx else None
    return y_c, y_l


def _swiglu(h, w1, w2):
    hg, hu = jnp.split(h @ w1, 2, axis=-1)
    return (jax.nn.silu(hg) * hu) @ w2


def _moe(h, w_router, w1, w2):
    shape = h.shape
    hf = h.reshape(-1, shape[-1])
    logits = (hf @ w_router).astype(jnp.float32)
    top_v, top_i = lax.top_k(logits, TOP_K)
    gates = jax.nn.softmax(top_v, axis=-1)
    comb = jnp.sum(jax.nn.one_hot(top_i, N_EXPERTS, dtype=jnp.float32) * gates[..., None], axis=1)
    out = jnp.zeros_like(hf)
    for e in range(N_EXPERTS):
        out = out + comb[:, e:e + 1].astype(hf.dtype) * _swiglu(hf, w1[e], w2[e])
    return out.reshape(shape)


def _ada(cond, w, b):
    return jnp.split(jax.nn.silu(cond) @ w + b, 6, axis=-1)


def _diff_lambda_init(layer):
    return 0.8 - 0.6 * math.exp(-0.3 * layer)


def setup_inputs(seed: int = 0) -> dict:
    key = jax.random.key(seed)
    ks = iter(jax.random.split(key, 32))

    def nrm(shape, scale):
        return jax.random.normal(next(ks), shape, jnp.float32) * scale

    D = D_MODEL
    return {
        "x": nrm((BATCH, SEQ, D), 1.0),
        "c": nrm((BATCH, D), 1.0),
        "ctx": nrm((BATCH, CTX_LEN, D), 1.0),
        "c_ctx": nrm((D,), 1.0),
        "ada_w": nrm((DEPTH, D, 6 * D), 0.5 * D ** -0.5),
        "ada_b": nrm((DEPTH, 6 * D), 0.02),
        "post_ln_g": 1.0 + nrm((DEPTH, 2, D), 0.02),
        "post_ln_b": nrm((DEPTH, 2, D), 0.02),
        "lb_table": nrm((DEPTH, HG_K), 1.0),
        "ev_w_in": nrm((N_EVEN, D, EV_IN), D ** -0.5),
        "ev_lam": nrm((N_EVEN, 4, DA_DH), 0.1),
        "ev_subln_g": 1.0 + nrm((N_EVEN, DA_DV), 0.02),
        "ev_gk_w2": nrm((N_EVEN, 2, GLA_LR, B_QK), GLA_LR ** -0.5),
        "ev_gk_b": nrm((N_EVEN, 2, B_QK), 0.02),
        "ev_gla_norm_g": 1.0 + nrm((N_EVEN, GLA_DV), 0.02),
        "ev_w_out": nrm((N_EVEN, EV_MIX, D), DN_BETA * EV_MIX ** -0.5),
        "ev_ffn_w1": nrm((N_EVEN, D, 2 * D_FF), D ** -0.5),
        "ev_ffn_w2": nrm((N_EVEN, D_FF, D), DN_BETA * D_FF ** -0.5),
        "od_w_in": nrm((N_ODD, D, OD_IN), D ** -0.5),
        "od_q_norm_g": 1.0 + nrm((N_ODD, MLA_Q_RANK), 0.02),
        "od_kv_norm_g": 1.0 + nrm((N_ODD, MLA_KV_RANK), 0.02),
        "od_w_uq": nrm((N_ODD, MLA_Q_RANK, MLA_HEADS * (MLA_NOPE + MLA_ROPE)), MLA_Q_RANK ** -0.5),
        "od_w_ukv": nrm((N_ODD, MLA_KV_RANK, MLA_HEADS * (MLA_NOPE + MLA_DV)), MLA_KV_RANK ** -0.5),
        "od_hg_norm_g": 1.0 + nrm((N_ODD, HG_DV), 0.02),
        "od_w_out": nrm((N_ODD, OD_MIX, D), DN_BETA * OD_MIX ** -0.5),
        "od_router": nrm((N_ODD, D, N_EXPERTS), D ** -0.5),
        "od_exp_w1": nrm((N_ODD, N_EXPERTS, D, 2 * D_FF), D ** -0.5),
        "od_exp_w2": nrm((N_ODD, N_EXPERTS, D_FF, D), DN_BETA * D_FF ** -0.5),
    }


def reference(x, c, ctx, c_ctx, ada_w, ada_b, post_ln_g, post_ln_b, lb_table,
              ev_w_in, ev_lam, ev_subln_g, ev_gk_w2, ev_gk_b, ev_gla_norm_g, ev_w_out, ev_ffn_w1, ev_ffn_w2,
              od_w_in, od_q_norm_g, od_kv_norm_g, od_w_uq, od_w_ukv, od_hg_norm_g, od_w_out,
              od_router, od_exp_w1, od_exp_w2):
    rows = x.shape[1] // GRID_W
    cos, sin = _axial_rope_table(rows, ROT_DIM)
    lb_soft = jax.nn.softmax(lb_table.astype(jnp.float32), axis=0)
    lower_bounds = jnp.cumsum(lb_soft, axis=0) - lb_soft[0]
    for layer in range(DEPTH):
        last = layer == DEPTH - 1
        j = layer // 2
        m_l = [t[:, None, :] for t in _ada(c, ada_w[layer], ada_b[layer])]
        m_c = _ada(c_ctx, ada_w[layer], ada_b[layer])
        h_l = x * (1.0 + m_l[1]) + m_l[0]
        h_c = ctx * (1.0 + m_c[1]) + m_c[0]
        if layer % 2 == 0:
            y_c, y_l = _even_mixer(h_c, h_l, cos, sin, ev_w_in[j], ev_lam[j], _diff_lambda_init(layer),
                                   ev_subln_g[j], ev_gk_w2[j], ev_gk_b[j], ev_gla_norm_g[j], ev_w_out[j], not last)
        else:
            y_c, y_l = _odd_mixer(h_c, h_l, cos, sin, od_w_in[j], od_q_norm_g[j], od_kv_norm_g[j], od_w_uq[j],
                                  od_w_ukv[j], lower_bounds[layer], od_hg_norm_g[j], od_w_out[j], not last)
        x = _layernorm(DN_ALPHA * x + m_l[2] * y_l, post_ln_g[layer, 0], post_ln_b[layer, 0])
        h_l = x * (1.0 + m_l[4]) + m_l[3]
        if layer % 2 == 0:
            f_l = _swiglu(h_l, ev_ffn_w1[j], ev_ffn_w2[j])
        else:
            f_l = _moe(h_l, od_router[j], od_exp_w1[j], od_exp_w2[j])
        x = _layernorm(DN_ALPHA * x + m_l[5] * f_l, post_ln_g[layer, 1], post_ln_b[layer, 1])
        if not last:
            ctx = _layernorm(DN_ALPHA * ctx + m_c[2] * y_c, post_ln_g[layer, 0], post_ln_b[layer, 0])
            h_c = ctx * (1.0 + m_c[4]) + m_c[3]
            if layer % 2 == 0:
                f_c = _swiglu(h_c, ev_ffn_w1[j], ev_ffn_w2[j])
            else:
                f_c = _moe(h_c, od_router[j], od_exp_w1[j], od_exp_w2[j])
            ctx = _layernorm(DN_ALPHA * ctx + m_c[5] * f_c, post_ln_g[layer, 1], post_ln_b[layer, 1])
    return x
```

```python
import functools
import math

import jax
import jax.numpy as jnp
import numpy as np
from jax import lax
from jax.experimental import pallas as pl
from jax.experimental.pallas import tpu as pltpu

F32 = jnp.float32
BF16 = jnp.bfloat16

DEPTH = 2
GRID_W = 64
ROT_DIM = 64
ROPE_BASE = 10000.0
DA_HEADS = 4
DA_DH = ROT_DIM
DA_DV = 2 * DA_DH
GLA_HEADS = 4
GLA_DK = 64
GLA_DV = 128
GLA_LR = 16
GLA_NORMALIZER = 16.0
MLA_HEADS = 4
MLA_Q_RANK = 256
MLA_KV_RANK = 128
MLA_NOPE = 128
MLA_ROPE = ROT_DIM
MLA_DV = 128
MLA_SCALE = (MLA_NOPE + MLA_ROPE) ** -0.5
HG_HEADS = 4
HG_DK = 128
HG_DV = 128
D_FF = 3584
N_EXPERTS = 8
TOP_K = 2
LN_EPS = 1e-5
RMS_EPS = 1e-6
DN_ALPHA = (2 * DEPTH) ** 0.25

LANES = 128
VMEM_LIMIT = 56 * 1024 * 1024
SCAN_BLOCK = 256
SCAN_CHUNK = 32
MOE_TM = 512
FFN_TF = 512

EV_QA, EV_KA, EV_VA, EV_QB, EV_KB, EV_VB, EV_GB, EV_LR = 0, 4, 8, 12, 14, 16, 20, 24
EV_NPAD = 25 * LANES
EV_TN = 5 * LANES
OD_CQ, OD_CKV, OD_KR, OD_HQ, OD_FF, OD_FB, OD_HI, OD_HG = 0, 2, 3, 4, 8, 12, 16, 20
OD_NPAD = 24 * LANES
OD_TN = 6 * LANES


def _cparams(sem):
    return pltpu.CompilerParams(dimension_semantics=sem, vmem_limit_bytes=VMEM_LIMIT)


def _silu(v):
    return v * jax.nn.sigmoid(v)


def _layernorm_rows(z, g, b):
    mu = jnp.mean(z, axis=-1, keepdims=True)
    zc = z - mu
    var = jnp.mean(zc * zc, axis=-1, keepdims=True)
    return zc * lax.rsqrt(var + LN_EPS) * g + b


def _rmsnorm_rows(v, g):
    return v * lax.rsqrt(jnp.mean(v * v, axis=-1, keepdims=True) + RMS_EPS) * g


def _dot_nt(a, b):
    return lax.dot_general(a, b, (((1,), (1,)), ((), ())), preferred_element_type=F32)


def _dot_tn(a, b):
    return lax.dot_general(a, b, (((0,), (0,)), ((), ())), preferred_element_type=F32)


def _dot(a, b):
    return jnp.dot(a, b, preferred_element_type=F32)


def _ada_kernel(c_ref, w_ref, b_ref, o_ref):
    s = _silu(c_ref[...]).astype(BF16)
    o_ref[0] = _dot(s, w_ref[0].astype(BF16)) + b_ref[0]


def _ada(cond, ada_w, ada_b):
    depth, d, n = ada_w.shape
    r = cond.shape[0]
    tn = n // 4
    return pl.pallas_call(
        _ada_kernel,
        grid=(depth, n // tn),
        in_specs=[
            pl.BlockSpec((r, d), lambda l, j: (0, 0)),
            pl.BlockSpec((1, d, tn), lambda l, j: (l, 0, j)),
            pl.BlockSpec((1, 1, tn), lambda l, j: (l, 0, j)),
        ],
        out_specs=pl.BlockSpec((1, r, tn), lambda l, j: (l, 0, j)),
        out_shape=jax.ShapeDtypeStruct((depth, r, n), F32),
        compiler_params=_cparams(("parallel", "parallel")),
        name="ada_modulation",
    )(cond, ada_w, ada_b.reshape(depth, 1, n))


def _proj_kernel(x_ref, sc_ref, sh_ref, w_ref, o_ref):
    h = (x_ref[0] * sc_ref[0] + sh_ref[0]).astype(BF16)
    o_ref[0] = _dot(h, w_ref[...])


def _proj(x, scale, shift, w, tn):
    b, t, d = x.shape
    n = w.shape[1]
    tm = min(t, 512)
    return pl.pallas_call(
        _proj_kernel,
        grid=(b, t // tm, n // tn),
        in_specs=[
            pl.BlockSpec((1, tm, d), lambda bi, i, j: (bi, i, 0)),
            pl.BlockSpec((1, 1, d), lambda bi, i, j: (bi, 0, 0)),
            pl.BlockSpec((1, 1, d), lambda bi, i, j: (bi, 0, 0)),
            pl.BlockSpec((d, tn), lambda bi, i, j: (0, j)),
        ],
        out_specs=pl.BlockSpec((1, tm, tn), lambda bi, i, j: (bi, i, j)),
        out_shape=jax.ShapeDtypeStruct((b, t, n), F32),
        compiler_params=_cparams(("parallel", "parallel", "parallel")),
        name="mod_proj",
    )(x, scale, shift, w)


def _rope128(t, cs, sn):
    return t * cs + pltpu.roll(t, LANES // 2, axis=1) * sn


def _softmax_rows(s):
    m = jnp.max(s, axis=-1, keepdims=True)
    e = jnp.exp(s - m)
    return e / jnp.sum(e, axis=-1, keepdims=True)


def _q1_lane_mask(shape):
    lane = lax.broadcasted_iota(jnp.int32, shape, 1)
    return (lane // 32) % 2 == 0


def _diff_scores_out(q, k_bf, v_bf, lam):
    m1 = _q1_lane_mask(q.shape)
    q1 = jnp.where(m1, q, 0.0).astype(BF16)
    q2 = jnp.where(m1, 0.0, q).astype(BF16)
    p1 = _softmax_rows(_dot_nt(q1, k_bf))
    p2 = _softmax_rows(_dot_nt(q2, k_bf))
    return _dot((p1 - lam * p2).astype(BF16), v_bf)


def _diffattn_lat_kernel(lam_init, tc, q_ref, kl_ref, vl_ref, kc_ref, vc_ref, cq_ref, sq_ref, ck_ref, sk_ref,
                         lam_ref, g_ref, o_ref, k_s, v_s):
    @pl.when(pl.program_id(2) == 0)
    def _():
        k_s[0:tc, :] = kc_ref[0].astype(BF16)
        v_s[0:tc, :] = vc_ref[0].astype(BF16)
        k_s[tc:, :] = _rope128(kl_ref[0], ck_ref[...], sk_ref[...]).astype(BF16)
        v_s[tc:, :] = vl_ref[0].astype(BF16)

    q = _rope128(q_ref[0], cq_ref[...], sq_ref[...]) * (DA_DH ** -0.5)
    o = _diff_scores_out(q, k_s[...], v_s[...], lam_ref[0, 0])
    o_ref[0] = _rmsnorm_rows(o, g_ref[...]) * (1.0 - lam_init)


def _diffattn_ctx_kernel(lam_init, q_ref, k_ref, v_ref, lam_ref, g_ref, o_ref):
    q = q_ref[0] * (DA_DH ** -0.5)
    o = _diff_scores_out(q, k_ref[0].astype(BF16), v_ref[0].astype(BF16), lam_ref[0, 0])
    o_ref[0] = _rmsnorm_rows(o, g_ref[...]) * (1.0 - lam_init)


def _diff_attention(p_l, p_c, rope_c, rope_s, lam, subln_g, lam_init, need_ctx):
    b, tl, _ = p_l.shape
    tc = p_c.shape[1]
    h = DA_HEADS
    tq = 256
    lam2 = lam.reshape(1, 1).astype(F32)
    g2 = subln_g.reshape(1, DA_DV).astype(F32)
    smem = pl.BlockSpec(memory_space=pltpu.SMEM)
    o_l = pl.pallas_call(
        functools.partial(_diffattn_lat_kernel, lam_init, tc),
        grid=(b, h, tl // tq),
        in_specs=[
            pl.BlockSpec((1, tq, LANES), lambda bi, hi, i: (bi, i, EV_QA + hi)),
            pl.BlockSpec((1, tl, LANES), lambda bi, hi, i: (bi, 0, EV_KA + hi)),
            pl.BlockSpec((1, tl, LANES), lambda bi, hi, i: (bi, 0, EV_VA + hi)),
            pl.BlockSpec((1, tc, LANES), lambda bi, hi, i: (bi, 0, EV_KA + hi)),
            pl.BlockSpec((1, tc, LANES), lambda bi, hi, i: (bi, 0, EV_VA + hi)),
            pl.BlockSpec((tq, LANES), lambda bi, hi, i: (i, 0)),
            pl.BlockSpec((tq, LANES), lambda bi, hi, i: (i, 0)),
            pl.BlockSpec((tl, LANES), lambda bi, hi, i: (0, 0)),
            pl.BlockSpec((tl, LANES), lambda bi, hi, i: (0, 0)),
            smem,
            pl.BlockSpec((1, LANES), lambda bi, hi, i: (0, 0)),
        ],
        out_specs=pl.BlockSpec((1, tq, LANES), lambda bi, hi, i: (bi, i, hi)),
        out_shape=jax.ShapeDtypeStruct((b, tl, h * DA_DV), F32),
        scratch_shapes=[pltpu.VMEM((tc + tl, LANES), BF16), pltpu.VMEM((tc + tl, LANES), BF16)],
        compiler_params=_cparams(("parallel", "parallel", "arbitrary")),
        name="diff_attention_latent",
    )(p_l, p_l, p_l, p_c, p_c, rope_c, rope_s, rope_c, rope_s, lam2, g2)
    if not need_ctx:
        return None, o_l
    o_c = pl.pallas_call(
        functools.partial(_diffattn_ctx_kernel, lam_init),
        grid=(b, h),
        in_specs=[
            pl.BlockSpec((1, tc, LANES), lambda bi, hi: (bi, 0, EV_QA + hi)),
            pl.BlockSpec((1, tc, LANES), lambda bi, hi: (bi, 0, EV_KA + hi)),
            pl.BlockSpec((1, tc, LANES), lambda bi, hi: (bi, 0, EV_VA + hi)),
            smem,
            pl.BlockSpec((1, LANES), lambda bi, hi: (0, 0)),
        ],
        out_specs=pl.BlockSpec((1, tc, LANES), lambda bi, hi: (bi, 0, hi)),
        out_shape=jax.ShapeDtypeStruct((b, tc, h * DA_DV), F32),
        compiler_params=_cparams(("parallel", "parallel")),
        name="diff_attention_ctx",
    )(p_c, p_c, p_c, lam2, g2)
    return o_c, o_l


def _mla_q(cq, qg, wq, cs, sn):
    q = _dot(_rmsnorm_rows(cq, qg).astype(BF16), wq)
    if cs is not None:
        q = jnp.concatenate([q[:, :LANES], _rope128(q[:, LANES:], cs, sn)], axis=1)
    return (q * MLA_SCALE).astype(BF16)


def _mla_kv(ckv, kr, kvg, wkv, cs, sn):
    kv = _dot(_rmsnorm_rows(ckv, kvg).astype(BF16), wkv)
    if cs is not None:
        kr = _rope128(kr, cs, sn)
    k = jnp.concatenate([kv[:, :LANES], kr], axis=1).astype(BF16)
    return k, kv[:, LANES:].astype(BF16)


def _mla_lat_kernel(tc, cq_ref, ckvl_ref, krl_ref, ckvc_ref, krc_ref, cq_c_ref, cq_s_ref, ck_ref, sk_ref,
                    qg_ref, kvg_ref, wq_ref, wkv_ref, o_ref, k_s, v_s):
    @pl.when(pl.program_id(2) == 0)
    def _():
        kc, vc = _mla_kv(ckvc_ref[0], krc_ref[0], kvg_ref[...], wkv_ref[0], None, None)
        k_s[0:tc, :] = kc
        v_s[0:tc, :] = vc
        kl, vl = _mla_kv(ckvl_ref[0], krl_ref[0], kvg_ref[...], wkv_ref[0], ck_ref[...], sk_ref[...])
        k_s[tc:, :] = kl
        v_s[tc:, :] = vl

    q = _mla_q(cq_ref[0], qg_ref[...], wq_ref[0], cq_c_ref[...], cq_s_ref[...])
    p = _softmax_rows(_dot_nt(q, k_s[...]))
    o_ref[0] = _dot(p.astype(BF16), v_s[...])


def _mla_attention(p_l, p_c, rope_c, rope_s, q_norm_g, kv_norm_g, wq, wkv):
    b, tl, _ = p_l.shape
    tc = p_c.shape[1]
    h = MLA_HEADS
    tq = 256
    return pl.pallas_call(
        functools.partial(_mla_lat_kernel, tc),
        grid=(b, h, tl // tq),
        in_specs=[
            pl.BlockSpec((1, tq, MLA_Q_RANK), lambda bi, hi, i: (bi, i, OD_CQ)),
            pl.BlockSpec((1, tl, LANES), lambda bi, hi, i: (bi, 0, OD_CKV)),
            pl.BlockSpec((1, tl, LANES), lambda bi, hi, i: (bi, 0, OD_KR)),
            pl.BlockSpec((1, tc, LANES), lambda bi, hi, i: (bi, 0, OD_CKV)),
            pl.BlockSpec((1, tc, LANES), lambda bi, hi, i: (bi, 0, OD_KR)),
            pl.BlockSpec((tq, LANES), lambda bi, hi, i: (i, 0)),
            pl.BlockSpec((tq, LANES), lambda bi, hi, i: (i, 0)),
            pl.BlockSpec((tl, LANES), lambda bi, hi, i: (0, 0)),
            pl.BlockSpec((tl, LANES), lambda bi, hi, i: (0, 0)),
            pl.BlockSpec((1, MLA_Q_RANK), lambda bi, hi, i: (0, 0)),
            pl.BlockSpec((1, MLA_KV_RANK), lambda bi, hi, i: (0, 0)),
            pl.BlockSpec((1, MLA_Q_RANK, 2 * LANES), lambda bi, hi, i: (hi, 0, 0)),
            pl.BlockSpec((1, MLA_KV_RANK, 2 * LANES), lambda bi, hi, i: (hi, 0, 0)),
        ],
        out_specs=pl.BlockSpec((1, tq, LANES), lambda bi, hi, i: (bi, i, hi)),
        out_shape=jax.ShapeDtypeStruct((b, tl, h * MLA_DV), F32),
        scratch_shapes=[pltpu.VMEM((tc + tl, 2 * LANES), BF16), pltpu.VMEM((tc + tl, LANES), BF16)],
        compiler_params=_cparams(("parallel", "parallel", "arbitrary")),
        name="mla_attention_latent",
    )(p_l, p_l, p_l, p_c, p_c, rope_c, rope_s, rope_c, rope_s,
      q_norm_g.reshape(1, -1), kv_norm_g.reshape(1, -1), wq, wkv)


def _mla_ctx_kernel(cq_ref, ckv_ref, kr_ref, qg_ref, kvg_ref, wq_ref, wkv_ref, o_ref):
    k, v = _mla_kv(ckv_ref[0], kr_ref[0], kvg_ref[...], wkv_ref[0], None, None)
    q = _mla_q(cq_ref[0], qg_ref[...], wq_ref[0], None, None)
    o_ref[0] = _dot(_softmax_rows(_dot_nt(q, k)).astype(BF16), v)


def _mla_attention_ctx(p_c, q_norm_g, kv_norm_g, wq, wkv):
    b, tc, _ = p_c.shape
    h = MLA_HEADS
    return pl.pallas_call(
        _mla_ctx_kernel,
        grid=(b, h),
        in_specs=[
            pl.BlockSpec((1, tc, MLA_Q_RANK), lambda bi, hi: (bi, 0, OD_CQ)),
            pl.BlockSpec((1, tc, LANES), lambda bi, hi: (bi, 0, OD_CKV)),
            pl.BlockSpec((1, tc, LANES), lambda bi, hi: (bi, 0, OD_KR)),
            pl.BlockSpec((1, MLA_Q_RANK), lambda bi, hi: (0, 0)),
            pl.BlockSpec((1, MLA_KV_RANK), lambda bi, hi: (0, 0)),
            pl.BlockSpec((1, MLA_Q_RANK, 2 * LANES), lambda bi, hi: (hi, 0, 0)),
            pl.BlockSpec((1, MLA_KV_RANK, 2 * LANES), lambda bi, hi: (hi, 0, 0)),
        ],
        out_specs=pl.BlockSpec((1, tc, LANES), lambda bi, hi: (bi, 0, hi)),
        out_shape=jax.ShapeDtypeStruct((b, tc, h * MLA_DV), F32),
        compiler_params=_cparams(("parallel", "parallel")),
        name="mla_attention_ctx",
    )(p_c, p_c, p_c, q_norm_g.reshape(1, -1), kv_norm_g.reshape(1, -1), wq, wkv)


def _scan_levels(bt):
    levels = []
    c = SCAN_CHUNK
    while c <= bt:
        levels.append(c)
        c *= 2
    return levels


def _level_table(bt, reverse):
    i = np.arange(bt)[:, None]
    j = np.arange(bt)[None, :]
    if reverse:
        i, j = j, i
    tab = np.zeros((bt, bt), np.int32)
    for lvl, c in enumerate(_scan_levels(bt), start=1):
        same = (i // c) == (j // c)
        if lvl == 1:
            m = same & (j <= i)
        else:
            m = same & ((i % c) >= c // 2) & ((j % c) < c // 2)
        tab[m] = lvl
    return tab


def _chunk_row(a, c, r):
    bt, n = a.shape
    a3 = a.reshape(bt // c, c, n)
    return jnp.broadcast_to(a3[:, r:r + 1, :], (bt // c, c, n)).reshape(bt, n)


def _scan_block(q_s, k_s, g_s, v_s, o_s, tri_ref, lvl_ref, start, st, reverse, compute_out, accumulate):
    bt = SCAN_BLOCK
    rows = pl.ds(start, bt)
    g = g_s[rows, :]
    k = k_s[rows, :]
    v = v_s[rows, :]
    g_hi = g.astype(BF16)
    g_lo = (g - g_hi.astype(F32)).astype(BF16)
    tri = tri_ref[...]
    gc = _dot(tri, g_hi) + _dot(tri, g_lo)
    g_tot = gc[0:1, :] if reverse else gc[bt - 1:bt, :]
    kd = (k * jnp.exp(g_tot - gc)).astype(BF16)
    st_new = st * jnp.exp(g_tot) + _dot_tn(v, kd)
    if compute_out:
        q = q_s[rows, :]
        o = _dot_nt((q * jnp.exp(gc)).astype(BF16), st.astype(BF16))
        lvl = lvl_ref[...]
        att = jnp.zeros((bt, bt), F32)
        for li, c in enumerate(_scan_levels(bt), start=1):
            if li == 1:
                ref_row = c // 2 if reverse else c // 2 - 1
                r = _chunk_row(gc, c, ref_row)
                eq = jnp.exp(gc - r)
                ek = jnp.exp(r - gc)
            else:
                ref_row = c // 2 if reverse else c // 2 - 1
                r = _chunk_row(gc, c, ref_row)
                eq = jnp.exp(jnp.minimum(gc - r, 0.0))
                ek = jnp.exp(jnp.minimum(r - gc, 0.0))
            a = _dot_nt((q * eq).astype(BF16), (k * ek).astype(BF16))
            att = jnp.where(lvl == li, a, att)
        o = o + _dot(att.astype(BF16), v)
        if accumulate:
            o_s[rows, :] += o
        else:
            o_s[rows, :] = o
    return st_new


def _scan_sweep(q_s, k_s, g_s, v_s, o_s, tri_ref, lvl_ref, tc, tl, reverse, need_ctx):
    bt = SCAN_BLOCK
    nc, nl = tc // bt, tl // bt
    st = jnp.zeros((LANES, LANES), F32)

    def run(first, n, st, compute_out):
        def body(i, st):
            blk = first + (n - 1 - i if reverse else i)
            start = pl.multiple_of(blk * bt, bt)
            return _scan_block(q_s, k_s, g_s, v_s, o_s, tri_ref, lvl_ref, start, st, reverse, compute_out,
                               accumulate=reverse)
        return lax.fori_loop(0, n, body, st)

    st = run(0, nc, st, need_ctx)
    run(nc, nl, st, True)


def _scan_finish(o_s, gate_c_ref, gate_l_ref, ng_ref, oc_ref, ol_ref, tc, need_ctx):
    ng = ng_ref[...]
    if need_ctx:
        oc_ref[0] = _rmsnorm_rows(o_s[0:tc, :], ng) * _silu(gate_c_ref[0])
    ol_ref[0] = _rmsnorm_rows(o_s[tc:, :], ng) * _silu(gate_l_ref[0])


def _gla_scan_kernel(tc, tl, need_ctx, *refs):
    (qc_ref, kc_ref, vc_ref, gbc_ref, lrc_ref, ql_ref, kl_ref, vl_ref, gbl_ref, lrl_ref,
     wgk_ref, bgk_ref, ng_ref, trif_ref, trib_ref, lvlf_ref, lvlb_ref) = refs[:17]
    if need_ctx:
        oc_ref, ol_ref = refs[17:19]
        scratch = refs[19:]
    else:
        oc_ref, ol_ref = None, refs[17]
        scratch = refs[18:]
    q_s, k_s, g_s, v_s, o_s = scratch
    lane = lax.broadcasted_iota(jnp.int32, (1, LANES), 1)
    mine = (lane // GLA_DK) == (pl.program_id(1) % 2)
    q_s[0:tc, :] = jnp.where(mine, qc_ref[0], 0.0) * (GLA_DK ** -0.5)
    q_s[tc:, :] = jnp.where(mine, ql_ref[0], 0.0) * (GLA_DK ** -0.5)
    k_s[0:tc, :] = jnp.where(mine, kc_ref[0], 0.0)
    k_s[tc:, :] = jnp.where(mine, kl_ref[0], 0.0)
    v_s[0:tc, :] = vc_ref[0].astype(BF16)
    v_s[tc:, :] = vl_ref[0].astype(BF16)
    for d, (tri_ref, lvl_ref) in enumerate(((trif_ref, lvlf_ref), (trib_ref, lvlb_ref))):
        w = wgk_ref[0, d]
        bias = bgk_ref[0, d]
        for lr_ref, lo, hi in ((lrc_ref, 0, tc), (lrl_ref, tc, tc + tl)):
            z = _dot(lr_ref[0].astype(BF16), w) + bias
            g_s[lo:hi, :] = jnp.where(mine, jax.nn.log_sigmoid(z) / GLA_NORMALIZER, 0.0)
        _scan_sweep(q_s, k_s, g_s, v_s, o_s, tri_ref, lvl_ref, tc, tl, d == 1, need_ctx)
    _scan_finish(o_s, gbc_ref, gbl_ref, ng_ref, oc_ref, ol_ref, tc, need_ctx)


def _hgrn_scan_kernel(tc, tl, need_ctx, *refs):
    (qc_ref, ffc_ref, fbc_ref, vc_ref, gtc_ref, ql_ref, ffl_ref, fbl_ref, vl_ref, gtl_ref,
     lb_ref, ng_ref, trif_ref, trib_ref, lvlf_ref, lvlb_ref) = refs[:16]
    if need_ctx:
        oc_ref, ol_ref = refs[16:18]
        scratch = refs[18:]
    else:
        oc_ref, ol_ref = None, refs[16]
        scratch = refs[17:]
    q_s, k_s, g_s, v_s, o_s = scratch
    lb = lb_ref[...]
    q_s[0:tc, :] = qc_ref[0]
    q_s[tc:, :] = ql_ref[0]
    v_s[0:tc, :] = vc_ref[0].astype(BF16)
    v_s[tc:, :] = vl_ref[0].astype(BF16)
    dirs = (((ffc_ref, ffl_ref), trif_ref, lvlf_ref), ((fbc_ref, fbl_ref), trib_ref, lvlb_ref))
    for d, ((fc_ref, fl_ref), tri_ref, lvl_ref) in enumerate(dirs):
        for f_ref, lo, hi in ((fc_ref, 0, tc), (fl_ref, tc, tc + tl)):
            f = lb + (1.0 - lb) * jax.nn.sigmoid(f_ref[0])
            k_s[lo:hi, :] = 1.0 - f
            g_s[lo:hi, :] = jnp.log(f)
        _scan_sweep(q_s, k_s, g_s, v_s, o_s, tri_ref, lvl_ref, tc, tl, d == 1, need_ctx)
    _scan_finish(o_s, gtc_ref, gtl_ref, ng_ref, oc_ref, ol_ref, tc, need_ctx)


def _scan_consts():
    bt = SCAN_BLOCK
    lower = np.tril(np.ones((bt, bt), np.float32))
    return (jnp.asarray(lower, BF16), jnp.asarray(lower.T, BF16),
            jnp.asarray(_level_table(bt, False)), jnp.asarray(_level_table(bt, True)))


def _scan_call(kernel_fn, name, p_c, p_l, col_specs, extra, extra_specs, need_ctx, heads):
    b, tl, _ = p_l.shape
    tc = p_c.shape[1]
    bt = SCAN_BLOCK
    t = tc + tl
    consts = _scan_consts()
    const_specs = [pl.BlockSpec((bt, bt), lambda bi, hi: (0, 0)) for _ in consts]
    in_specs = ([pl.BlockSpec((1, tc, LANES), f) for f in col_specs]
                + [pl.BlockSpec((1, tl, LANES), f) for f in col_specs] + extra_specs + const_specs)
    args = [p_c] * len(col_specs) + [p_l] * len(col_specs) + list(extra) + list(consts)
    out_l = jax.ShapeDtypeStruct((b, tl, heads * LANES), F32)
    spec_l = pl.BlockSpec((1, tl, LANES), lambda bi, hi: (bi, 0, hi))
    if need_ctx:
        out_shape = (jax.ShapeDtypeStruct((b, tc, heads * LANES), F32), out_l)
        out_specs = (pl.BlockSpec((1, tc, LANES), lambda bi, hi: (bi, 0, hi)), spec_l)
    else:
        out_shape, out_specs = out_l, spec_l
    res = pl.pallas_call(
        functools.partial(kernel_fn, tc, tl, need_ctx),
        grid=(b, heads),
        in_specs=in_specs,
        out_specs=out_specs,
        out_shape=out_shape,
        scratch_shapes=[pltpu.VMEM((t, LANES), F32), pltpu.VMEM((t, LANES), F32), pltpu.VMEM((t, LANES), F32),
                        pltpu.VMEM((t, LANES), BF16), pltpu.VMEM((t, LANES), F32)],
        compiler_params=_cparams(("parallel", "parallel")),
        name=name,
    )(*args)
    return res if need_ctx else (None, res)


def _gla_scan(p_c, p_l, wgk, bgk, norm_g, need_ctx):
    col_specs = [
        lambda bi, hi: (bi, 0, EV_QB + hi // 2),
        lambda bi, hi: (bi, 0, EV_KB + hi // 2),
        lambda bi, hi: (bi, 0, EV_VB + hi),
        lambda bi, hi: (bi, 0, EV_GB + hi),
        lambda bi, hi: (bi, 0, EV_LR),
    ]
    extra_specs = [
        pl.BlockSpec((1, 2, LANES, LANES), lambda bi, hi: (hi // 2, 0, 0, 0)),
        pl.BlockSpec((1, 2, 1, LANES), lambda bi, hi: (hi // 2, 0, 0, 0)),
        pl.BlockSpec((1, LANES), lambda bi, hi: (0, 0)),
    ]
    return _scan_call(_gla_scan_kernel, "gla_scan", p_c, p_l, col_specs,
                      (wgk, bgk, norm_g.reshape(1, LANES)), extra_specs, need_ctx, GLA_HEADS)


def _hgrn_scan(p_c, p_l, lb, norm_g, need_ctx):
    col_specs = [
        lambda bi, hi: (bi, 0, OD_HQ + hi),
        lambda bi, hi: (bi, 0, OD_FF + hi),
        lambda bi, hi: (bi, 0, OD_FB + hi),
        lambda bi, hi: (bi, 0, OD_HI + hi),
        lambda bi, hi: (bi, 0, OD_HG + hi),
    ]
    extra_specs = [
        pl.BlockSpec((1, LANES), lambda bi, hi: (0, hi)),
        pl.BlockSpec((1, LANES), lambda bi, hi: (0, 0)),
    ]
    return _scan_call(_hgrn_scan_kernel, "hgrn2_scan", p_c, p_l, col_specs,
                      (lb.reshape(1, -1), norm_g.reshape(1, LANES)), extra_specs, need_ctx, HG_HEADS)


def _outproj_kernel(oa_ref, ob_ref, w_ref, x_ref, gate_ref, g_ref, b_ref, o_ref):
    ka = oa_ref.shape[2]
    y = _dot(oa_ref[0].astype(BF16), w_ref[0:ka, :]) + _dot(ob_ref[0].astype(BF16), w_ref[ka:, :])
    z = DN_ALPHA * x_ref[0] + gate_ref[0] * y
    o_ref[0] = _layernorm_rows(z, g_ref[...], b_ref[...])


def _outproj_ln(oa, ob, w, x, gate, ln_g, ln_b):
    b, t, d = x.shape
    ka, kb = oa.shape[2], ob.shape[2]
    tm = min(t, 512)
    return pl.pallas_call(
        _outproj_kernel,
        grid=(b, t // tm),
        in_specs=[
            pl.BlockSpec((1, tm, ka), lambda bi, i: (bi, i, 0)),
            pl.BlockSpec((1, tm, kb), lambda bi, i: (bi, i, 0)),
            pl.BlockSpec((ka + kb, d), lambda bi, i: (0, 0)),
            pl.BlockSpec((1, tm, d), lambda bi, i: (bi, i, 0)),
            pl.BlockSpec((1, 1, d), lambda bi, i: (bi, 0, 0)),
            pl.BlockSpec((1, d), lambda bi, i: (0, 0)),
            pl.BlockSpec((1, d), lambda bi, i: (0, 0)),
        ],
        out_specs=pl.BlockSpec((1, tm, d), lambda bi, i: (bi, i, 0)),
        out_shape=jax.ShapeDtypeStruct((b, t, d), F32),
        compiler_params=_cparams(("parallel", "parallel")),
        name="outproj_residual_ln",
    )(oa, ob, w, x, gate, ln_g.reshape(1, d), ln_b.reshape(1, d))


def _ffn_kernel(x_ref, sc_ref, sh_ref, w1g_ref, w1u_ref, w2_ref, gate_ref, g_ref, b_ref, o_ref, h_s, acc_s):
    f = pl.program_id(2)

    @pl.when(f == 0)
    def _():
        h_s[...] = (x_ref[0] * sc_ref[0] + sh_ref[0]).astype(BF16)
        acc_s[...] = jnp.zeros_like(acc_s)

    h = h_s[...]
    a = _silu(_dot(h, w1g_ref[...])) * _dot(h, w1u_ref[...])
    acc_s[...] += _dot(a.astype(BF16), w2_ref[...])

    @pl.when(f == pl.num_programs(2) - 1)
    def _():
        z = DN_ALPHA * x_ref[0] + gate_ref[0] * acc_s[...]
        o_ref[0] = _layernorm_rows(z, g_ref[...], b_ref[...])


def _ffn_ln(x, scale, shift, w1, w2, gate, ln_g, ln_b):
    b, t, d = x.shape
    ff = w2.shape[0]
    tf = FFN_TF
    nf = ff // tf
    tm = min(t, 1024)
    return pl.pallas_call(
        _ffn_kernel,
        grid=(b, t // tm, nf),
        in_specs=[
            pl.BlockSpec((1, tm, d), lambda bi, i, f: (bi, i, 0)),
            pl.BlockSpec((1, 1, d), lambda bi, i, f: (bi, 0, 0)),
            pl.BlockSpec((1, 1, d), lambda bi, i, f: (bi, 0, 0)),
            pl.BlockSpec((d, tf), lambda bi, i, f: (0, f)),
            pl.BlockSpec((d, tf), lambda bi, i, f: (0, nf + f)),
            pl.BlockSpec((tf, d), lambda bi, i, f: (f, 0)),
            pl.BlockSpec((1, 1, d), lambda bi, i, f: (bi, 0, 0)),
            pl.BlockSpec((1, d), lambda bi, i, f: (0, 0)),
            pl.BlockSpec((1, d), lambda bi, i, f: (0, 0)),
        ],
        out_specs=pl.BlockSpec((1, tm, d), lambda bi, i, f: (bi, i, 0)),
        out_shape=jax.ShapeDtypeStruct((b, t, d), F32),
        scratch_shapes=[pltpu.VMEM((tm, d), BF16), pltpu.VMEM((tm, d), F32)],
        compiler_params=_cparams(("parallel", "parallel", "arbitrary")),
        name="swiglu_residual_ln",
    )(x, scale, shift, w1, w1, w2, gate, ln_g.reshape(1, d), ln_b.reshape(1, d))


ROUTE_E0, ROUTE_E1, ROUTE_G0, ROUTE_G1, ROUTE_R0, ROUTE_R1 = range(6)


def _router_kernel(x_ref, sc_ref, sh_ref, wr_ref, h_ref, info_ref, cnt_ref, carry_s):
    first = (pl.program_id(0) == 0) & (pl.program_id(1) == 0)

    @pl.when(first)
    def _():
        carry_s[...] = jnp.zeros_like(carry_s)

    h = x_ref[0] * sc_ref[0] + sh_ref[0]
    h_ref[0] = h
    tm = h.shape[0]
    logits = jnp.dot(h, wr_ref[...], preferred_element_type=F32, precision=lax.Precision.HIGHEST)
    lane = lax.broadcasted_iota(jnp.int32, (tm, LANES), 1).astype(F32)
    neg = jnp.float32(-jnp.inf)
    logits = jnp.where(lane < N_EXPERTS, logits, neg)
    v0 = jnp.max(logits, axis=-1, keepdims=True)
    e0 = jnp.min(jnp.where(logits == v0, lane, float(LANES)), axis=-1, keepdims=True)
    rest = jnp.where(lane == e0, neg, logits)
    v1 = jnp.max(rest, axis=-1, keepdims=True)
    e1 = jnp.min(jnp.where(rest == v1, lane, float(LANES)), axis=-1, keepdims=True)
    d = jnp.exp(v1 - v0)
    g0 = 1.0 / (1.0 + d)
    g1 = d / (1.0 + d)
    oh0 = (lane == e0).astype(BF16)
    oh1 = (lane == e1).astype(BF16)
    ri = lax.broadcasted_iota(jnp.int32, (tm, tm), 0)
    ci = lax.broadcasted_iota(jnp.int32, (tm, tm), 1)
    before = (ci < ri).astype(BF16)
    c0 = _dot(before, oh0)
    c1 = _dot(before, oh1)
    tot0 = jnp.sum(oh0.astype(F32), axis=0, keepdims=True)
    tot1 = jnp.sum(oh1.astype(F32), axis=0, keepdims=True)
    carry = carry_s[...]
    r0 = jnp.sum(jnp.where(lane == e0, carry + c0, 0.0), axis=-1, keepdims=True)
    r1 = jnp.sum(jnp.where(lane == e1, carry + tot0 + c1, 0.0), axis=-1, keepdims=True)
    carry = carry + tot0 + tot1
    carry_s[...] = carry
    cnt_ref[...] = carry
    info = jnp.zeros((tm, LANES), F32)
    for col, val in ((ROUTE_E0, e0), (ROUTE_E1, e1), (ROUTE_G0, g0), (ROUTE_G1, g1), (ROUTE_R0, r0), (ROUTE_R1, r1)):
        info = jnp.where(lane == col, val, info)
    info_ref[0] = info


def _router(x, scale, shift, w_router):
    b, t, d = x.shape
    tm = min(t, 512)
    wr = jnp.zeros((d, LANES), F32).at[:, :N_EXPERTS].set(w_router.astype(F32))
    return pl.pallas_call(
        _router_kernel,
        grid=(b, t // tm),
        in_specs=[
            pl.BlockSpec((1, tm, d), lambda bi, i: (bi, i, 0)),
            pl.BlockSpec((1, 1, d), lambda bi, i: (bi, 0, 0)),
            pl.BlockSpec((1, 1, d), lambda bi, i: (bi, 0, 0)),
            pl.BlockSpec((d, LANES), lambda bi, i: (0, 0)),
        ],
        out_specs=(
            pl.BlockSpec((1, tm, d), lambda bi, i: (bi, i, 0)),
            pl.BlockSpec((1, tm, LANES), lambda bi, i: (bi, i, 0)),
            pl.BlockSpec((1, LANES), lambda bi, i: (0, 0)),
        ),
        out_shape=(
            jax.ShapeDtypeStruct((b, t, d), F32),
            jax.ShapeDtypeStruct((b, t, LANES), F32),
            jax.ShapeDtypeStruct((1, LANES), F32),
        ),
        scratch_shapes=[pltpu.VMEM((1, LANES), F32)],
        compiler_params=_cparams(("arbitrary", "arbitrary")),
        name="moe_router",
    )(x, scale, shift, wr)


DISPATCH_ROWS = 512


def _dispatch_kernel(pos0_ref, pos1_ref, h_ref, zero_ref, hs_ref, sem):
    del zero_ref
    base = pl.program_id(0) * DISPATCH_ROWS

    def row_copy(r, pos_ref):
        return pltpu.make_async_copy(h_ref.at[pl.ds(base + r, 1), :], hs_ref.at[pl.ds(pos_ref[base + r], 1), :],
                                     sem)

    def start(r, carry):
        row_copy(r, pos0_ref).start()
        row_copy(r, pos1_ref).start()
        return carry

    def wait(r, carry):
        row_copy(r, pos0_ref).wait()
        row_copy(r, pos1_ref).wait()
        return carry

    lax.fori_loop(0, DISPATCH_ROWS, start, 0)
    lax.fori_loop(0, DISPATCH_ROWS, wait, 0)


def _dispatch(h2, pos0, pos1, p_pad):
    n, d = h2.shape
    grid_spec = pltpu.PrefetchScalarGridSpec(
        num_scalar_prefetch=2,
        grid=(n // DISPATCH_ROWS,),
        in_specs=[pl.BlockSpec(memory_space=pl.ANY), pl.BlockSpec(memory_space=pl.ANY)],
        out_specs=pl.BlockSpec(memory_space=pl.ANY),
        scratch_shapes=[pltpu.SemaphoreType.DMA],
    )
    return pl.pallas_call(
        _dispatch_kernel,
        grid_spec=grid_spec,
        out_shape=jax.ShapeDtypeStruct((p_pad, d), F32),
        input_output_aliases={3: 0},
        compiler_params=_cparams(("arbitrary",)),
        name="moe_dispatch",
    )(pos0, pos1, h2, jnp.zeros((p_pad, d), F32))


def _expert_ffn_kernel(te_ref, nu_ref, hs_ref, w1g_ref, w1u_ref, w2_ref, ys_ref, h_s, acc_s):
    t = pl.program_id(0)
    f = pl.program_id(1)

    @pl.when(t < nu_ref[0])
    def _():
        @pl.when(f == 0)
        def _():
            h_s[...] = hs_ref[...].astype(BF16)
            acc_s[...] = jnp.zeros_like(acc_s)

        h = h_s[...]
        a = _silu(_dot(h, w1g_ref[0])) * _dot(h, w1u_ref[0])
        acc_s[...] += _dot(a.astype(BF16), w2_ref[0])

        @pl.when(f == pl.num_programs(1) - 1)
        def _():
            ys_ref[...] = acc_s[...]

    @pl.when((t >= nu_ref[0]) & (f == 0))
    def _():
        ys_ref[...] = jnp.zeros_like(ys_ref)


def _expert_ffn(hs, tile_e, n_used, w1, w2):
    p_pad, d = hs.shape
    ff = w2.shape[1]
    tm, tf = MOE_TM, FFN_TF
    nf = ff // tf
    nt = p_pad // tm

    def tile(t, nu):
        return jnp.minimum(t, nu[0] - 1)

    def ftile(t, f, nu):
        return jnp.where(t < nu[0], f, nf - 1)

    grid_spec = pltpu.PrefetchScalarGridSpec(
        num_scalar_prefetch=2,
        grid=(nt, nf),
        in_specs=[
            pl.BlockSpec((tm, d), lambda t, f, te, nu: (tile(t, nu), 0)),
            pl.BlockSpec((1, d, tf), lambda t, f, te, nu: (te[tile(t, nu)], 0, ftile(t, f, nu))),
            pl.BlockSpec((1, d, tf), lambda t, f, te, nu: (te[tile(t, nu)], 0, nf + ftile(t, f, nu))),
            pl.BlockSpec((1, tf, d), lambda t, f, te, nu: (te[tile(t, nu)], ftile(t, f, nu), 0)),
        ],
        out_specs=pl.BlockSpec((tm, d), lambda t, f, te, nu: (t, 0)),
        scratch_shapes=[pltpu.VMEM((tm, d), BF16), pltpu.VMEM((tm, d), F32)],
    )
    return pl.pallas_call(
        _expert_ffn_kernel,
        grid_spec=grid_spec,
        out_shape=jax.ShapeDtypeStruct((p_pad, d), F32),
        compiler_params=_cparams(("arbitrary", "arbitrary")),
        name="moe_expert_ffn",
    )(tile_e, n_used, hs, w1, w1, w2)


def _combine_kernel(pos0_ref, pos1_ref, ys_ref, x_ref, info_ref, gate_ref, g_ref, b_ref, o_ref, y0_s, y1_s, sem):
    tm = y0_s.shape[0]
    base = (pl.program_id(0) * pl.num_programs(1) + pl.program_id(1)) * tm

    def row_copy(r, pos_ref, dst):
        return pltpu.make_async_copy(ys_ref.at[pl.ds(pos_ref[base + r], 1), :], dst.at[pl.ds(r, 1), :], sem)

    def start(r, carry):
        row_copy(r, pos0_ref, y0_s).start()
        row_copy(r, pos1_ref, y1_s).start()
        return carry

    def wait(r, carry):
        row_copy(r, pos0_ref, y0_s).wait()
        row_copy(r, pos1_ref, y1_s).wait()
        return carry

    lax.fori_loop(0, tm, start, 0)
    lax.fori_loop(0, tm, wait, 0)
    info = info_ref[0]
    g0 = info[:, ROUTE_G0:ROUTE_G0 + 1]
    g1 = info[:, ROUTE_G1:ROUTE_G1 + 1]
    f = g0 * y0_s[...] + g1 * y1_s[...]
    z = DN_ALPHA * x_ref[0] + gate_ref[0] * f
    o_ref[0] = _layernorm_rows(z, g_ref[...], b_ref[...])


def _combine_ln(ys, pos0, pos1, x, info, gate, ln_g, ln_b):
    b, t, d = x.shape
    tm = 256
    grid_spec = pltpu.PrefetchScalarGridSpec(
        num_scalar_prefetch=2,
        grid=(b, t // tm),
        in_specs=[
            pl.BlockSpec(memory_space=pl.ANY),
            pl.BlockSpec((1, tm, d), lambda bi, i, p0, p1: (bi, i, 0)),
            pl.BlockSpec((1, tm, LANES), lambda bi, i, p0, p1: (bi, i, 0)),
            pl.BlockSpec((1, 1, d), lambda bi, i, p0, p1: (bi, 0, 0)),
            pl.BlockSpec((1, d), lambda bi, i, p0, p1: (0, 0)),
            pl.BlockSpec((1, d), lambda bi, i, p0, p1: (0, 0)),
        ],
        out_specs=pl.BlockSpec((1, tm, d), lambda bi, i, p0, p1: (bi, i, 0)),
        scratch_shapes=[pltpu.VMEM((tm, d), F32), pltpu.VMEM((tm, d), F32), pltpu.SemaphoreType.DMA],
    )
    return pl.pallas_call(
        _combine_kernel,
        grid_spec=grid_spec,
        out_shape=jax.ShapeDtypeStruct((b, t, d), F32),
        compiler_params=_cparams(("arbitrary", "arbitrary")),
        name="moe_combine_ln",
    )(pos0, pos1, ys, x, info, gate, ln_g.reshape(1, d), ln_b.reshape(1, d))


def _moe_ln(x, scale, shift, w_router, w1, w2, gate, ln_g, ln_b):
    b, t, d = x.shape
    n = b * t
    tm = MOE_TM
    h, info, counts = _router(x, scale, shift, w_router)
    cnt = counts[0, :N_EXPERTS].astype(jnp.int32)
    padded = ((cnt + tm - 1) // tm) * tm
    ends = jnp.cumsum(padded)
    starts = ends - padded
    info2 = info.reshape(n, LANES)
    e0 = info2[:, ROUTE_E0].astype(jnp.int32)
    e1 = info2[:, ROUTE_E1].astype(jnp.int32)
    pos0 = starts[e0] + info2[:, ROUTE_R0].astype(jnp.int32)
    pos1 = starts[e1] + info2[:, ROUTE_R1].astype(jnp.int32)
    n_tiles = (TOP_K * n) // tm + N_EXPERTS
    tile_start = jnp.arange(n_tiles, dtype=jnp.int32) * tm
    tile_e = jnp.minimum(jnp.sum(tile_start[:, None] >= ends[None, :], axis=1), N_EXPERTS - 1).astype(jnp.int32)
    n_used = (ends[-1] // tm).astype(jnp.int32).reshape(1)
    hs = _dispatch(h.reshape(n, d), pos0, pos1, n_tiles * tm)
    ys = _expert_ffn(hs, tile_e, n_used, w1, w2)
    return _combine_ln(ys, pos0, pos1, x, info, gate, ln_g, ln_b)


def _pair_perm(comp_offsets):
    even = np.concatenate([off + np.arange(0, ROT_DIM, 2) for off in comp_offsets])
    odd = np.concatenate([off + np.arange(1, ROT_DIM, 2) for off in comp_offsets])
    return np.concatenate([even, odd])


def _even_w_in(w_in):
    d = w_in.shape[0]
    a_qk = DA_HEADS * 2 * DA_DH
    head_perm = _pair_perm((0, DA_DH))
    qk_perm = np.concatenate([hh * 2 * DA_DH + head_perm for hh in range(DA_HEADS)])
    cols = np.concatenate([qk_perm, a_qk + qk_perm, np.arange(2 * a_qk, w_in.shape[1])])
    w = w_in[:, cols]
    return jnp.pad(w, ((0, 0), (0, EV_NPAD - w.shape[1]))).astype(BF16)


def _odd_w_in(w_in):
    o_ckv = MLA_Q_RANK
    o_kr = o_ckv + MLA_KV_RANK
    o_rest = o_kr + MLA_ROPE
    ev = o_kr + np.arange(0, MLA_ROPE, 2)
    od = o_kr + np.arange(1, MLA_ROPE, 2)
    cols = np.concatenate([np.arange(0, o_kr), ev, ev, od, od, np.arange(o_rest, w_in.shape[1])])
    return w_in[:, cols].astype(BF16)


def _mla_weights(w_uq, w_ukv):
    hq = MLA_NOPE + MLA_ROPE
    q3 = w_uq.reshape(MLA_Q_RANK, MLA_HEADS, hq).transpose(1, 0, 2)
    zeros = jnp.zeros((MLA_HEADS, MLA_Q_RANK, MLA_ROPE // 2), w_uq.dtype)
    rope = q3[:, :, MLA_NOPE:]
    wq = jnp.concatenate([q3[:, :, :MLA_NOPE], rope[:, :, 0::2], zeros, rope[:, :, 1::2], zeros], axis=-1)
    wkv = w_ukv.reshape(MLA_KV_RANK, MLA_HEADS, MLA_NOPE + MLA_DV).transpose(1, 0, 2)
    return wq.astype(BF16), wkv.astype(BF16)


def _gla_gate_weights(gk_w2, gk_b):
    pairs = GLA_HEADS // 2
    w = jnp.zeros((pairs, 2, LANES, LANES), F32)
    for d in range(2):
        blk = gk_w2[d].reshape(GLA_LR, pairs, LANES).transpose(1, 0, 2)
        w = w.at[:, d, d * GLA_LR:(d + 1) * GLA_LR, :].set(blk)
    bias = gk_b.reshape(2, pairs, 1, LANES).transpose(1, 0, 2, 3).astype(F32)
    return w.astype(BF16), bias


def _rope_tables(rows):
    n_freq = ROT_DIM // 4
    inv = ROPE_BASE ** (-jnp.arange(n_freq, dtype=F32) / n_freq)
    row = jnp.repeat(jnp.arange(rows, dtype=F32), GRID_W)
    col = jnp.tile(jnp.arange(GRID_W, dtype=F32), rows)
    ang = jnp.concatenate([row[:, None] * inv, col[:, None] * inv], axis=-1)
    cos, sin = jnp.cos(ang), jnp.sin(ang)
    return jnp.concatenate([cos] * 4, axis=-1), jnp.concatenate([-sin, -sin, sin, sin], axis=-1)


def _diff_lambda_init(layer):
    return 0.8 - 0.6 * math.exp(-0.3 * layer)


def kernel(x, c, ctx, c_ctx, ada_w, ada_b, post_ln_g, post_ln_b, lb_table, ev_w_in, ev_lam, ev_subln_g, ev_gk_w2,
           ev_gk_b, ev_gla_norm_g, ev_w_out, ev_ffn_w1, ev_ffn_w2, od_w_in, od_q_norm_g, od_kv_norm_g, od_w_uq,
           od_w_ukv, od_hg_norm_g, od_w_out, od_router, od_exp_w1, od_exp_w2):
    b, t, d = x.shape
    tc = ctx.shape[1]
    rope_c, rope_s = _rope_tables(t // GRID_W)
    lb_soft = jax.nn.softmax(lb_table.astype(F32), axis=0)
    lower_bounds = jnp.cumsum(lb_soft, axis=0) - lb_soft[0]

    n_cond = ((b + 1 + 7) // 8) * 8
    cond = jnp.zeros((n_cond, d), F32).at[:b].set(c).at[b].set(c_ctx)
    mods = _ada(cond, ada_w, ada_b).reshape(DEPTH, n_cond, 6, d)

    ctx_flat = None
    for layer in range(DEPTH):
        last = layer == DEPTH - 1
        j = layer // 2
        m_l = [mods[layer, :b, i][:, None, :] for i in range(6)]
        m_c = [jnp.broadcast_to(mods[layer, b, i][None, None, :], (b, 1, d)) for i in range(6)]
        m_c1 = [m[:1] for m in m_c]
        even = layer % 2 == 0
        if even:
            w_in = _even_w_in(ev_w_in[j])
            tn = EV_TN
        else:
            w_in = _odd_w_in(od_w_in[j])
            tn = OD_TN
        p_l = _proj(x, 1.0 + m_l[1], m_l[0], w_in, tn)
        p_c = _proj(ctx, 1.0 + m_c[1], m_c[0], w_in, tn)
        need_ctx = not last
        if even:
            lam_init = _diff_lambda_init(layer)
            lv = ev_lam[j].astype(F32)
            lam = jnp.exp(jnp.sum(lv[0] * lv[1])) - jnp.exp(jnp.sum(lv[2] * lv[3])) + lam_init
            oa_c, oa_l = _diff_attention(p_l, p_c, rope_c, rope_s, lam, ev_subln_g[j], lam_init, need_ctx)
            wgk, bgk = _gla_gate_weights(ev_gk_w2[j], ev_gk_b[j])
            ob_c, ob_l = _gla_scan(p_c, p_l, wgk, bgk, ev_gla_norm_g[j], need_ctx)
            w_out = ev_w_out[j].astype(BF16)
        else:
            wq, wkv = _mla_weights(od_w_uq[j], od_w_ukv[j])
            oa_l = _mla_attention(p_l, p_c, rope_c, rope_s, od_q_norm_g[j], od_kv_norm_g[j], wq, wkv)
            oa_c = _mla_attention_ctx(p_c, od_q_norm_g[j], od_kv_norm_g[j], wq, wkv) if need_ctx else None
            ob_c, ob_l = _hgrn_scan(p_c, p_l, lower_bounds[layer], od_hg_norm_g[j], need_ctx)
            w_out = od_w_out[j].astype(BF16)
        g0, b0 = post_ln_g[layer, 0], post_ln_b[layer, 0]
        g1, b1 = post_ln_g[layer, 1], post_ln_b[layer, 1]
        x = _outproj_ln(oa_l, ob_l, w_out, x, m_l[2], g0, b0)
        if even:
            w1, w2 = ev_ffn_w1[j].astype(BF16), ev_ffn_w2[j].astype(BF16)
            x = _ffn_ln(x, 1.0 + m_l[4], m_l[3], w1, w2, m_l[5], g1, b1)
        else:
            w1, w2 = od_exp_w1[j].astype(BF16), od_exp_w2[j].astype(BF16)
            x = _moe_ln(x, 1.0 + m_l[4], m_l[3], od_router[j], w1, w2, m_l[5], g1, b1)
        if need_ctx:
            ctx = _outproj_ln(oa_c, ob_c, w_out, ctx, m_c[2], g0, b0)
            ctx_flat = ctx.reshape(1, b * tc, d)
            if even:
                ctx_flat = _ffn_ln(ctx_flat, 1.0 + m_c1[4], m_c1[3], w1, w2, m_c1[5], g1, b1)
            else:
                ctx_flat = _moe_ln(ctx_flat, 1.0 + m_c1[4], m_c1[3], od_router[j], w1, w2, m_c1[5], g1, b1)
            ctx = ctx_flat.reshape(b, tc, d)
    return x
```

```python
import functools
import math

import jax
import jax.numpy as jnp
import numpy as np
from jax import lax
from jax.experimental import pallas as pl
from jax.experimental.pallas import tpu as pltpu

F32 = jnp.float32
BF16 = jnp.bfloat16

DEPTH = 2
GRID_W = 64
ROT_DIM = 64
ROPE_BASE = 10000.0
DA_HEADS = 4
DA_DH = ROT_DIM
DA_DV = 2 * DA_DH
GLA_HEADS = 4
GLA_DK = 64
GLA_DV = 128
GLA_LR = 16
GLA_NORMALIZER = 16.0
MLA_HEADS = 4
MLA_Q_RANK = 256
MLA_KV_RANK = 128
MLA_NOPE = 128
MLA_ROPE = ROT_DIM
MLA_DV = 128
MLA_SCALE = (MLA_NOPE + MLA_ROPE) ** -0.5
HG_HEADS = 4
HG_DK = 128
HG_DV = 128
D_FF = 3584
N_EXPERTS = 8
TOP_K = 2
LN_EPS = 1e-5
RMS_EPS = 1e-6
DN_ALPHA = (2 * DEPTH) ** 0.25

LANES = 128
VMEM_LIMIT = 56 * 1024 * 1024
SCAN_BLOCK = 256
SCAN_CHUNK = 32
MOE_TM = 512
FFN_TF = 512

EV_QA, EV_KA, EV_VA, EV_QB, EV_KB, EV_VB, EV_GB, EV_LR = 0, 4, 8, 12, 14, 16, 20, 24
EV_NPAD = 25 * LANES
OD_CQ, OD_CKV, OD_KR, OD_HQ, OD_FF, OD_FB, OD_HI, OD_HG = 0, 2, 3, 4, 8, 12, 16, 20
OD_NPAD = 24 * LANES


def _cparams(sem):
    return pltpu.CompilerParams(dimension_semantics=sem, vmem_limit_bytes=VMEM_LIMIT)


def _silu(v):
    return v * jax.nn.sigmoid(v)


def _layernorm_rows(z, g, b):
    mu = jnp.mean(z, axis=-1, keepdims=True)
    zc = z - mu
    var = jnp.mean(zc * zc, axis=-1, keepdims=True)
    return zc * lax.rsqrt(var + LN_EPS) * g + b


def _rmsnorm_rows(v, g):
    return v * lax.rsqrt(jnp.mean(v * v, axis=-1, keepdims=True) + RMS_EPS) * g


def _dot_nt(a, b):
    return lax.dot_general(a, b, (((1,), (1,)), ((), ())), preferred_element_type=F32)


def _dot_tn(a, b):
    return lax.dot_general(a, b, (((0,), (0,)), ((), ())), preferred_element_type=F32)


def _dot(a, b):
    return jnp.dot(a, b, preferred_element_type=F32)


def _ada_kernel(c_ref, w_ref, b_ref, o_ref):
    s = _silu(c_ref[...]).astype(BF16)
    o_ref[0] = _dot(s, w_ref[0].astype(BF16)) + b_ref[0]


def _ada(cond, ada_w, ada_b):
    depth, d, n = ada_w.shape
    r = cond.shape[0]
    tn = n // 4
    return pl.pallas_call(
        _ada_kernel,
        grid=(depth, n // tn),
        in_specs=[
            pl.BlockSpec((r, d), lambda l, j: (0, 0)),
            pl.BlockSpec((1, d, tn), lambda l, j: (l, 0, j)),
            pl.BlockSpec((1, 1, tn), lambda l, j: (l, 0, j)),
        ],
        out_specs=pl.BlockSpec((1, r, tn), lambda l, j: (l, 0, j)),
        out_shape=jax.ShapeDtypeStruct((depth, r, n), F32),
        compiler_params=_cparams(("parallel", "parallel")),
        name="ada_modulation",
    )(cond, ada_w, ada_b.reshape(depth, 1, n))


def _proj_kernel(x_ref, sc_ref, sh_ref, w_ref, o_ref):
    h = (x_ref[0] * sc_ref[0] + sh_ref[0]).astype(BF16)
    o_ref[0] = _dot(h, w_ref[...]).astype(o_ref.dtype)


def _proj(x, scale, shift, w):
    b, t, d = x.shape
    n = w.shape[1]
    tm = min(t, 512)
    return pl.pallas_call(
        _proj_kernel,
        grid=(b, t // tm),
        in_specs=[
            pl.BlockSpec((1, tm, d), lambda bi, i: (bi, i, 0)),
            pl.BlockSpec((1, 1, d), lambda bi, i: (bi, 0, 0)),
            pl.BlockSpec((1, 1, d), lambda bi, i: (bi, 0, 0)),
            pl.BlockSpec((d, n), lambda bi, i: (0, 0)),
        ],
        out_specs=pl.BlockSpec((1, tm, n), lambda bi, i: (bi, i, 0)),
        out_shape=jax.ShapeDtypeStruct((b, t, n), BF16),
        compiler_params=_cparams(("parallel", "parallel")),
        name="mod_proj",
    )(x, scale, shift, w)


def _rope128(t, cs, sn):
    t = t.astype(F32)
    return t * cs + pltpu.roll(t, LANES // 2, axis=1) * sn


def _softmax_rows(s):
    m = jnp.max(s, axis=-1, keepdims=True)
    e = jnp.exp(s - m)
    return e / jnp.sum(e, axis=-1, keepdims=True)


def _q1_lane_mask(shape):
    lane = lax.broadcasted_iota(jnp.int32, shape, 1)
    return (lane // 32) % 2 == 0


def _diff_scores_out(q, k_bf, v_bf, lam):
    m1 = _q1_lane_mask(q.shape)
    q1 = jnp.where(m1, q, 0.0).astype(BF16)
    q2 = jnp.where(m1, 0.0, q).astype(BF16)
    p1 = _softmax_rows(_dot_nt(q1, k_bf))
    p2 = _softmax_rows(_dot_nt(q2, k_bf))
    return _dot((p1 - lam * p2).astype(BF16), v_bf)


def _diffattn_lat_kernel(lam_init, tc, q_ref, kl_ref, vl_ref, kc_ref, vc_ref, cq_ref, sq_ref, ck_ref, sk_ref,
                         lam_ref, g_ref, o_ref, k_s, v_s):
    @pl.when(pl.program_id(2) == 0)
    def _():
        k_s[0:tc, :] = kc_ref[0].astype(BF16)
        v_s[0:tc, :] = vc_ref[0].astype(BF16)
        k_s[tc:, :] = _rope128(kl_ref[0], ck_ref[...], sk_ref[...]).astype(BF16)
        v_s[tc:, :] = vl_ref[0].astype(BF16)

    q = _rope128(q_ref[0], cq_ref[...], sq_ref[...]) * (DA_DH ** -0.5)
    o = _diff_scores_out(q, k_s[...], v_s[...], lam_ref[0, 0])
    o_ref[0] = (_rmsnorm_rows(o, g_ref[...]) * (1.0 - lam_init)).astype(o_ref.dtype)


def _diffattn_ctx_kernel(lam_init, q_ref, k_ref, v_ref, lam_ref, g_ref, o_ref):
    q = q_ref[0].astype(F32) * (DA_DH ** -0.5)
    o = _diff_scores_out(q, k_ref[0].astype(BF16), v_ref[0].astype(BF16), lam_ref[0, 0])
    o_ref[0] = (_rmsnorm_rows(o, g_ref[...]) * (1.0 - lam_init)).astype(o_ref.dtype)


def _diff_attention(p_l, p_c, rope_c, rope_s, lam, subln_g, lam_init, need_ctx):
    b, tl, _ = p_l.shape
    tc = p_c.shape[1]
    h = DA_HEADS
    tq = 256
    lam2 = lam.reshape(1, 1).astype(F32)
    g2 = subln_g.reshape(1, DA_DV).astype(F32)
    smem = pl.BlockSpec(memory_space=pltpu.SMEM)
    o_l = pl.pallas_call(
        functools.partial(_diffattn_lat_kernel, lam_init, tc),
        grid=(b, h, tl // tq),
        in_specs=[
            pl.BlockSpec((1, tq, LANES), lambda bi, hi, i: (bi, i, EV_QA + hi)),
            pl.BlockSpec((1, tl, LANES), lambda bi, hi, i: (bi, 0, EV_KA + hi)),
            pl.BlockSpec((1, tl, LANES), lambda bi, hi, i: (bi, 0, EV_VA + hi)),
            pl.BlockSpec((1, tc, LANES), lambda bi, hi, i: (bi, 0, EV_KA + hi)),
            pl.BlockSpec((1, tc, LANES), lambda bi, hi, i: (bi, 0, EV_VA + hi)),
            pl.BlockSpec((tq, LANES), lambda bi, hi, i: (i, 0)),
            pl.BlockSpec((tq, LANES), lambda bi, hi, i: (i, 0)),
            pl.BlockSpec((tl, LANES), lambda bi, hi, i: (0, 0)),
            pl.BlockSpec((tl, LANES), lambda bi, hi, i: (0, 0)),
            smem,
            pl.BlockSpec((1, LANES), lambda bi, hi, i: (0, 0)),
        ],
        out_specs=pl.BlockSpec((1, tq, LANES), lambda bi, hi, i: (bi, i, hi)),
        out_shape=jax.ShapeDtypeStruct((b, tl, h * DA_DV), BF16),
        scratch_shapes=[pltpu.VMEM((tc + tl, LANES), BF16), pltpu.VMEM((tc + tl, LANES), BF16)],
        compiler_params=_cparams(("parallel", "parallel", "arbitrary")),
        name="diff_attention_latent",
    )(p_l, p_l, p_l, p_c, p_c, rope_c, rope_s, rope_c, rope_s, lam2, g2)
    if not need_ctx:
        return None, o_l
    o_c = pl.pallas_call(
        functools.partial(_diffattn_ctx_kernel, lam_init),
        grid=(b, h),
        in_specs=[
            pl.BlockSpec((1, tc, LANES), lambda bi, hi: (bi, 0, EV_QA + hi)),
            pl.BlockSpec((1, tc, LANES), lambda bi, hi: (bi, 0, EV_KA + hi)),
            pl.BlockSpec((1, tc, LANES), lambda bi, hi: (bi, 0, EV_VA + hi)),
            smem,
            pl.BlockSpec((1, LANES), lambda bi, hi: (0, 0)),
        ],
        out_specs=pl.BlockSpec((1, tc, LANES), lambda bi, hi: (bi, 0, hi)),
        out_shape=jax.ShapeDtypeStruct((b, tc, h * DA_DV), BF16),
        compiler_params=_cparams(("parallel", "parallel")),
        name="diff_attention_ctx",
    )(p_c, p_c, p_c, lam2, g2)
    return o_c, o_l


def _mla_q(cq, qg, wq, cs, sn):
    q = _dot(_rmsnorm_rows(cq.astype(F32), qg).astype(BF16), wq)
    if cs is not None:
        q = jnp.concatenate([q[:, :LANES], _rope128(q[:, LANES:], cs, sn)], axis=1)
    return (q * MLA_SCALE).astype(BF16)


def _mla_kv(ckv, kr, kvg, wkv, cs, sn):
    kv = _dot(_rmsnorm_rows(ckv.astype(F32), kvg).astype(BF16), wkv)
    if cs is not None:
        kr = _rope128(kr, cs, sn)
    k = jnp.concatenate([kv[:, :LANES].astype(BF16), kr.astype(BF16)], axis=1)
    return k, kv[:, LANES:].astype(BF16)


def _mla_lat_kernel(tc, cq_ref, ckvl_ref, krl_ref, ckvc_ref, krc_ref, cq_c_ref, cq_s_ref, ck_ref, sk_ref,
                    qg_ref, kvg_ref, wq_ref, wkv_ref, o_ref, k_s, v_s):
    @pl.when(pl.program_id(2) == 0)
    def _():
        kc, vc = _mla_kv(ckvc_ref[0], krc_ref[0], kvg_ref[...], wkv_ref[0], None, None)
        k_s[0:tc, :] = kc
        v_s[0:tc, :] = vc
        kl, vl = _mla_kv(ckvl_ref[0], krl_ref[0], kvg_ref[...], wkv_ref[0], ck_ref[...], sk_ref[...])
        k_s[tc:, :] = kl
        v_s[tc:, :] = vl

    q = _mla_q(cq_ref[0], qg_ref[...], wq_ref[0], cq_c_ref[...], cq_s_ref[...])
    p = _softmax_rows(_dot_nt(q, k_s[...]))
    o_ref[0] = _dot(p.astype(BF16), v_s[...]).astype(o_ref.dtype)


def _mla_attention(p_l, p_c, rope_c, rope_s, q_norm_g, kv_norm_g, wq, wkv):
    b, tl, _ = p_l.shape
    tc = p_c.shape[1]
    h = MLA_HEADS
    tq = 256
    return pl.pallas_call(
        functools.partial(_mla_lat_kernel, tc),
        grid=(b, h, tl // tq),
        in_specs=[
            pl.BlockSpec((1, tq, MLA_Q_RANK), lambda bi, hi, i: (bi, i, OD_CQ)),
            pl.BlockSpec((1, tl, LANES), lambda bi, hi, i: (bi, 0, OD_CKV)),
            pl.BlockSpec((1, tl, LANES), lambda bi, hi, i: (bi, 0, OD_KR)),
            pl.BlockSpec((1, tc, LANES), lambda bi, hi, i: (bi, 0, OD_CKV)),
            pl.BlockSpec((1, tc, LANES), lambda bi, hi, i: (bi, 0, OD_KR)),
            pl.BlockSpec((tq, LANES), lambda bi, hi, i: (i, 0)),
            pl.BlockSpec((tq, LANES), lambda bi, hi, i: (i, 0)),
            pl.BlockSpec((tl, LANES), lambda bi, hi, i: (0, 0)),
            pl.BlockSpec((tl, LANES), lambda bi, hi, i: (0, 0)),
            pl.BlockSpec((1, MLA_Q_RANK), lambda bi, hi, i: (0, 0)),
            pl.BlockSpec((1, MLA_KV_RANK), lambda bi, hi, i: (0, 0)),
            pl.BlockSpec((1, MLA_Q_RANK, 2 * LANES), lambda bi, hi, i: (hi, 0, 0)),
            pl.BlockSpec((1, MLA_KV_RANK, 2 * LANES), lambda bi, hi, i: (hi, 0, 0)),
        ],
        out_specs=pl.BlockSpec((1, tq, LANES), lambda bi, hi, i: (bi, i, hi)),
        out_shape=jax.ShapeDtypeStruct((b, tl, h * MLA_DV), BF16),
        scratch_shapes=[pltpu.VMEM((tc + tl, 2 * LANES), BF16), pltpu.VMEM((tc + tl, LANES), BF16)],
        compiler_params=_cparams(("parallel", "parallel", "arbitrary")),
        name="mla_attention_latent",
    )(p_l, p_l, p_l, p_c, p_c, rope_c, rope_s, rope_c, rope_s,
      q_norm_g.reshape(1, -1), kv_norm_g.reshape(1, -1), wq, wkv)


def _mla_ctx_kernel(cq_ref, ckv_ref, kr_ref, qg_ref, kvg_ref, wq_ref, wkv_ref, o_ref):
    k, v = _mla_kv(ckv_ref[0], kr_ref[0], kvg_ref[...], wkv_ref[0], None, None)
    q = _mla_q(cq_ref[0], qg_ref[...], wq_ref[0], None, None)
    o_ref[0] = _dot(_softmax_rows(_dot_nt(q, k)).astype(BF16), v).astype(o_ref.dtype)


def _mla_attention_ctx(p_c, q_norm_g, kv_norm_g, wq, wkv):
    b, tc, _ = p_c.shape
    h = MLA_HEADS
    return pl.pallas_call(
        _mla_ctx_kernel,
        grid=(b, h),
        in_specs=[
            pl.BlockSpec((1, tc, MLA_Q_RANK), lambda bi, hi: (bi, 0, OD_CQ)),
            pl.BlockSpec((1, tc, LANES), lambda bi, hi: (bi, 0, OD_CKV)),
            pl.BlockSpec((1, tc, LANES), lambda bi, hi: (bi, 0, OD_KR)),
            pl.BlockSpec((1, MLA_Q_RANK), lambda bi, hi: (0, 0)),
            pl.BlockSpec((1, MLA_KV_RANK), lambda bi, hi: (0, 0)),
            pl.BlockSpec((1, MLA_Q_RANK, 2 * LANES), lambda bi, hi: (hi, 0, 0)),
            pl.BlockSpec((1, MLA_KV_RANK, 2 * LANES), lambda bi, hi: (hi, 0, 0)),
        ],
        out_specs=pl.BlockSpec((1, tc, LANES), lambda bi, hi: (bi, 0, hi)),
        out_shape=jax.ShapeDtypeStruct((b, tc, h * MLA_DV), BF16),
        compiler_params=_cparams(("parallel", "parallel")),
        name="mla_attention_ctx",
    )(p_c, p_c, p_c, q_norm_g.reshape(1, -1), kv_norm_g.reshape(1, -1), wq, wkv)


def _scan_levels(bt):
    levels = []
    c = SCAN_CHUNK
    while c <= bt:
        levels.append(c)
        c *= 2
    return levels


def _level_table(bt, reverse):
    i = np.arange(bt)[:, None]
    j = np.arange(bt)[None, :]
    if reverse:
        i, j = j, i
    tab = np.zeros((bt, bt), np.int32)
    for lvl, c in enumerate(_scan_levels(bt), start=1):
        same = (i // c) == (j // c)
        if lvl == 1:
            m = same & (j <= i)
        else:
            m = same & ((i % c) >= c // 2) & ((j % c) < c // 2)
        tab[m] = lvl
    return tab


def _chunk_row(a, c, r):
    bt, n = a.shape
    a3 = a.reshape(bt // c, c, n)
    return jnp.broadcast_to(a3[:, r:r + 1, :], (bt // c, c, n)).reshape(bt, n)


def _scan_block(q_s, k_s, g_s, v_s, o_s, tri_ref, lvl_ref, start, st, reverse, compute_out, accumulate):
    bt = SCAN_BLOCK
    rows = pl.ds(start, bt)
    g = g_s[rows, :]
    k = k_s[rows, :]
    v = v_s[rows, :]
    g_hi = g.astype(BF16)
    g_lo = (g - g_hi.astype(F32)).astype(BF16)
    tri = tri_ref[...]
    gc = _dot(tri, g_hi) + _dot(tri, g_lo)
    g_tot = gc[0:1, :] if reverse else gc[bt - 1:bt, :]
    kd = (k * jnp.exp(g_tot - gc)).astype(BF16)
    st_new = st * jnp.exp(g_tot) + _dot_tn(v, kd)
    if compute_out:
        q = q_s[rows, :]
        o = _dot_nt((q * jnp.exp(gc)).astype(BF16), st.astype(BF16))
        lvl = lvl_ref[...]
        att = jnp.zeros((bt, bt), F32)
        for li, c in enumerate(_scan_levels(bt), start=1):
            if li == 1:
                ref_row = c // 2 if reverse else c // 2 - 1
                r = _chunk_row(gc, c, ref_row)
                eq = jnp.exp(gc - r)
                ek = jnp.exp(r - gc)
            else:
                ref_row = c // 2 if reverse else c // 2 - 1
                r = _chunk_row(gc, c, ref_row)
                eq = jnp.exp(jnp.minimum(gc - r, 0.0))
                ek = jnp.exp(jnp.minimum(r - gc, 0.0))
            a = _dot_nt((q * eq).astype(BF16), (k * ek).astype(BF16))
            att = jnp.where(lvl == li, a, att)
        o = o + _dot(att.astype(BF16), v)
        if accumulate:
            o_s[rows, :] += o
        else:
            o_s[rows, :] = o
    return st_new


def _scan_sweep(q_s, k_s, g_s, v_s, o_s, tri_ref, lvl_ref, tc, tl, reverse, need_ctx):
    bt = SCAN_BLOCK
    nc, nl = tc // bt, tl // bt
    st = jnp.zeros((LANES, LANES), F32)

    def run(first, n, st, compute_out):
        def body(i, st):
            blk = first + (n - 1 - i if reverse else i)
            start = pl.multiple_of(blk * bt, bt)
            return _scan_block(q_s, k_s, g_s, v_s, o_s, tri_ref, lvl_ref, start, st, reverse, compute_out,
                               accumulate=reverse)
        return lax.fori_loop(0, n, body, st)

    st = run(0, nc, st, need_ctx)
    run(nc, nl, st, True)


def _scan_finish(o_s, gate_c_ref, gate_l_ref, ng_ref, oc_ref, ol_ref, tc, need_ctx):
    ng = ng_ref[...]
    if need_ctx:
        oc_ref[0] = (_rmsnorm_rows(o_s[0:tc, :], ng) * _silu(gate_c_ref[0].astype(F32))).astype(oc_ref.dtype)
    ol_ref[0] = (_rmsnorm_rows(o_s[tc:, :], ng) * _silu(gate_l_ref[0].astype(F32))).astype(ol_ref.dtype)


def _gla_scan_kernel(tc, tl, need_ctx, *refs):
    (qc_ref, kc_ref, vc_ref, gbc_ref, lrc_ref, ql_ref, kl_ref, vl_ref, gbl_ref, lrl_ref,
     wgk_ref, bgk_ref, ng_ref, trif_ref, trib_ref, lvlf_ref, lvlb_ref) = refs[:17]
    if need_ctx:
        oc_ref, ol_ref = refs[17:19]
        scratch = refs[19:]
    else:
        oc_ref, ol_ref = None, refs[17]
        scratch = refs[18:]
    q_s, k_s, g_s, v_s, o_s = scratch
    lane = lax.broadcasted_iota(jnp.int32, (1, LANES), 1)
    mine = (lane // GLA_DK) == (pl.program_id(1) % 2)
    q_s[0:tc, :] = jnp.where(mine, qc_ref[0].astype(F32), 0.0) * (GLA_DK ** -0.5)
    q_s[tc:, :] = jnp.where(mine, ql_ref[0].astype(F32), 0.0) * (GLA_DK ** -0.5)
    k_s[0:tc, :] = jnp.where(mine, kc_ref[0].astype(F32), 0.0)
    k_s[tc:, :] = jnp.where(mine, kl_ref[0].astype(F32), 0.0)
    v_s[0:tc, :] = vc_ref[0].astype(BF16)
    v_s[tc:, :] = vl_ref[0].astype(BF16)
    for d, (tri_ref, lvl_ref) in enumerate(((trif_ref, lvlf_ref), (trib_ref, lvlb_ref))):
        w = wgk_ref[0, d]
        bias = bgk_ref[0, d]
        for lr_ref, lo, hi in ((lrc_ref, 0, tc), (lrl_ref, tc, tc + tl)):
            z = _dot(lr_ref[0].astype(BF16), w) + bias
            g_s[lo:hi, :] = jnp.where(mine, jax.nn.log_sigmoid(z) / GLA_NORMALIZER, 0.0)
        _scan_sweep(q_s, k_s, g_s, v_s, o_s, tri_ref, lvl_ref, tc, tl, d == 1, need_ctx)
    _scan_finish(o_s, gbc_ref, gbl_ref, ng_ref, oc_ref, ol_ref, tc, need_ctx)


def _hgrn_scan_kernel(tc, tl, need_ctx, *refs):
    (qc_ref, ffc_ref, fbc_ref, vc_ref, gtc_ref, ql_ref, ffl_ref, fbl_ref, vl_ref, gtl_ref,
     lb_ref, ng_ref, trif_ref, trib_ref, lvlf_ref, lvlb_ref) = refs[:16]
    if need_ctx:
        oc_ref, ol_ref = refs[16:18]
        scratch = refs[18:]
    else:
        oc_ref, ol_ref = None, refs[16]
        scratch = refs[17:]
    q_s, k_s, g_s, v_s, o_s = scratch
    lb = lb_ref[...]
    q_s[0:tc, :] = qc_ref[0].astype(F32)
    q_s[tc:, :] = ql_ref[0].astype(F32)
    v_s[0:tc, :] = vc_ref[0].astype(BF16)
    v_s[tc:, :] = vl_ref[0].astype(BF16)
    dirs = (((ffc_ref, ffl_ref), trif_ref, lvlf_ref), ((fbc_ref, fbl_ref), trib_ref, lvlb_ref))
    for d, ((fc_ref, fl_ref), tri_ref, lvl_ref) in enumerate(dirs):
        for f_ref, lo, hi in ((fc_ref, 0, tc), (fl_ref, tc, tc + tl)):
            f = lb + (1.0 - lb) * jax.nn.sigmoid(f_ref[0].astype(F32))
            k_s[lo:hi, :] = 1.0 - f
            g_s[lo:hi, :] = jnp.log(f)
        _scan_sweep(q_s, k_s, g_s, v_s, o_s, tri_ref, lvl_ref, tc, tl, d == 1, need_ctx)
    _scan_finish(o_s, gtc_ref, gtl_ref, ng_ref, oc_ref, ol_ref, tc, need_ctx)


def _scan_consts():
    bt = SCAN_BLOCK
    lower = np.tril(np.ones((bt, bt), np.float32))
    return (jnp.asarray(lower, BF16), jnp.asarray(lower.T, BF16),
            jnp.asarray(_level_table(bt, False)), jnp.asarray(_level_table(bt, True)))


def _scan_call(kernel_fn, name, p_c, p_l, col_specs, extra, extra_specs, need_ctx, heads):
    b, tl, _ = p_l.shape
    tc = p_c.shape[1]
    bt = SCAN_BLOCK
    t = tc + tl
    consts = _scan_consts()
    const_specs = [pl.BlockSpec((bt, bt), lambda bi, hi: (0, 0)) for _ in consts]
    in_specs = ([pl.BlockSpec((1, tc, LANES), f) for f in col_specs]
                + [pl.BlockSpec((1, tl, LANES), f) for f in col_specs] + extra_specs + const_specs)
    args = [p_c] * len(col_specs) + [p_l] * len(col_specs) + list(extra) + list(consts)
    out_l = jax.ShapeDtypeStruct((b, tl, heads * LANES), BF16)
    spec_l = pl.BlockSpec((1, tl, LANES), lambda bi, hi: (bi, 0, hi))
    if need_ctx:
        out_shape = (jax.ShapeDtypeStruct((b, tc, heads * LANES), BF16), out_l)
        out_specs = (pl.BlockSpec((1, tc, LANES), lambda bi, hi: (bi, 0, hi)), spec_l)
    else:
        out_shape, out_specs = out_l, spec_l
    res = pl.pallas_call(
        functools.partial(kernel_fn, tc, tl, need_ctx),
        grid=(b, heads),
        in_specs=in_specs,
        out_specs=out_specs,
        out_shape=out_shape,
        scratch_shapes=[pltpu.VMEM((t, LANES), F32), pltpu.VMEM((t, LANES), F32), pltpu.VMEM((t, LANES), F32),
                        pltpu.VMEM((t, LANES), BF16), pltpu.VMEM((t, LANES), F32)],
        compiler_params=_cparams(("parallel", "parallel")),
        name=name,
    )(*args)
    return res if need_ctx else (None, res)


def _gla_scan(p_c, p_l, wgk, bgk, norm_g, need_ctx):
    col_specs = [
        lambda bi, hi: (bi, 0, EV_QB + hi // 2),
        lambda bi, hi: (bi, 0, EV_KB + hi // 2),
        lambda bi, hi: (bi, 0, EV_VB + hi),
        lambda bi, hi: (bi, 0, EV_GB + hi),
        lambda bi, hi: (bi, 0, EV_LR),
    ]
    extra_specs = [
        pl.BlockSpec((1, 2, LANES, LANES), lambda bi, hi: (hi // 2, 0, 0, 0)),
        pl.BlockSpec((1, 2, 1, LANES), lambda bi, hi: (hi // 2, 0, 0, 0)),
        pl.BlockSpec((1, LANES), lambda bi, hi: (0, 0)),
    ]
    return _scan_call(_gla_scan_kernel, "gla_scan", p_c, p_l, col_specs,
                      (wgk, bgk, norm_g.reshape(1, LANES)), extra_specs, need_ctx, GLA_HEADS)


def _hgrn_scan(p_c, p_l, lb, norm_g, need_ctx):
    col_specs = [
        lambda bi, hi: (bi, 0, OD_HQ + hi),
        lambda bi, hi: (bi, 0, OD_FF + hi),
        lambda bi, hi: (bi, 0, OD_FB + hi),
        lambda bi, hi: (bi, 0, OD_HI + hi),
        lambda bi, hi: (bi, 0, OD_HG + hi),
    ]
    extra_specs = [
        pl.BlockSpec((1, LANES), lambda bi, hi: (0, hi)),
        pl.BlockSpec((1, LANES), lambda bi, hi: (0, 0)),
    ]
    return _scan_call(_hgrn_scan_kernel, "hgrn2_scan", p_c, p_l, col_specs,
                      (lb.reshape(1, -1), norm_g.reshape(1, LANES)), extra_specs, need_ctx, HG_HEADS)


def _outproj_kernel(oa_ref, ob_ref, w_ref, x_ref, gate_ref, g_ref, b_ref, o_ref):
    ka = oa_ref.shape[2]
    y = _dot(oa_ref[0], w_ref[0:ka, :]) + _dot(ob_ref[0], w_ref[ka:, :])
    z = DN_ALPHA * x_ref[0] + gate_ref[0] * y
    o_ref[0] = _layernorm_rows(z, g_ref[...], b_ref[...])


def _outproj_ln(oa, ob, w, x, gate, ln_g, ln_b):
    b, t, d = x.shape
    ka, kb = oa.shape[2], ob.shape[2]
    tm = min(t, 512)
    return pl.pallas_call(
        _outproj_kernel,
        grid=(b, t // tm),
        in_specs=[
            pl.BlockSpec((1, tm, ka), lambda bi, i: (bi, i, 0)),
            pl.BlockSpec((1, tm, kb), lambda bi, i: (bi, i, 0)),
            pl.BlockSpec((ka + kb, d), lambda bi, i: (0, 0)),
            pl.BlockSpec((1, tm, d), lambda bi, i: (bi, i, 0)),
            pl.BlockSpec((1, 1, d), lambda bi, i: (bi, 0, 0)),
            pl.BlockSpec((1, d), lambda bi, i: (0, 0)),
            pl.BlockSpec((1, d), lambda bi, i: (0, 0)),
        ],
        out_specs=pl.BlockSpec((1, tm, d), lambda bi, i: (bi, i, 0)),
        out_shape=jax.ShapeDtypeStruct((b, t, d), F32),
        compiler_params=_cparams(("parallel", "parallel")),
        name="outproj_residual_ln",
    )(oa, ob, w, x, gate, ln_g.reshape(1, d), ln_b.reshape(1, d))


def _ffn_kernel(x_ref, sc_ref, sh_ref, w1g_ref, w1u_ref, w2_ref, gate_ref, g_ref, b_ref, o_ref, h_s, acc_s):
    f = pl.program_id(2)

    @pl.when(f == 0)
    def _():
        h_s[...] = (x_ref[0] * sc_ref[0] + sh_ref[0]).astype(BF16)
        acc_s[...] = jnp.zeros_like(acc_s)

    h = h_s[...]
    a = _silu(_dot(h, w1g_ref[...])) * _dot(h, w1u_ref[...])
    acc_s[...] += _dot(a.astype(BF16), w2_ref[...])

    @pl.when(f == pl.num_programs(2) - 1)
    def _():
        z = DN_ALPHA * x_ref[0] + gate_ref[0] * acc_s[...]
        o_ref[0] = _layernorm_rows(z, g_ref[...], b_ref[...])


def _ffn_ln(x, scale, shift, w1, w2, gate, ln_g, ln_b):
    b, t, d = x.shape
    ff = w2.shape[0]
    tf = FFN_TF
    nf = ff // tf
    tm = min(t, 1024)
    return pl.pallas_call(
        _ffn_kernel,
        grid=(b, t // tm, nf),
        in_specs=[
            pl.BlockSpec((1, tm, d), lambda bi, i, f: (bi, i, 0)),
            pl.BlockSpec((1, 1, d), lambda bi, i, f: (bi, 0, 0)),
            pl.BlockSpec((1, 1, d), lambda bi, i, f: (bi, 0, 0)),
            pl.BlockSpec((d, tf), lambda bi, i, f: (0, f)),
            pl.BlockSpec((d, tf), lambda bi, i, f: (0, nf + f)),
            pl.BlockSpec((tf, d), lambda bi, i, f: (f, 0)),
            pl.BlockSpec((1, 1, d), lambda bi, i, f: (bi, 0, 0)),
            pl.BlockSpec((1, d), lambda bi, i, f: (0, 0)),
            pl.BlockSpec((1, d), lambda bi, i, f: (0, 0)),
        ],
        out_specs=pl.BlockSpec((1, tm, d), lambda bi, i, f: (bi, i, 0)),
        out_shape=jax.ShapeDtypeStruct((b, t, d), F32),
        scratch_shapes=[pltpu.VMEM((tm, d), BF16), pltpu.VMEM((tm, d), F32)],
        compiler_params=_cparams(("parallel", "parallel", "arbitrary")),
        name="swiglu_residual_ln",
    )(x, scale, shift, w1, w1, w2, gate, ln_g.reshape(1, d), ln_b.reshape(1, d))


ROUTE_E0, ROUTE_E1, ROUTE_G0, ROUTE_G1, ROUTE_R0, ROUTE_R1 = range(6)


def _router_kernel(x_ref, sc_ref, sh_ref, wr_ref, h_ref, info_ref, cnt_ref, carry_s):
    first = (pl.program_id(0) == 0) & (pl.program_id(1) == 0)

    @pl.when(first)
    def _():
        carry_s[...] = jnp.zeros_like(carry_s)

    h = x_ref[0] * sc_ref[0] + sh_ref[0]
    h_ref[0] = h
    tm = h.shape[0]
    logits = jnp.dot(h, wr_ref[...], preferred_element_type=F32, precision=lax.Precision.HIGHEST)
    lane = lax.broadcasted_iota(jnp.int32, (tm, LANES), 1).astype(F32)
    neg = jnp.float32(-jnp.inf)
    logits = jnp.where(lane < N_EXPERTS, logits, neg)
    v0 = jnp.max(logits, axis=-1, keepdims=True)
    e0 = jnp.min(jnp.where(logits == v0, lane, float(LANES)), axis=-1, keepdims=True)
    rest = jnp.where(lane == e0, neg, logits)
    v1 = jnp.max(rest, axis=-1, keepdims=True)
    e1 = jnp.min(jnp.where(rest == v1, lane, float(LANES)), axis=-1, keepdims=True)
    d = jnp.exp(v1 - v0)
    g0 = 1.0 / (1.0 + d)
    g1 = d / (1.0 + d)
    oh0 = (lane == e0).astype(BF16)
    oh1 = (lane == e1).astype(BF16)
    ri = lax.broadcasted_iota(jnp.int32, (tm, tm), 0)
    ci = lax.broadcasted_iota(jnp.int32, (tm, tm), 1)
    before = (ci < ri).astype(BF16)
    c0 = _dot(before, oh0)
    c1 = _dot(before, oh1)
    tot0 = jnp.sum(oh0.astype(F32), axis=0, keepdims=True)
    tot1 = jnp.sum(oh1.astype(F32), axis=0, keepdims=True)
    carry = carry_s[...]
    r0 = jnp.sum(jnp.where(lane == e0, carry + c0, 0.0), axis=-1, keepdims=True)
    r1 = jnp.sum(jnp.where(lane == e1, carry + tot0 + c1, 0.0), axis=-1, keepdims=True)
    carry = carry + tot0 + tot1
    carry_s[...] = carry
    cnt_ref[...] = carry
    info = jnp.zeros((tm, LANES), F32)
    for col, val in ((ROUTE_E0, e0), (ROUTE_E1, e1), (ROUTE_G0, g0), (ROUTE_G1, g1), (ROUTE_R0, r0), (ROUTE_R1, r1)):
        info = jnp.where(lane == col, val, info)
    info_ref[0] = info


def _router(x, scale, shift, w_router):
    b, t, d = x.shape
    tm = min(t, 512)
    wr = jnp.zeros((d, LANES), F32).at[:, :N_EXPERTS].set(w_router.astype(F32))
    return pl.pallas_call(
        _router_kernel,
        grid=(b, t // tm),
        in_specs=[
            pl.BlockSpec((1, tm, d), lambda bi, i: (bi, i, 0)),
            pl.BlockSpec((1, 1, d), lambda bi, i: (bi, 0, 0)),
            pl.BlockSpec((1, 1, d), lambda bi, i: (bi, 0, 0)),
            pl.BlockSpec((d, LANES), lambda bi, i: (0, 0)),
        ],
        out_specs=(
            pl.BlockSpec((1, tm, d), lambda bi, i: (bi, i, 0)),
            pl.BlockSpec((1, tm, LANES), lambda bi, i: (bi, i, 0)),
            pl.BlockSpec((1, LANES), lambda bi, i: (0, 0)),
        ),
        out_shape=(
            jax.ShapeDtypeStruct((b, t, d), F32),
            jax.ShapeDtypeStruct((b, t, LANES), F32),
            jax.ShapeDtypeStruct((1, LANES), F32),
        ),
        scratch_shapes=[pltpu.VMEM((1, LANES), F32)],
        compiler_params=_cparams(("arbitrary", "arbitrary")),
        name="moe_router",
    )(x, scale, shift, wr)


DISPATCH_ROWS = 512


def _dispatch_kernel(pos0_ref, pos1_ref, h_ref, zero_ref, hs_ref, sem):
    del zero_ref
    base = pl.program_id(0) * DISPATCH_ROWS

    def row_copy(r, pos_ref):
        return pltpu.make_async_copy(h_ref.at[pl.ds(r, 1), :], hs_ref.at[pl.ds(pos_ref[base + r], 1), :], sem)

    def start(r, carry):
        row_copy(r, pos0_ref).start()
        row_copy(r, pos1_ref).start()
        return carry

    def wait(r, carry):
        row_copy(r, pos0_ref).wait()
        row_copy(r, pos1_ref).wait()
        return carry

    lax.fori_loop(0, DISPATCH_ROWS, start, 0, unroll=8)
    lax.fori_loop(0, DISPATCH_ROWS, wait, 0, unroll=8)


def _dispatch(h2, pos0, pos1, p_pad):
    n, d = h2.shape
    grid_spec = pltpu.PrefetchScalarGridSpec(
        num_scalar_prefetch=2,
        grid=(n // DISPATCH_ROWS,),
        in_specs=[pl.BlockSpec((DISPATCH_ROWS, d), lambda i, p0, p1: (i, 0)), pl.BlockSpec(memory_space=pl.ANY)],
        out_specs=pl.BlockSpec(memory_space=pl.ANY),
        scratch_shapes=[pltpu.SemaphoreType.DMA],
    )
    return pl.pallas_call(
        _dispatch_kernel,
        grid_spec=grid_spec,
        out_shape=jax.ShapeDtypeStruct((p_pad, d), F32),
        input_output_aliases={3: 0},
        compiler_params=_cparams(("arbitrary",)),
        name="moe_dispatch",
    )(pos0, pos1, h2, jnp.zeros((p_pad, d), F32))


def _expert_ffn_kernel(te_ref, nu_ref, hs_ref, w1g_ref, w1u_ref, w2_ref, ys_ref, h_s, acc_s):
    t = pl.program_id(0)
    f = pl.program_id(1)

    @pl.when(t < nu_ref[0])
    def _():
        @pl.when(f == 0)
        def _():
            h_s[...] = hs_ref[...].astype(BF16)
            acc_s[...] = jnp.zeros_like(acc_s)

        h = h_s[...]
        a = _silu(_dot(h, w1g_ref[0])) * _dot(h, w1u_ref[0])
        acc_s[...] += _dot(a.astype(BF16), w2_ref[0])

        @pl.when(f == pl.num_programs(1) - 1)
        def _():
            ys_ref[...] = acc_s[...]

    @pl.when((t >= nu_ref[0]) & (f == 0))
    def _():
        ys_ref[...] = jnp.zeros_like(ys_ref)


def _expert_ffn(hs, tile_e, n_used, w1, w2):
    p_pad, d = hs.shape
    ff = w2.shape[1]
    tm, tf = MOE_TM, FFN_TF
    nf = ff // tf
    nt = p_pad // tm

    def tile(t, nu):
        return jnp.minimum(t, nu[0] - 1)

    def ftile(t, f, nu):
        return jnp.where(t < nu[0], f, nf - 1)

    grid_spec = pltpu.PrefetchScalarGridSpec(
        num_scalar_prefetch=2,
        grid=(nt, nf),
        in_specs=[
            pl.BlockSpec((tm, d), lambda t, f, te, nu: (tile(t, nu), 0)),
            pl.BlockSpec((1, d, tf), lambda t, f, te, nu: (te[tile(t, nu)], 0, ftile(t, f, nu))),
            pl.BlockSpec((1, d, tf), lambda t, f, te, nu: (te[tile(t, nu)], 0, nf + ftile(t, f, nu))),
            pl.BlockSpec((1, tf, d), lambda t, f, te, nu: (te[tile(t, nu)], ftile(t, f, nu), 0)),
        ],
        out_specs=pl.BlockSpec((tm, d), lambda t, f, te, nu: (t, 0)),
        scratch_shapes=[pltpu.VMEM((tm, d), BF16), pltpu.VMEM((tm, d), F32)],
    )
    return pl.pallas_call(
        _expert_ffn_kernel,
        grid_spec=grid_spec,
        out_shape=jax.ShapeDtypeStruct((p_pad, d), F32),
        compiler_params=_cparams(("arbitrary", "arbitrary")),
        name="moe_expert_ffn",
    )(tile_e, n_used, hs, w1, w1, w2)


def _combine_kernel(pos0_ref, pos1_ref, ys_ref, x_ref, info_ref, gate_ref, g_ref, b_ref, o_ref, y0_s, y1_s, sem):
    tm = y0_s.shape[0]
    base = (pl.program_id(0) * pl.num_programs(1) + pl.program_id(1)) * tm

    def row_copy(r, pos_ref, dst):
        return pltpu.make_async_copy(ys_ref.at[pl.ds(pos_ref[base + r], 1), :], dst.at[pl.ds(r, 1), :], sem)

    def start(r, carry):
        row_copy(r, pos0_ref, y0_s).start()
        row_copy(r, pos1_ref, y1_s).start()
        return carry

    def wait(r, carry):
        row_copy(r, pos0_ref, y0_s).wait()
        row_copy(r, pos1_ref, y1_s).wait()
        return carry

    lax.fori_loop(0, tm, start, 0, unroll=8)
    lax.fori_loop(0, tm, wait, 0, unroll=8)
    info = info_ref[0]
    g0 = info[:, ROUTE_G0:ROUTE_G0 + 1]
    g1 = info[:, ROUTE_G1:ROUTE_G1 + 1]
    f = g0 * y0_s[...] + g1 * y1_s[...]
    z = DN_ALPHA * x_ref[0] + gate_ref[0] * f
    o_ref[0] = _layernorm_rows(z, g_ref[...], b_ref[...])


def _combine_ln(ys, pos0, pos1, x, info, gate, ln_g, ln_b):
    b, t, d = x.shape
    tm = 256
    grid_spec = pltpu.PrefetchScalarGridSpec(
        num_scalar_prefetch=2,
        grid=(b, t // tm),
        in_specs=[
            pl.BlockSpec(memory_space=pl.ANY),
            pl.BlockSpec((1, tm, d), lambda bi, i, p0, p1: (bi, i, 0)),
            pl.BlockSpec((1, tm, LANES), lambda bi, i, p0, p1: (bi, i, 0)),
            pl.BlockSpec((1, 1, d), lambda bi, i, p0, p1: (bi, 0, 0)),
            pl.BlockSpec((1, d), lambda bi, i, p0, p1: (0, 0)),
            pl.BlockSpec((1, d), lambda bi, i, p0, p1: (0, 0)),
        ],
        out_specs=pl.BlockSpec((1, tm, d), lambda bi, i, p0, p1: (bi, i, 0)),
        scratch_shapes=[pltpu.VMEM((tm, d), F32), pltpu.VMEM((tm, d), F32), pltpu.SemaphoreType.DMA],
    )
    return pl.pallas_call(
        _combine_kernel,
        grid_spec=grid_spec,
        out_shape=jax.ShapeDtypeStruct((b, t, d), F32),
        compiler_params=_cparams(("arbitrary", "arbitrary")),
        name="moe_combine_ln",
    )(pos0, pos1, ys, x, info, gate, ln_g.reshape(1, d), ln_b.reshape(1, d))


def _moe_ln(x, scale, shift, w_router, w1, w2, gate, ln_g, ln_b):
    b, t, d = x.shape
    n = b * t
    tm = MOE_TM
    h, info, counts = _router(x, scale, shift, w_router)
    cnt = counts[0, :N_EXPERTS].astype(jnp.int32)
    padded = ((cnt + tm - 1) // tm) * tm
    ends = jnp.cumsum(padded)
    starts = ends - padded
    info2 = info.reshape(n, LANES)
    e0 = info2[:, ROUTE_E0].astype(jnp.int32)
    e1 = info2[:, ROUTE_E1].astype(jnp.int32)
    pos0 = starts[e0] + info2[:, ROUTE_R0].astype(jnp.int32)
    pos1 = starts[e1] + info2[:, ROUTE_R1].astype(jnp.int32)
    n_tiles = (TOP_K * n) // tm + N_EXPERTS
    tile_start = jnp.arange(n_tiles, dtype=jnp.int32) * tm
    tile_e = jnp.minimum(jnp.sum(tile_start[:, None] >= ends[None, :], axis=1), N_EXPERTS - 1).astype(jnp.int32)
    n_used = (ends[-1] // tm).astype(jnp.int32).reshape(1)
    hs = _dispatch(h.reshape(n, d), pos0, pos1, n_tiles * tm)
    ys = _expert_ffn(hs, tile_e, n_used, w1, w2)
    return _combine_ln(ys, pos0, pos1, x, info, gate, ln_g, ln_b)


def _pair_perm(comp_offsets):
    even = np.concatenate([off + np.arange(0, ROT_DIM, 2) for off in comp_offsets])
    odd = np.concatenate([off + np.arange(1, ROT_DIM, 2) for off in comp_offsets])
    return np.concatenate([even, odd])


def _even_w_in(w_in):
    d = w_in.shape[0]
    a_qk = DA_HEADS * 2 * DA_DH
    head_perm = _pair_perm((0, DA_DH))
    qk_perm = np.concatenate([hh * 2 * DA_DH + head_perm for hh in range(DA_HEADS)])
    cols = np.concatenate([qk_perm, a_qk + qk_perm, np.arange(2 * a_qk, w_in.shape[1])])
    w = w_in[:, cols]
    return jnp.pad(w, ((0, 0), (0, EV_NPAD - w.shape[1]))).astype(BF16)


def _odd_w_in(w_in):
    o_ckv = MLA_Q_RANK
    o_kr = o_ckv + MLA_KV_RANK
    o_rest = o_kr + MLA_ROPE
    ev = o_kr + np.arange(0, MLA_ROPE, 2)
    od = o_kr + np.arange(1, MLA_ROPE, 2)
    cols = np.concatenate([np.arange(0, o_kr), ev, ev, od, od, np.arange(o_rest, w_in.shape[1])])
    return w_in[:, cols].astype(BF16)


def _mla_weights(w_uq, w_ukv):
    hq = MLA_NOPE + MLA_ROPE
    q3 = w_uq.reshape(MLA_Q_RANK, MLA_HEADS, hq).transpose(1, 0, 2)
    zeros = jnp.zeros((MLA_HEADS, MLA_Q_RANK, MLA_ROPE // 2), w_uq.dtype)
    rope = q3[:, :, MLA_NOPE:]
    wq = jnp.concatenate([q3[:, :, :MLA_NOPE], rope[:, :, 0::2], zeros, rope[:, :, 1::2], zeros], axis=-1)
    wkv = w_ukv.reshape(MLA_KV_RANK, MLA_HEADS, MLA_NOPE + MLA_DV).transpose(1, 0, 2)
    return wq.astype(BF16), wkv.astype(BF16)


def _gla_gate_weights(gk_w2, gk_b):
    pairs = GLA_HEADS // 2
    w = jnp.zeros((pairs, 2, LANES, LANES), F32)
    for d in range(2):
        blk = gk_w2[d].reshape(GLA_LR, pairs, LANES).transpose(1, 0, 2)
        w = w.at[:, d, d * GLA_LR:(d + 1) * GLA_LR, :].set(blk)
    bias = gk_b.reshape(2, pairs, 1, LANES).transpose(1, 0, 2, 3).astype(F32)
    return w.astype(BF16), bias


def _rope_tables(rows):
    n_freq = ROT_DIM // 4
    inv = ROPE_BASE ** (-jnp.arange(n_freq, dtype=F32) / n_freq)
    row = jnp.repeat(jnp.arange(rows, dtype=F32), GRID_W)
    col = jnp.tile(jnp.arange(GRID_W, dtype=F32), rows)
    ang = jnp.concatenate([row[:, None] * inv, col[:, None] * inv], axis=-1)
    cos, sin = jnp.cos(ang), jnp.sin(ang)
    return jnp.concatenate([cos] * 4, axis=-1), jnp.concatenate([-sin, -sin, sin, sin], axis=-1)


def _diff_lambda_init(layer):
    return 0.8 - 0.6 * math.exp(-0.3 * layer)


def kernel(x, c, ctx, c_ctx, ada_w, ada_b, post_ln_g, post_ln_b, lb_table, ev_w_in, ev_lam, ev_subln_g, ev_gk_w2,
           ev_gk_b, ev_gla_norm_g, ev_w_out, ev_ffn_w1, ev_ffn_w2, od_w_in, od_q_norm_g, od_kv_norm_g, od_w_uq,
           od_w_ukv, od_hg_norm_g, od_w_out, od_router, od_exp_w1, od_exp_w2):
    b, t, d = x.shape
    tc = ctx.shape[1]
    rope_c, rope_s = _rope_tables(t // GRID_W)
    lb_soft = jax.nn.softmax(lb_table.astype(F32), axis=0)
    lower_bounds = jnp.cumsum(lb_soft, axis=0) - lb_soft[0]

    n_cond = ((b + 1 + 7) // 8) * 8
    cond = jnp.zeros((n_cond, d), F32).at[:b].set(c).at[b].set(c_ctx)
    mods = _ada(cond, ada_w, ada_b).reshape(DEPTH, n_cond, 6, d)

    ctx_flat = None
    for layer in range(DEPTH):
        last = layer == DEPTH - 1
        j = layer // 2
        m_l = [mods[layer, :b, i][:, None, :] for i in range(6)]
        m_c = [jnp.broadcast_to(mods[layer, b, i][None, None, :], (b, 1, d)) for i in range(6)]
        m_c1 = [m[:1] for m in m_c]
        even = layer % 2 == 0
        w_in = _even_w_in(ev_w_in[j]) if even else _odd_w_in(od_w_in[j])
        p_l = _proj(x, 1.0 + m_l[1], m_l[0], w_in)
        p_c = _proj(ctx, 1.0 + m_c[1], m_c[0], w_in)
        need_ctx = not last
        if even:
            lam_init = _diff_lambda_init(layer)
            lv = ev_lam[j].astype(F32)
            lam = jnp.exp(jnp.sum(lv[0] * lv[1])) - jnp.exp(jnp.sum(lv[2] * lv[3])) + lam_init
            oa_c, oa_l = _diff_attention(p_l, p_c, rope_c, rope_s, lam, ev_subln_g[j], lam_init, need_ctx)
            wgk, bgk = _gla_gate_weights(ev_gk_w2[j], ev_gk_b[j])
            ob_c, ob_l = _gla_scan(p_c, p_l, wgk, bgk, ev_gla_norm_g[j], need_ctx)
            w_out = ev_w_out[j].astype(BF16)
        else:
            wq, wkv = _mla_weights(od_w_uq[j], od_w_ukv[j])
            oa_l = _mla_attention(p_l, p_c, rope_c, rope_s, od_q_norm_g[j], od_kv_norm_g[j], wq, wkv)
            oa_c = _mla_attention_ctx(p_c, od_q_norm_g[j], od_kv_norm_g[j], wq, wkv) if need_ctx else None
            ob_c, ob_l = _hgrn_scan(p_c, p_l, lower_bounds[layer], od_hg_norm_g[j], need_ctx)
            w_out = od_w_out[j].astype(BF16)
        g0, b0 = post_ln_g[layer, 0], post_ln_b[layer, 0]
        g1, b1 = post_ln_g[layer, 1], post_ln_b[layer, 1]
        x = _outproj_ln(oa_l, ob_l, w_out, x, m_l[2], g0, b0)
        if even:
            w1, w2 = ev_ffn_w1[j].astype(BF16), ev_ffn_w2[j].astype(BF16)
            x = _ffn_ln(x, 1.0 + m_l[4], m_l[3], w1, w2, m_l[5], g1, b1)
        else:
            w1, w2 = od_exp_w1[j].astype(BF16), od_exp_w2[j].astype(BF16)
            x = _moe_ln(x, 1.0 + m_l[4], m_l[3], od_router[j], w1, w2, m_l[5], g1, b1)
        if need_ctx:
            ctx = _outproj_ln(oa_c, ob_c, w_out, ctx, m_c[2], g0, b0)
            ctx_flat = ctx.reshape(1, b * tc, d)
            if even:
                ctx_flat = _ffn_ln(ctx_flat, 1.0 + m_c1[4], m_c1[3], w1, w2, m_c1[5], g1, b1)
            else:
                ctx_flat = _moe_ln(ctx_flat, 1.0 + m_c1[4], m_c1[3], od_router[j], w1, w2, m_c1[5], g1, b1)
            ctx = ctx_flat.reshape(b, tc, d)
    return x
```

```python
import functools
import math

import jax
import jax.numpy as jnp
import numpy as np
from jax import lax
from jax.experimental import pallas as pl
from jax.experimental.pallas import tpu as pltpu

F32 = jnp.float32
BF16 = jnp.bfloat16

DEPTH = 2
GRID_W = 64
ROT_DIM = 64
ROPE_BASE = 10000.0
DA_HEADS = 4
DA_DH = ROT_DIM
DA_DV = 2 * DA_DH
GLA_HEADS = 4
GLA_DK = 64
GLA_DV = 128
GLA_LR = 16
GLA_NORMALIZER = 16.0
MLA_HEADS = 4
MLA_Q_RANK = 256
MLA_KV_RANK = 128
MLA_NOPE = 128
MLA_ROPE = ROT_DIM
MLA_DV = 128
MLA_SCALE = (MLA_NOPE + MLA_ROPE) ** -0.5
HG_HEADS = 4
HG_DK = 128
HG_DV = 128
D_FF = 3584
N_EXPERTS = 8
TOP_K = 2
LN_EPS = 1e-5
RMS_EPS = 1e-6
DN_ALPHA = (2 * DEPTH) ** 0.25

LANES = 128
VMEM_LIMIT = 56 * 1024 * 1024
SCAN_BLOCK = 256
SCAN_CHUNK = 32
MOE_TM = 1024
FFN_TF = 512

EV_QA, EV_KA, EV_VA, EV_QB, EV_KB, EV_VB, EV_GB, EV_LR = 0, 4, 8, 12, 14, 16, 20, 24
EV_NPAD = 25 * LANES
OD_CQ, OD_CKV, OD_KR, OD_HQ, OD_FF, OD_FB, OD_HI, OD_HG = 0, 2, 3, 4, 8, 12, 16, 20
OD_NPAD = 24 * LANES


def _cparams(sem):
    return pltpu.CompilerParams(dimension_semantics=sem, vmem_limit_bytes=VMEM_LIMIT)


def _silu(v):
    return v * jax.nn.sigmoid(v)


def _layernorm_rows(z, g, b):
    mu = jnp.mean(z, axis=-1, keepdims=True)
    zc = z - mu
    var = jnp.mean(zc * zc, axis=-1, keepdims=True)
    return zc * lax.rsqrt(var + LN_EPS) * g + b


def _rmsnorm_rows(v, g):
    return v * lax.rsqrt(jnp.mean(v * v, axis=-1, keepdims=True) + RMS_EPS) * g


def _dot_nt(a, b):
    return lax.dot_general(a, b, (((1,), (1,)), ((), ())), preferred_element_type=F32)


def _dot_tn(a, b):
    return lax.dot_general(a, b, (((0,), (0,)), ((), ())), preferred_element_type=F32)


def _dot(a, b):
    return jnp.dot(a, b, preferred_element_type=F32)


def _ada_kernel(c_ref, w_ref, b_ref, o_ref):
    s = _silu(c_ref[...]).astype(BF16)
    o_ref[0] = _dot(s, w_ref[0].astype(BF16)) + b_ref[0]


def _ada(cond, ada_w, ada_b):
    depth, d, n = ada_w.shape
    r = cond.shape[0]
    tn = n // 4
    return pl.pallas_call(
        _ada_kernel,
        grid=(depth, n // tn),
        in_specs=[
            pl.BlockSpec((r, d), lambda l, j: (0, 0)),
            pl.BlockSpec((1, d, tn), lambda l, j: (l, 0, j)),
            pl.BlockSpec((1, 1, tn), lambda l, j: (l, 0, j)),
        ],
        out_specs=pl.BlockSpec((1, r, tn), lambda l, j: (l, 0, j)),
        out_shape=jax.ShapeDtypeStruct((depth, r, n), F32),
        compiler_params=_cparams(("parallel", "parallel")),
        name="ada_modulation",
    )(cond, ada_w, ada_b.reshape(depth, 1, n))


def _proj_kernel(x_ref, sc_ref, sh_ref, w_ref, o_ref):
    h = (x_ref[0] * sc_ref[0] + sh_ref[0]).astype(BF16)
    o_ref[0] = _dot(h, w_ref[...]).astype(o_ref.dtype)


def _proj(x, scale, shift, w):
    b, t, d = x.shape
    n = w.shape[1]
    tm = min(t, 512)
    return pl.pallas_call(
        _proj_kernel,
        grid=(b, t // tm),
        in_specs=[
            pl.BlockSpec((1, tm, d), lambda bi, i: (bi, i, 0)),
            pl.BlockSpec((1, 1, d), lambda bi, i: (bi, 0, 0)),
            pl.BlockSpec((1, 1, d), lambda bi, i: (bi, 0, 0)),
            pl.BlockSpec((d, n), lambda bi, i: (0, 0)),
        ],
        out_specs=pl.BlockSpec((1, tm, n), lambda bi, i: (bi, i, 0)),
        out_shape=jax.ShapeDtypeStruct((b, t, n), BF16),
        compiler_params=_cparams(("parallel", "parallel")),
        name="mod_proj",
    )(x, scale, shift, w)


def _rope128(t, cs, sn):
    t = t.astype(F32)
    return t * cs + pltpu.roll(t, LANES // 2, axis=1) * sn


LOG2E = math.log2(math.e)


def _softmax_pv(s2, v_bf):
    m = jnp.max(s2, axis=-1, keepdims=True)
    e = jnp.exp2(s2 - m)
    return _dot(e.astype(BF16), v_bf) / jnp.sum(e, axis=-1, keepdims=True)


def _q1_lane_mask(shape):
    lane = lax.broadcasted_iota(jnp.int32, shape, 1)
    return (lane // 32) % 2 == 0


def _diff_scores_out(q, k_bf, v_bf, lam):
    m1 = _q1_lane_mask(q.shape)
    q1 = jnp.where(m1, q, 0.0).astype(BF16)
    q2 = jnp.where(m1, 0.0, q).astype(BF16)
    return _softmax_pv(_dot_nt(q1, k_bf), v_bf) - lam * _softmax_pv(_dot_nt(q2, k_bf), v_bf)


def _diffattn_lat_kernel(lam_init, tc, q_ref, kl_ref, vl_ref, kc_ref, vc_ref, cq_ref, sq_ref, ck_ref, sk_ref,
                         lam_ref, g_ref, o_ref, k_s, v_s):
    @pl.when(pl.program_id(2) == 0)
    def _():
        k_s[0:tc, :] = kc_ref[0].astype(BF16)
        v_s[0:tc, :] = vc_ref[0].astype(BF16)
        k_s[tc:, :] = _rope128(kl_ref[0], ck_ref[...], sk_ref[...]).astype(BF16)
        v_s[tc:, :] = vl_ref[0].astype(BF16)

    q = _rope128(q_ref[0], cq_ref[...], sq_ref[...]) * (DA_DH ** -0.5 * LOG2E)
    o = _diff_scores_out(q, k_s[...], v_s[...], lam_ref[0, 0])
    o_ref[0] = (_rmsnorm_rows(o, g_ref[...]) * (1.0 - lam_init)).astype(o_ref.dtype)


def _diffattn_ctx_kernel(lam_init, q_ref, k_ref, v_ref, lam_ref, g_ref, o_ref):
    q = q_ref[0].astype(F32) * (DA_DH ** -0.5 * LOG2E)
    o = _diff_scores_out(q, k_ref[0].astype(BF16), v_ref[0].astype(BF16), lam_ref[0, 0])
    o_ref[0] = (_rmsnorm_rows(o, g_ref[...]) * (1.0 - lam_init)).astype(o_ref.dtype)


def _diff_attention(p_l, p_c, rope_c, rope_s, lam, subln_g, lam_init, need_ctx):
    b, tl, _ = p_l.shape
    tc = p_c.shape[1]
    h = DA_HEADS
    tq = 256
    lam2 = lam.reshape(1, 1).astype(F32)
    g2 = subln_g.reshape(1, DA_DV).astype(F32)
    smem = pl.BlockSpec(memory_space=pltpu.SMEM)
    o_l = pl.pallas_call(
        functools.partial(_diffattn_lat_kernel, lam_init, tc),
        grid=(b, h, tl // tq),
        in_specs=[
            pl.BlockSpec((1, tq, LANES), lambda bi, hi, i: (bi, i, EV_QA + hi)),
            pl.BlockSpec((1, tl, LANES), lambda bi, hi, i: (bi, 0, EV_KA + hi)),
            pl.BlockSpec((1, tl, LANES), lambda bi, hi, i: (bi, 0, EV_VA + hi)),
            pl.BlockSpec((1, tc, LANES), lambda bi, hi, i: (bi, 0, EV_KA + hi)),
            pl.BlockSpec((1, tc, LANES), lambda bi, hi, i: (bi, 0, EV_VA + hi)),
            pl.BlockSpec((tq, LANES), lambda bi, hi, i: (i, 0)),
            pl.BlockSpec((tq, LANES), lambda bi, hi, i: (i, 0)),
            pl.BlockSpec((tl, LANES), lambda bi, hi, i: (0, 0)),
            pl.BlockSpec((tl, LANES), lambda bi, hi, i: (0, 0)),
            smem,
            pl.BlockSpec((1, LANES), lambda bi, hi, i: (0, 0)),
        ],
        out_specs=pl.BlockSpec((1, tq, LANES), lambda bi, hi, i: (bi, i, hi)),
        out_shape=jax.ShapeDtypeStruct((b, tl, h * DA_DV), BF16),
        scratch_shapes=[pltpu.VMEM((tc + tl, LANES), BF16), pltpu.VMEM((tc + tl, LANES), BF16)],
        compiler_params=_cparams(("parallel", "parallel", "arbitrary")),
        name="diff_attention_latent",
    )(p_l, p_l, p_l, p_c, p_c, rope_c, rope_s, rope_c, rope_s, lam2, g2)
    if not need_ctx:
        return None, o_l
    o_c = pl.pallas_call(
        functools.partial(_diffattn_ctx_kernel, lam_init),
        grid=(b, h),
        in_specs=[
            pl.BlockSpec((1, tc, LANES), lambda bi, hi: (bi, 0, EV_QA + hi)),
            pl.BlockSpec((1, tc, LANES), lambda bi, hi: (bi, 0, EV_KA + hi)),
            pl.BlockSpec((1, tc, LANES), lambda bi, hi: (bi, 0, EV_VA + hi)),
            smem,
            pl.BlockSpec((1, LANES), lambda bi, hi: (0, 0)),
        ],
        out_specs=pl.BlockSpec((1, tc, LANES), lambda bi, hi: (bi, 0, hi)),
        out_shape=jax.ShapeDtypeStruct((b, tc, h * DA_DV), BF16),
        compiler_params=_cparams(("parallel", "parallel")),
        name="diff_attention_ctx",
    )(p_c, p_c, p_c, lam2, g2)
    return o_c, o_l


def _mla_q(cq, qg, wq, cs, sn):
    q = _dot(_rmsnorm_rows(cq.astype(F32), qg).astype(BF16), wq)
    if cs is not None:
        q = jnp.concatenate([q[:, :LANES], _rope128(q[:, LANES:], cs, sn)], axis=1)
    return (q * (MLA_SCALE * LOG2E)).astype(BF16)


def _mla_kv(ckv, kr, kvg, wkv, cs, sn):
    kv = _dot(_rmsnorm_rows(ckv.astype(F32), kvg).astype(BF16), wkv)
    if cs is not None:
        kr = _rope128(kr, cs, sn)
    k = jnp.concatenate([kv[:, :LANES].astype(BF16), kr.astype(BF16)], axis=1)
    return k, kv[:, LANES:].astype(BF16)


def _mla_lat_kernel(tc, cq_ref, ckvl_ref, krl_ref, ckvc_ref, krc_ref, cq_c_ref, cq_s_ref, ck_ref, sk_ref,
                    qg_ref, kvg_ref, wq_ref, wkv_ref, o_ref, k_s, v_s):
    @pl.when(pl.program_id(2) == 0)
    def _():
        kc, vc = _mla_kv(ckvc_ref[0], krc_ref[0], kvg_ref[...], wkv_ref[0], None, None)
        k_s[0:tc, :] = kc
        v_s[0:tc, :] = vc
        kl, vl = _mla_kv(ckvl_ref[0], krl_ref[0], kvg_ref[...], wkv_ref[0], ck_ref[...], sk_ref[...])
        k_s[tc:, :] = kl
        v_s[tc:, :] = vl

    q = _mla_q(cq_ref[0], qg_ref[...], wq_ref[0], cq_c_ref[...], cq_s_ref[...])
    o_ref[0] = _softmax_pv(_dot_nt(q, k_s[...]), v_s[...]).astype(o_ref.dtype)


def _mla_attention(p_l, p_c, rope_c, rope_s, q_norm_g, kv_norm_g, wq, wkv):
    b, tl, _ = p_l.shape
    tc = p_c.shape[1]
    h = MLA_HEADS
    tq = 256
    return pl.pallas_call(
        functools.partial(_mla_lat_kernel, tc),
        grid=(b, h, tl // tq),
        in_specs=[
            pl.BlockSpec((1, tq, MLA_Q_RANK), lambda bi, hi, i: (bi, i, OD_CQ)),
            pl.BlockSpec((1, tl, LANES), lambda bi, hi, i: (bi, 0, OD_CKV)),
            pl.BlockSpec((1, tl, LANES), lambda bi, hi, i: (bi, 0, OD_KR)),
            pl.BlockSpec((1, tc, LANES), lambda bi, hi, i: (bi, 0, OD_CKV)),
            pl.BlockSpec((1, tc, LANES), lambda bi, hi, i: (bi, 0, OD_KR)),
            pl.BlockSpec((tq, LANES), lambda bi, hi, i: (i, 0)),
            pl.BlockSpec((tq, LANES), lambda bi, hi, i: (i, 0)),
            pl.BlockSpec((tl, LANES), lambda bi, hi, i: (0, 0)),
            pl.BlockSpec((tl, LANES), lambda bi, hi, i: (0, 0)),
            pl.BlockSpec((1, MLA_Q_RANK), lambda bi, hi, i: (0, 0)),
            pl.BlockSpec((1, MLA_KV_RANK), lambda bi, hi, i: (0, 0)),
            pl.BlockSpec((1, MLA_Q_RANK, 2 * LANES), lambda bi, hi, i: (hi, 0, 0)),
            pl.BlockSpec((1, MLA_KV_RANK, 2 * LANES), lambda bi, hi, i: (hi, 0, 0)),
        ],
        out_specs=pl.BlockSpec((1, tq, LANES), lambda bi, hi, i: (bi, i, hi)),
        out_shape=jax.ShapeDtypeStruct((b, tl, h * MLA_DV), BF16),
        scratch_shapes=[pltpu.VMEM((tc + tl, 2 * LANES), BF16), pltpu.VMEM((tc + tl, LANES), BF16)],
        compiler_params=_cparams(("parallel", "parallel", "arbitrary")),
        name="mla_attention_latent",
    )(p_l, p_l, p_l, p_c, p_c, rope_c, rope_s, rope_c, rope_s,
      q_norm_g.reshape(1, -1), kv_norm_g.reshape(1, -1), wq, wkv)


def _mla_ctx_kernel(cq_ref, ckv_ref, kr_ref, qg_ref, kvg_ref, wq_ref, wkv_ref, o_ref):
    k, v = _mla_kv(ckv_ref[0], kr_ref[0], kvg_ref[...], wkv_ref[0], None, None)
    q = _mla_q(cq_ref[0], qg_ref[...], wq_ref[0], None, None)
    o_ref[0] = _softmax_pv(_dot_nt(q, k), v).astype(o_ref.dtype)


def _mla_attention_ctx(p_c, q_norm_g, kv_norm_g, wq, wkv):
    b, tc, _ = p_c.shape
    h = MLA_HEADS
    return pl.pallas_call(
        _mla_ctx_kernel,
        grid=(b, h),
        in_specs=[
            pl.BlockSpec((1, tc, MLA_Q_RANK), lambda bi, hi: (bi, 0, OD_CQ)),
            pl.BlockSpec((1, tc, LANES), lambda bi, hi: (bi, 0, OD_CKV)),
            pl.BlockSpec((1, tc, LANES), lambda bi, hi: (bi, 0, OD_KR)),
            pl.BlockSpec((1, MLA_Q_RANK), lambda bi, hi: (0, 0)),
            pl.BlockSpec((1, MLA_KV_RANK), lambda bi, hi: (0, 0)),
            pl.BlockSpec((1, MLA_Q_RANK, 2 * LANES), lambda bi, hi: (hi, 0, 0)),
            pl.BlockSpec((1, MLA_KV_RANK, 2 * LANES), lambda bi, hi: (hi, 0, 0)),
        ],
        out_specs=pl.BlockSpec((1, tc, LANES), lambda bi, hi: (bi, 0, hi)),
        out_shape=jax.ShapeDtypeStruct((b, tc, h * MLA_DV), BF16),
        compiler_params=_cparams(("parallel", "parallel")),
        name="mla_attention_ctx",
    )(p_c, p_c, p_c, q_norm_g.reshape(1, -1), kv_norm_g.reshape(1, -1), wq, wkv)


def _scan_levels(bt):
    levels = []
    c = SCAN_CHUNK
    while c <= bt:
        levels.append(c)
        c *= 2
    return levels


def _level_table(bt, reverse):
    i = np.arange(bt)[:, None]
    j = np.arange(bt)[None, :]
    if reverse:
        i, j = j, i
    tab = np.zeros((bt, bt), np.int32)
    for lvl, c in enumerate(_scan_levels(bt), start=1):
        same = (i // c) == (j // c)
        if lvl == 1:
            m = same & (j <= i)
        else:
            m = same & ((i % c) >= c // 2) & ((j % c) < c // 2)
        tab[m] = lvl
    return tab


def _chunk_row(a, c, r):
    bt, n = a.shape
    a3 = a.reshape(bt // c, c, n)
    return jnp.broadcast_to(a3[:, r:r + 1, :], (bt // c, c, n)).reshape(bt, n)


def _scan_block(q_s, k_s, g_s, v_s, o_s, tri_ref, lvl_ref, start, st, reverse, compute_out):
    bt = SCAN_BLOCK
    rows = pl.ds(start, bt)
    g = g_s[rows, :]
    k = k_s[rows, :]
    v = v_s[rows, :]
    g_hi = g.astype(BF16)
    g_lo = (g - g_hi.astype(F32)).astype(BF16)
    tri = tri_ref[...]
    gc = _dot(tri, g_hi) + _dot(tri, g_lo)
    g_tot = gc[0:1, :] if reverse else gc[bt - 1:bt, :]
    kd = (k * jnp.exp(g_tot - gc)).astype(BF16)
    st_new = st * jnp.exp(g_tot) + _dot_tn(v, kd)
    if compute_out:
        q = q_s[rows, :]
        o = _dot_nt((q * jnp.exp(gc)).astype(BF16), st.astype(BF16))
        lvl = lvl_ref[...]
        att = jnp.zeros((bt, bt), F32)
        for li, c in enumerate(_scan_levels(bt), start=1):
            r = _chunk_row(gc, c, c // 2 if reverse else c // 2 - 1)
            a = _dot_nt((q * jnp.exp(gc - r)).astype(BF16), (k * jnp.exp(r - gc)).astype(BF16))
            att = jnp.where(lvl == li, a, att)
        o_s[rows, :] = o + _dot(att.astype(BF16), v)
    return st_new


def _scan_both(q_s, v_s, k_refs, g_refs, o_refs, tri_refs, lvl_refs, tc, tl, need_ctx):
    bt = SCAN_BLOCK
    nc, nl = tc // bt, tl // bt

    def run(first, n, carry, compute_out):
        def body(i, carry):
            new = []
            for d in range(2):
                blk = first + (n - 1 - i if d else i)
                start = pl.multiple_of(blk * bt, bt)
                new.append(_scan_block(q_s, k_refs[d], g_refs[d], v_s, o_refs[d], tri_refs[d], lvl_refs[d], start,
                                       carry[d], d == 1, compute_out))
            return tuple(new)
        return lax.fori_loop(0, n, body, carry)

    zero = jnp.zeros((LANES, LANES), F32)
    carry = run(0, nc, (zero, zero), need_ctx)
    run(nc, nl, carry, True)


def _scan_finish(o_refs, gate_c_ref, gate_l_ref, ng_ref, oc_ref, ol_ref, tc, need_ctx):
    ng = ng_ref[...]
    of_s, ob_s = o_refs
    if need_ctx:
        o = of_s[0:tc, :] + ob_s[0:tc, :]
        oc_ref[0] = (_rmsnorm_rows(o, ng) * _silu(gate_c_ref[0].astype(F32))).astype(oc_ref.dtype)
    o = of_s[tc:, :] + ob_s[tc:, :]
    ol_ref[0] = (_rmsnorm_rows(o, ng) * _silu(gate_l_ref[0].astype(F32))).astype(ol_ref.dtype)


def _gla_scan_kernel(tc, tl, need_ctx, *refs):
    (qc_ref, kc_ref, vc_ref, gbc_ref, lrc_ref, ql_ref, kl_ref, vl_ref, gbl_ref, lrl_ref,
     wgk_ref, bgk_ref, ng_ref, trif_ref, trib_ref, lvlf_ref, lvlb_ref) = refs[:17]
    if need_ctx:
        oc_ref, ol_ref = refs[17:19]
        scratch = refs[19:]
    else:
        oc_ref, ol_ref = None, refs[17]
        scratch = refs[18:]
    q_s, v_s, k_s, _, gf_s, gb_s, of_s, ob_s = scratch
    lane = lax.broadcasted_iota(jnp.int32, (1, LANES), 1)
    mine = (lane // GLA_DK) == (pl.program_id(1) % 2)
    q_s[0:tc, :] = jnp.where(mine, qc_ref[0].astype(F32), 0.0) * (GLA_DK ** -0.5)
    q_s[tc:, :] = jnp.where(mine, ql_ref[0].astype(F32), 0.0) * (GLA_DK ** -0.5)
    k_s[0:tc, :] = jnp.where(mine, kc_ref[0].astype(F32), 0.0)
    k_s[tc:, :] = jnp.where(mine, kl_ref[0].astype(F32), 0.0)
    v_s[0:tc, :] = vc_ref[0].astype(BF16)
    v_s[tc:, :] = vl_ref[0].astype(BF16)
    for d, g_s in enumerate((gf_s, gb_s)):
        w = wgk_ref[0, d]
        bias = bgk_ref[0, d]
        for lr_ref, lo, hi in ((lrc_ref, 0, tc), (lrl_ref, tc, tc + tl)):
            z = _dot(lr_ref[0], w) + bias
            g_s[lo:hi, :] = jnp.where(mine, jax.nn.log_sigmoid(z) / GLA_NORMALIZER, 0.0)
    _scan_both(q_s, v_s, (k_s, k_s), (gf_s, gb_s), (of_s, ob_s), (trif_ref, trib_ref), (lvlf_ref, lvlb_ref),
               tc, tl, need_ctx)
    _scan_finish((of_s, ob_s), gbc_ref, gbl_ref, ng_ref, oc_ref, ol_ref, tc, need_ctx)


def _hgrn_scan_kernel(tc, tl, need_ctx, *refs):
    (qc_ref, ffc_ref, fbc_ref, vc_ref, gtc_ref, ql_ref, ffl_ref, fbl_ref, vl_ref, gtl_ref,
     lb_ref, ng_ref, trif_ref, trib_ref, lvlf_ref, lvlb_ref) = refs[:16]
    if need_ctx:
        oc_ref, ol_ref = refs[16:18]
        scratch = refs[18:]
    else:
        oc_ref, ol_ref = None, refs[16]
        scratch = refs[17:]
    q_s, v_s, kf_s, kb_s, gf_s, gb_s, of_s, ob_s = scratch
    lb = lb_ref[...]
    q_s[0:tc, :] = qc_ref[0].astype(F32)
    q_s[tc:, :] = ql_ref[0].astype(F32)
    v_s[0:tc, :] = vc_ref[0].astype(BF16)
    v_s[tc:, :] = vl_ref[0].astype(BF16)
    for (fc_ref, fl_ref), k_s, g_s in (((ffc_ref, ffl_ref), kf_s, gf_s), ((fbc_ref, fbl_ref), kb_s, gb_s)):
        for f_ref, lo, hi in ((fc_ref, 0, tc), (fl_ref, tc, tc + tl)):
            f = lb + (1.0 - lb) * jax.nn.sigmoid(f_ref[0].astype(F32))
            k_s[lo:hi, :] = 1.0 - f
            g_s[lo:hi, :] = jnp.log(f)
    _scan_both(q_s, v_s, (kf_s, kb_s), (gf_s, gb_s), (of_s, ob_s), (trif_ref, trib_ref), (lvlf_ref, lvlb_ref),
               tc, tl, need_ctx)
    _scan_finish((of_s, ob_s), gtc_ref, gtl_ref, ng_ref, oc_ref, ol_ref, tc, need_ctx)


def _scan_consts():
    bt = SCAN_BLOCK
    lower = np.tril(np.ones((bt, bt), np.float32))
    return (jnp.asarray(lower, BF16), jnp.asarray(lower.T, BF16),
            jnp.asarray(_level_table(bt, False)), jnp.asarray(_level_table(bt, True)))


def _scan_call(kernel_fn, name, p_c, p_l, col_specs, extra, extra_specs, need_ctx, heads):
    b, tl, _ = p_l.shape
    tc = p_c.shape[1]
    bt = SCAN_BLOCK
    t = tc + tl
    consts = _scan_consts()
    const_specs = [pl.BlockSpec((bt, bt), lambda bi, hi: (0, 0)) for _ in consts]
    in_specs = ([pl.BlockSpec((1, tc, LANES), f) for f in col_specs]
                + [pl.BlockSpec((1, tl, LANES), f) for f in col_specs] + extra_specs + const_specs)
    args = [p_c] * len(col_specs) + [p_l] * len(col_specs) + list(extra) + list(consts)
    out_l = jax.ShapeDtypeStruct((b, tl, heads * LANES), BF16)
    spec_l = pl.BlockSpec((1, tl, LANES), lambda bi, hi: (bi, 0, hi))
    if need_ctx:
        out_shape = (jax.ShapeDtypeStruct((b, tc, heads * LANES), BF16), out_l)
        out_specs = (pl.BlockSpec((1, tc, LANES), lambda bi, hi: (bi, 0, hi)), spec_l)
    else:
        out_shape, out_specs = out_l, spec_l
    res = pl.pallas_call(
        functools.partial(kernel_fn, tc, tl, need_ctx),
        grid=(b, heads),
        in_specs=in_specs,
        out_specs=out_specs,
        out_shape=out_shape,
        scratch_shapes=[pltpu.VMEM((t, LANES), F32), pltpu.VMEM((t, LANES), BF16)]
        + [pltpu.VMEM((t, LANES), F32) for _ in range(6)],
        compiler_params=_cparams(("parallel", "parallel")),
        name=name,
    )(*args)
    return res if need_ctx else (None, res)


def _gla_scan(p_c, p_l, wgk, bgk, norm_g, need_ctx):
    col_specs = [
        lambda bi, hi: (bi, 0, EV_QB + hi // 2),
        lambda bi, hi: (bi, 0, EV_KB + hi // 2),
        lambda bi, hi: (bi, 0, EV_VB + hi),
        lambda bi, hi: (bi, 0, EV_GB + hi),
        lambda bi, hi: (bi, 0, EV_LR),
    ]
    extra_specs = [
        pl.BlockSpec((1, 2, LANES, LANES), lambda bi, hi: (hi // 2, 0, 0, 0)),
        pl.BlockSpec((1, 2, 1, LANES), lambda bi, hi: (hi // 2, 0, 0, 0)),
        pl.BlockSpec((1, LANES), lambda bi, hi: (0, 0)),
    ]
    return _scan_call(_gla_scan_kernel, "gla_scan", p_c, p_l, col_specs,
                      (wgk, bgk, norm_g.reshape(1, LANES)), extra_specs, need_ctx, GLA_HEADS)


def _hgrn_scan(p_c, p_l, lb, norm_g, need_ctx):
    col_specs = [
        lambda bi, hi: (bi, 0, OD_HQ + hi),
        lambda bi, hi: (bi, 0, OD_FF + hi),
        lambda bi, hi: (bi, 0, OD_FB + hi),
        lambda bi, hi: (bi, 0, OD_HI + hi),
        lambda bi, hi: (bi, 0, OD_HG + hi),
    ]
    extra_specs = [
        pl.BlockSpec((1, LANES), lambda bi, hi: (0, hi)),
        pl.BlockSpec((1, LANES), lambda bi, hi: (0, 0)),
    ]
    return _scan_call(_hgrn_scan_kernel, "hgrn2_scan", p_c, p_l, col_specs,
                      (lb.reshape(1, -1), norm_g.reshape(1, LANES)), extra_specs, need_ctx, HG_HEADS)


def _outproj_kernel(oa_ref, ob_ref, w_ref, x_ref, gate_ref, g_ref, b_ref, o_ref):
    ka = oa_ref.shape[2]
    y = _dot(oa_ref[0], w_ref[0:ka, :]) + _dot(ob_ref[0], w_ref[ka:, :])
    z = DN_ALPHA * x_ref[0] + gate_ref[0] * y
    o_ref[0] = _layernorm_rows(z, g_ref[...], b_ref[...])


def _outproj_ln(oa, ob, w, x, gate, ln_g, ln_b):
    b, t, d = x.shape
    ka, kb = oa.shape[2], ob.shape[2]
    tm = min(t, 512)
    return pl.pallas_call(
        _outproj_kernel,
        grid=(b, t // tm),
        in_specs=[
            pl.BlockSpec((1, tm, ka), lambda bi, i: (bi, i, 0)),
            pl.BlockSpec((1, tm, kb), lambda bi, i: (bi, i, 0)),
            pl.BlockSpec((ka + kb, d), lambda bi, i: (0, 0)),
            pl.BlockSpec((1, tm, d), lambda bi, i: (bi, i, 0)),
            pl.BlockSpec((1, 1, d), lambda bi, i: (bi, 0, 0)),
            pl.BlockSpec((1, d), lambda bi, i: (0, 0)),
            pl.BlockSpec((1, d), lambda bi, i: (0, 0)),
        ],
        out_specs=pl.BlockSpec((1, tm, d), lambda bi, i: (bi, i, 0)),
        out_shape=jax.ShapeDtypeStruct((b, t, d), F32),
        compiler_params=_cparams(("parallel", "parallel")),
        name="outproj_residual_ln",
    )(oa, ob, w, x, gate, ln_g.reshape(1, d), ln_b.reshape(1, d))


def _ffn_kernel(x_ref, sc_ref, sh_ref, w1g_ref, w1u_ref, w2_ref, gate_ref, g_ref, b_ref, o_ref, h_s, acc_s):
    f = pl.program_id(2)

    @pl.when(f == 0)
    def _():
        h_s[...] = (x_ref[0] * sc_ref[0] + sh_ref[0]).astype(BF16)
        acc_s[...] = jnp.zeros_like(acc_s)

    h = h_s[...]
    a = _silu(_dot(h, w1g_ref[...])) * _dot(h, w1u_ref[...])
    acc_s[...] += _dot(a.astype(BF16), w2_ref[...])

    @pl.when(f == pl.num_programs(2) - 1)
    def _():
        z = DN_ALPHA * x_ref[0] + gate_ref[0] * acc_s[...]
        o_ref[0] = _layernorm_rows(z, g_ref[...], b_ref[...])


def _ffn_ln(x, scale, shift, w1, w2, gate, ln_g, ln_b):
    b, t, d = x.shape
    ff = w2.shape[0]
    tf = FFN_TF
    nf = ff // tf
    tm = min(t, 1024)
    return pl.pallas_call(
        _ffn_kernel,
        grid=(b, t // tm, nf),
        in_specs=[
            pl.BlockSpec((1, tm, d), lambda bi, i, f: (bi, i, 0)),
            pl.BlockSpec((1, 1, d), lambda bi, i, f: (bi, 0, 0)),
            pl.BlockSpec((1, 1, d), lambda bi, i, f: (bi, 0, 0)),
            pl.BlockSpec((d, tf), lambda bi, i, f: (0, f)),
            pl.BlockSpec((d, tf), lambda bi, i, f: (0, nf + f)),
            pl.BlockSpec((tf, d), lambda bi, i, f: (f, 0)),
            pl.BlockSpec((1, 1, d), lambda bi, i, f: (bi, 0, 0)),
            pl.BlockSpec((1, d), lambda bi, i, f: (0, 0)),
            pl.BlockSpec((1, d), lambda bi, i, f: (0, 0)),
        ],
        out_specs=pl.BlockSpec((1, tm, d), lambda bi, i, f: (bi, i, 0)),
        out_shape=jax.ShapeDtypeStruct((b, t, d), F32),
        scratch_shapes=[pltpu.VMEM((tm, d), BF16), pltpu.VMEM((tm, d), F32)],
        compiler_params=_cparams(("parallel", "parallel", "arbitrary")),
        name="swiglu_residual_ln",
    )(x, scale, shift, w1, w1, w2, gate, ln_g.reshape(1, d), ln_b.reshape(1, d))


ROUTE_E0, ROUTE_E1, ROUTE_G0, ROUTE_G1, ROUTE_R0, ROUTE_R1 = range(6)


def _router_kernel(x_ref, sc_ref, sh_ref, wr_ref, h_ref, info_ref, cnt_ref, carry_s):
    first = (pl.program_id(0) == 0) & (pl.program_id(1) == 0)

    @pl.when(first)
    def _():
        carry_s[...] = jnp.zeros_like(carry_s)

    h = x_ref[0] * sc_ref[0] + sh_ref[0]
    h_ref[0] = h
    tm = h.shape[0]
    logits = jnp.dot(h, wr_ref[...], preferred_element_type=F32, precision=lax.Precision.HIGHEST)
    lane = lax.broadcasted_iota(jnp.int32, (tm, LANES), 1).astype(F32)
    neg = jnp.float32(-jnp.inf)
    logits = jnp.where(lane < N_EXPERTS, logits, neg)
    v0 = jnp.max(logits, axis=-1, keepdims=True)
    e0 = jnp.min(jnp.where(logits == v0, lane, float(LANES)), axis=-1, keepdims=True)
    rest = jnp.where(lane == e0, neg, logits)
    v1 = jnp.max(rest, axis=-1, keepdims=True)
    e1 = jnp.min(jnp.where(rest == v1, lane, float(LANES)), axis=-1, keepdims=True)
    d = jnp.exp(v1 - v0)
    g0 = 1.0 / (1.0 + d)
    g1 = d / (1.0 + d)
    oh0 = (lane == e0).astype(BF16)
    oh1 = (lane == e1).astype(BF16)
    ri = lax.broadcasted_iota(jnp.int32, (tm, tm), 0)
    ci = lax.broadcasted_iota(jnp.int32, (tm, tm), 1)
    before = (ci < ri).astype(BF16)
    c0 = _dot(before, oh0)
    c1 = _dot(before, oh1)
    tot0 = jnp.sum(oh0.astype(F32), axis=0, keepdims=True)
    tot1 = jnp.sum(oh1.astype(F32), axis=0, keepdims=True)
    carry = carry_s[...]
    r0 = jnp.sum(jnp.where(lane == e0, carry + c0, 0.0), axis=-1, keepdims=True)
    r1 = jnp.sum(jnp.where(lane == e1, carry + tot0 + c1, 0.0), axis=-1, keepdims=True)
    carry = carry + tot0 + tot1
    carry_s[...] = carry
    cnt_ref[...] = carry
    info = jnp.zeros((tm, LANES), F32)
    for col, val in ((ROUTE_E0, e0), (ROUTE_E1, e1), (ROUTE_G0, g0), (ROUTE_G1, g1), (ROUTE_R0, r0), (ROUTE_R1, r1)):
        info = jnp.where(lane == col, val, info)
    info_ref[0] = info


def _router(x, scale, shift, w_router):
    b, t, d = x.shape
    tm = min(t, 512)
    wr = jnp.zeros((d, LANES), F32).at[:, :N_EXPERTS].set(w_router.astype(F32))
    return pl.pallas_call(
        _router_kernel,
        grid=(b, t // tm),
        in_specs=[
            pl.BlockSpec((1, tm, d), lambda bi, i: (bi, i, 0)),
            pl.BlockSpec((1, 1, d), lambda bi, i: (bi, 0, 0)),
            pl.BlockSpec((1, 1, d), lambda bi, i: (bi, 0, 0)),
            pl.BlockSpec((d, LANES), lambda bi, i: (0, 0)),
        ],
        out_specs=(
            pl.BlockSpec((1, tm, d), lambda bi, i: (bi, i, 0)),
            pl.BlockSpec((1, tm, LANES), lambda bi, i: (bi, i, 0)),
            pl.BlockSpec((1, LANES), lambda bi, i: (0, 0)),
        ),
        out_shape=(
            jax.ShapeDtypeStruct((b, t, d), F32),
            jax.ShapeDtypeStruct((b, t, LANES), F32),
            jax.ShapeDtypeStruct((1, LANES), F32),
        ),
        scratch_shapes=[pltpu.VMEM((1, LANES), F32)],
        compiler_params=_cparams(("arbitrary", "arbitrary")),
        name="moe_router",
    )(x, scale, shift, wr)


DISPATCH_ROWS = 512


def _dispatch_kernel(pos0_ref, pos1_ref, h_ref, zero_ref, hs_ref, sem):
    del zero_ref
    base = pl.program_id(0) * DISPATCH_ROWS

    def row_copy(r, pos_ref):
        return pltpu.make_async_copy(h_ref.at[pl.ds(r, 1), :], hs_ref.at[pl.ds(pos_ref[base + r], 1), :], sem)

    def start(r, carry):
        row_copy(r, pos0_ref).start(priority=0)
        row_copy(r, pos1_ref).start(priority=1)
        return carry

    def wait(r, carry):
        row_copy(r, pos0_ref).wait()
        row_copy(r, pos1_ref).wait()
        return carry

    lax.fori_loop(0, DISPATCH_ROWS, start, 0, unroll=8)
    lax.fori_loop(0, DISPATCH_ROWS, wait, 0, unroll=8)


def _dispatch(h2, pos0, pos1, p_pad):
    n, d = h2.shape
    grid_spec = pltpu.PrefetchScalarGridSpec(
        num_scalar_prefetch=2,
        grid=(n // DISPATCH_ROWS,),
        in_specs=[pl.BlockSpec((DISPATCH_ROWS, d), lambda i, p0, p1: (i, 0)), pl.BlockSpec(memory_space=pl.ANY)],
        out_specs=pl.BlockSpec(memory_space=pl.ANY),
        scratch_shapes=[pltpu.SemaphoreType.DMA],
    )
    return pl.pallas_call(
        _dispatch_kernel,
        grid_spec=grid_spec,
        out_shape=jax.ShapeDtypeStruct((p_pad, d), F32),
        input_output_aliases={3: 0},
        compiler_params=_cparams(("arbitrary",)),
        name="moe_dispatch",
    )(pos0, pos1, h2, jnp.zeros((p_pad, d), F32))


def _expert_ffn_kernel(te_ref, nr_ref, nu_ref, hs_ref, w1g_ref, w1u_ref, w2_ref, ys_ref, h_s, acc_s):
    t = pl.program_id(0)
    f = pl.program_id(1)
    tm = hs_ref.shape[0]
    used = t < nu_ref[0]
    full = nr_ref[t] > tm // 2

    def run(rows):
        @pl.when(f == 0)
        def _():
            h_s[0:rows, :] = hs_ref[0:rows, :].astype(BF16)
            acc_s[0:rows, :] = jnp.zeros((rows, acc_s.shape[1]), F32)

        h = h_s[0:rows, :]
        a = _silu(_dot(h, w1g_ref[0])) * _dot(h, w1u_ref[0])
        acc_s[0:rows, :] += _dot(a.astype(BF16), w2_ref[0])

        @pl.when(f == pl.num_programs(1) - 1)
        def _():
            ys_ref[0:rows, :] = acc_s[0:rows, :]
            if rows < tm:
                ys_ref[rows:, :] = jnp.zeros((tm - rows, ys_ref.shape[1]), F32)

    pl.when(used & full)(lambda: run(tm))
    pl.when(used & jnp.logical_not(full))(lambda: run(tm // 2))

    @pl.when(jnp.logical_not(used) & (f == 0))
    def _():
        ys_ref[...] = jnp.zeros_like(ys_ref)


def _expert_ffn(hs, tile_e, tile_rows, n_used, w1, w2):
    p_pad, d = hs.shape
    ff = w2.shape[1]
    tm, tf = MOE_TM, FFN_TF
    nf = ff // tf
    nt = p_pad // tm

    def tile(t, nu):
        return jnp.maximum(jnp.minimum(t, nu[0] - 1), 0)

    def ftile(t, f, nu):
        return jnp.where(t < nu[0], f, nf - 1)

    grid_spec = pltpu.PrefetchScalarGridSpec(
        num_scalar_prefetch=3,
        grid=(nt, nf),
        in_specs=[
            pl.BlockSpec((tm, d), lambda t, f, te, nr, nu: (tile(t, nu), 0)),
            pl.BlockSpec((1, d, tf), lambda t, f, te, nr, nu: (te[tile(t, nu)], 0, ftile(t, f, nu))),
            pl.BlockSpec((1, d, tf), lambda t, f, te, nr, nu: (te[tile(t, nu)], 0, nf + ftile(t, f, nu))),
            pl.BlockSpec((1, tf, d), lambda t, f, te, nr, nu: (te[tile(t, nu)], ftile(t, f, nu), 0)),
        ],
        out_specs=pl.BlockSpec((tm, d), lambda t, f, te, nr, nu: (t, 0)),
        scratch_shapes=[pltpu.VMEM((tm, d), BF16), pltpu.VMEM((tm, d), F32)],
    )
    return pl.pallas_call(
        _expert_ffn_kernel,
        grid_spec=grid_spec,
        out_shape=jax.ShapeDtypeStruct((p_pad, d), F32),
        compiler_params=_cparams(("arbitrary", "arbitrary")),
        name="moe_expert_ffn",
    )(tile_e, tile_rows, n_used, hs, w1, w1, w2)


def _combine_kernel(pos0_ref, pos1_ref, ys_ref, x_ref, info_ref, gate_ref, g_ref, b_ref, o_ref, y0_s, y1_s, sem):
    tm = y0_s.shape[0]
    base = (pl.program_id(0) * pl.num_programs(1) + pl.program_id(1)) * tm

    def row_copy(r, pos_ref, dst):
        return pltpu.make_async_copy(ys_ref.at[pl.ds(pos_ref[base + r], 1), :], dst.at[pl.ds(r, 1), :], sem)

    def start(r, carry):
        row_copy(r, pos0_ref, y0_s).start(priority=0)
        row_copy(r, pos1_ref, y1_s).start(priority=1)
        return carry

    def wait(r, carry):
        row_copy(r, pos0_ref, y0_s).wait()
        row_copy(r, pos1_ref, y1_s).wait()
        return carry

    lax.fori_loop(0, tm, start, 0, unroll=8)
    lax.fori_loop(0, tm, wait, 0, unroll=8)
    info = info_ref[0]
    g0 = info[:, ROUTE_G0:ROUTE_G0 + 1]
    g1 = info[:, ROUTE_G1:ROUTE_G1 + 1]
    f = g0 * y0_s[...] + g1 * y1_s[...]
    z = DN_ALPHA * x_ref[0] + gate_ref[0] * f
    o_ref[0] = _layernorm_rows(z, g_ref[...], b_ref[...])


def _combine_ln(ys, pos0, pos1, x, info, gate, ln_g, ln_b):
    b, t, d = x.shape
    tm = 256
    grid_spec = pltpu.PrefetchScalarGridSpec(
        num_scalar_prefetch=2,
        grid=(b, t // tm),
        in_specs=[
            pl.BlockSpec(memory_space=pl.ANY),
            pl.BlockSpec((1, tm, d), lambda bi, i, p0, p1: (bi, i, 0)),
            pl.BlockSpec((1, tm, LANES), lambda bi, i, p0, p1: (bi, i, 0)),
            pl.BlockSpec((1, 1, d), lambda bi, i, p0, p1: (bi, 0, 0)),
            pl.BlockSpec((1, d), lambda bi, i, p0, p1: (0, 0)),
            pl.BlockSpec((1, d), lambda bi, i, p0, p1: (0, 0)),
        ],
        out_specs=pl.BlockSpec((1, tm, d), lambda bi, i, p0, p1: (bi, i, 0)),
        scratch_shapes=[pltpu.VMEM((tm, d), F32), pltpu.VMEM((tm, d), F32), pltpu.SemaphoreType.DMA],
    )
    return pl.pallas_call(
        _combine_kernel,
        grid_spec=grid_spec,
        out_shape=jax.ShapeDtypeStruct((b, t, d), F32),
        compiler_params=_cparams(("arbitrary", "arbitrary")),
        name="moe_combine_ln",
    )(pos0, pos1, ys, x, info, gate, ln_g.reshape(1, d), ln_b.reshape(1, d))


def _moe_ln(x, scale, shift, w_router, w1, w2, gate, ln_g, ln_b):
    b, t, d = x.shape
    n = b * t
    tm = MOE_TM
    h, info, counts = _router(x, scale, shift, w_router)
    cnt = counts[0, :N_EXPERTS].astype(jnp.int32)
    padded = ((cnt + tm - 1) // tm) * tm
    ends = jnp.cumsum(padded)
    starts = ends - padded
    info2 = info.reshape(n, LANES)
    e0 = info2[:, ROUTE_E0].astype(jnp.int32)
    e1 = info2[:, ROUTE_E1].astype(jnp.int32)
    pos0 = starts[e0] + info2[:, ROUTE_R0].astype(jnp.int32)
    pos1 = starts[e1] + info2[:, ROUTE_R1].astype(jnp.int32)
    n_tiles = (TOP_K * n) // tm + N_EXPERTS
    tile_start = jnp.arange(n_tiles, dtype=jnp.int32) * tm
    tile_e = jnp.minimum(jnp.sum(tile_start[:, None] >= ends[None, :], axis=1), N_EXPERTS - 1).astype(jnp.int32)
    tile_rows = jnp.clip(starts[tile_e] + cnt[tile_e] - tile_start, 0, tm).astype(jnp.int32)
    n_used = (ends[-1] // tm).astype(jnp.int32).reshape(1)
    hs = _dispatch(h.reshape(n, d), pos0, pos1, n_tiles * tm)
    ys = _expert_ffn(hs, tile_e, tile_rows, n_used, w1, w2)
    return _combine_ln(ys, pos0, pos1, x, info, gate, ln_g, ln_b)


def _pair_perm(comp_offsets):
    even = np.concatenate([off + np.arange(0, ROT_DIM, 2) for off in comp_offsets])
    odd = np.concatenate([off + np.arange(1, ROT_DIM, 2) for off in comp_offsets])
    return np.concatenate([even, odd])


def _even_w_in(w_in):
    d = w_in.shape[0]
    a_qk = DA_HEADS * 2 * DA_DH
    head_perm = _pair_perm((0, DA_DH))
    qk_perm = np.concatenate([hh * 2 * DA_DH + head_perm for hh in range(DA_HEADS)])
    cols = np.concatenate([qk_perm, a_qk + qk_perm, np.arange(2 * a_qk, w_in.shape[1])])
    w = w_in[:, cols]
    return jnp.pad(w, ((0, 0), (0, EV_NPAD - w.shape[1]))).astype(BF16)


def _odd_w_in(w_in):
    o_ckv = MLA_Q_RANK
    o_kr = o_ckv + MLA_KV_RANK
    o_rest = o_kr + MLA_ROPE
    ev = o_kr + np.arange(0, MLA_ROPE, 2)
    od = o_kr + np.arange(1, MLA_ROPE, 2)
    cols = np.concatenate([np.arange(0, o_kr), ev, ev, od, od, np.arange(o_rest, w_in.shape[1])])
    return w_in[:, cols].astype(BF16)


def _mla_weights(w_uq, w_ukv):
    hq = MLA_NOPE + MLA_ROPE
    q3 = w_uq.reshape(MLA_Q_RANK, MLA_HEADS, hq).transpose(1, 0, 2)
    zeros = jnp.zeros((MLA_HEADS, MLA_Q_RANK, MLA_ROPE // 2), w_uq.dtype)
    rope = q3[:, :, MLA_NOPE:]
    wq = jnp.concatenate([q3[:, :, :MLA_NOPE], rope[:, :, 0::2], zeros, rope[:, :, 1::2], zeros], axis=-1)
    wkv = w_ukv.reshape(MLA_KV_RANK, MLA_HEADS, MLA_NOPE + MLA_DV).transpose(1, 0, 2)
    return wq.astype(BF16), wkv.astype(BF16)


def _gla_gate_weights(gk_w2, gk_b):
    pairs = GLA_HEADS // 2
    w = jnp.zeros((pairs, 2, LANES, LANES), F32)
    for d in range(2):
        blk = gk_w2[d].reshape(GLA_LR, pairs, LANES).transpose(1, 0, 2)
        w = w.at[:, d, d * GLA_LR:(d + 1) * GLA_LR, :].set(blk)
    bias = gk_b.reshape(2, pairs, 1, LANES).transpose(1, 0, 2, 3).astype(F32)
    return w.astype(BF16), bias


def _rope_tables(rows):
    n_freq = ROT_DIM // 4
    inv = ROPE_BASE ** (-jnp.arange(n_freq, dtype=F32) / n_freq)
    row = jnp.repeat(jnp.arange(rows, dtype=F32), GRID_W)
    col = jnp.tile(jnp.arange(GRID_W, dtype=F32), rows)
    ang = jnp.concatenate([row[:, None] * inv, col[:, None] * inv], axis=-1)
    cos, sin = jnp.cos(ang), jnp.sin(ang)
    return jnp.concatenate([cos] * 4, axis=-1), jnp.concatenate([-sin, -sin, sin, sin], axis=-1)


def _diff_lambda_init(layer):
    return 0.8 - 0.6 * math.exp(-0.3 * layer)


def kernel(x, c, ctx, c_ctx, ada_w, ada_b, post_ln_g, post_ln_b, lb_table, ev_w_in, ev_lam, ev_subln_g, ev_gk_w2,
           ev_gk_b, ev_gla_norm_g, ev_w_out, ev_ffn_w1, ev_ffn_w2, od_w_in, od_q_norm_g, od_kv_norm_g, od_w_uq,
           od_w_ukv, od_hg_norm_g, od_w_out, od_router, od_exp_w1, od_exp_w2):
    b, t, d = x.shape
    tc = ctx.shape[1]
    rope_c, rope_s = _rope_tables(t // GRID_W)
    lb_soft = jax.nn.softmax(lb_table.astype(F32), axis=0)
    lower_bounds = jnp.cumsum(lb_soft, axis=0) - lb_soft[0]

    n_cond = ((b + 1 + 7) // 8) * 8
    cond = jnp.zeros((n_cond, d), F32).at[:b].set(c).at[b].set(c_ctx)
    mods = _ada(cond, ada_w, ada_b).reshape(DEPTH, n_cond, 6, d)

    ctx_flat = None
    for layer in range(DEPTH):
        last = layer == DEPTH - 1
        j = layer // 2
        m_l = [mods[layer, :b, i][:, None, :] for i in range(6)]
        m_c = [jnp.broadcast_to(mods[layer, b, i][None, None, :], (b, 1, d)) for i in range(6)]
        m_c1 = [m[:1] for m in m_c]
        even = layer % 2 == 0
        w_in = _even_w_in(ev_w_in[j]) if even else _odd_w_in(od_w_in[j])
        p_l = _proj(x, 1.0 + m_l[1], m_l[0], w_in)
        p_c = _proj(ctx, 1.0 + m_c[1], m_c[0], w_in)
        need_ctx = not last
        if even:
            lam_init = _diff_lambda_init(layer)
            lv = ev_lam[j].astype(F32)
            lam = jnp.exp(jnp.sum(lv[0] * lv[1])) - jnp.exp(jnp.sum(lv[2] * lv[3])) + lam_init
            oa_c, oa_l = _diff_attention(p_l, p_c, rope_c, rope_s, lam, ev_subln_g[j], lam_init, need_ctx)
            wgk, bgk = _gla_gate_weights(ev_gk_w2[j], ev_gk_b[j])
            ob_c, ob_l = _gla_scan(p_c, p_l, wgk, bgk, ev_gla_norm_g[j], need_ctx)
            w_out = ev_w_out[j].astype(BF16)
        else:
            wq, wkv = _mla_weights(od_w_uq[j], od_w_ukv[j])
            oa_l = _mla_attention(p_l, p_c, rope_c, rope_s, od_q_norm_g[j], od_kv_norm_g[j], wq, wkv)
            oa_c = _mla_attention_ctx(p_c, od_q_norm_g[j], od_kv_norm_g[j], wq, wkv) if need_ctx else None
            ob_c, ob_l = _hgrn_scan(p_c, p_l, lower_bounds[layer], od_hg_norm_g[j], need_ctx)
            w_out = od_w_out[j].astype(BF16)
        g0, b0 = post_ln_g[layer, 0], post_ln_b[layer, 0]
        g1, b1 = post_ln_g[layer, 1], post_ln_b[layer, 1]
        x = _outproj_ln(oa_l, ob_l, w_out, x, m_l[2], g0, b0)
        if even:
            w1, w2 = ev_ffn_w1[j].astype(BF16), ev_ffn_w2[j].astype(BF16)
            x = _ffn_ln(x, 1.0 + m_l[4], m_l[3], w1, w2, m_l[5], g1, b1)
        else:
            w1, w2 = od_exp_w1[j].astype(BF16), od_exp_w2[j].astype(BF16)
            x = _moe_ln(x, 1.0 + m_l[4], m_l[3], od_router[j], w1, w2, m_l[5], g1, b1)
        if need_ctx:
            ctx = _outproj_ln(oa_c, ob_c, w_out, ctx, m_c[2], g0, b0)
            ctx_flat = ctx.reshape(1, b * tc, d)
            if even:
                ctx_flat = _ffn_ln(ctx_flat, 1.0 + m_c1[4], m_c1[3], w1, w2, m_c1[5], g1, b1)
            else:
                ctx_flat = _moe_ln(ctx_flat, 1.0 + m_c1[4], m_c1[3], od_router[j], w1, w2, m_c1[5], g1, b1)
            ctx = ctx_flat.reshape(b, tc, d)
    return x
```

```python
import functools
import math

import jax
import jax.numpy as jnp
import numpy as np
from jax import lax
from jax.experimental import pallas as pl
from jax.experimental.pallas import tpu as pltpu

F32 = jnp.float32
BF16 = jnp.bfloat16

DEPTH = 2
GRID_W = 64
ROT_DIM = 64
ROPE_BASE = 10000.0
DA_HEADS = 4
DA_DH = ROT_DIM
DA_DV = 2 * DA_DH
GLA_HEADS = 4
GLA_DK = 64
GLA_DV = 128
GLA_LR = 16
GLA_NORMALIZER = 16.0
MLA_HEADS = 4
MLA_Q_RANK = 256
MLA_KV_RANK = 128
MLA_NOPE = 128
MLA_ROPE = ROT_DIM
MLA_DV = 128
MLA_SCALE = (MLA_NOPE + MLA_ROPE) ** -0.5
HG_HEADS = 4
HG_DK = 128
HG_DV = 128
D_FF = 3584
N_EXPERTS = 8
TOP_K = 2
LN_EPS = 1e-5
RMS_EPS = 1e-6
DN_ALPHA = (2 * DEPTH) ** 0.25

LANES = 128
VMEM_LIMIT = 56 * 1024 * 1024
ATTN_TQ = 256
SCAN_BLOCK = 256
SCAN_CHUNK = 32
MOE_TM = 1024
FFN_TF = 512

EV_QA, EV_KA, EV_VA, EV_QB, EV_KB, EV_VB, EV_GB, EV_LR = 0, 4, 8, 12, 14, 16, 20, 24
EV_NPAD = 25 * LANES
OD_CQ, OD_CKV, OD_KR, OD_HQ, OD_FF, OD_FB, OD_HI, OD_HG = 0, 2, 3, 4, 8, 12, 16, 20
OD_NPAD = 24 * LANES


def _cparams(sem, flags=None):
    return pltpu.CompilerParams(dimension_semantics=sem, vmem_limit_bytes=VMEM_LIMIT, flags=flags)


def _silu(v):
    return v * jax.nn.sigmoid(v)


def _layernorm_rows(z, g, b):
    mu = jnp.mean(z, axis=-1, keepdims=True)
    zc = z - mu
    var = jnp.mean(zc * zc, axis=-1, keepdims=True)
    return zc * lax.rsqrt(var + LN_EPS) * g + b


def _rmsnorm_rows(v, g):
    return v * lax.rsqrt(jnp.mean(v * v, axis=-1, keepdims=True) + RMS_EPS) * g


def _dot_nt(a, b):
    return lax.dot_general(a, b, (((1,), (1,)), ((), ())), preferred_element_type=F32)


def _dot_tn(a, b):
    return lax.dot_general(a, b, (((0,), (0,)), ((), ())), preferred_element_type=F32)


def _dot(a, b):
    return jnp.dot(a, b, preferred_element_type=F32)


def _ada_kernel(c_ref, w_ref, b_ref, o_ref):
    s = _silu(c_ref[...]).astype(BF16)
    o_ref[0] = _dot(s, w_ref[0].astype(BF16)) + b_ref[0]


def _ada(cond, ada_w, ada_b):
    depth, d, n = ada_w.shape
    r = cond.shape[0]
    tn = n // 4
    return pl.pallas_call(
        _ada_kernel,
        grid=(depth, n // tn),
        in_specs=[
            pl.BlockSpec((r, d), lambda l, j: (0, 0)),
            pl.BlockSpec((1, d, tn), lambda l, j: (l, 0, j)),
            pl.BlockSpec((1, 1, tn), lambda l, j: (l, 0, j)),
        ],
        out_specs=pl.BlockSpec((1, r, tn), lambda l, j: (l, 0, j)),
        out_shape=jax.ShapeDtypeStruct((depth, r, n), F32),
        compiler_params=_cparams(("parallel", "parallel")),
        name="ada_modulation",
    )(cond, ada_w, ada_b.reshape(depth, 1, n))


def _proj_kernel(x_ref, sc_ref, sh_ref, w_ref, o_ref):
    h = (x_ref[0] * sc_ref[0] + sh_ref[0]).astype(BF16)
    o_ref[0] = _dot(h, w_ref[...]).astype(o_ref.dtype)


def _proj(x, scale, shift, w):
    b, t, d = x.shape
    n = w.shape[1]
    tm = min(t, 512)
    return pl.pallas_call(
        _proj_kernel,
        grid=(b, t // tm),
        in_specs=[
            pl.BlockSpec((1, tm, d), lambda bi, i: (bi, i, 0)),
            pl.BlockSpec((1, 1, d), lambda bi, i: (bi, 0, 0)),
            pl.BlockSpec((1, 1, d), lambda bi, i: (bi, 0, 0)),
            pl.BlockSpec((d, n), lambda bi, i: (0, 0)),
        ],
        out_specs=pl.BlockSpec((1, tm, n), lambda bi, i: (bi, i, 0)),
        out_shape=jax.ShapeDtypeStruct((b, t, n), BF16),
        compiler_params=_cparams(("parallel", "parallel")),
        name="mod_proj",
    )(x, scale, shift, w)


def _rope128(t, cs, sn):
    t = t.astype(F32)
    return t * cs + pltpu.roll(t, LANES // 2, axis=1) * sn


LOG2E = math.log2(math.e)


def _softmax_pv(s2, v_bf):
    m = jnp.max(s2, axis=-1, keepdims=True)
    e = jnp.exp2(s2 - m)
    return _dot(e.astype(BF16), v_bf) / jnp.sum(e, axis=-1, keepdims=True)


def _q1_lane_mask(shape):
    lane = lax.broadcasted_iota(jnp.int32, shape, 1)
    return (lane // 32) % 2 == 0


def _diff_scores_out(q, k_bf, v_bf, lam):
    m1 = _q1_lane_mask(q.shape)
    q1 = jnp.where(m1, q, 0.0).astype(BF16)
    q2 = jnp.where(m1, 0.0, q).astype(BF16)
    s1 = _dot_nt(q1, k_bf)
    s2 = _dot_nt(q2, k_bf)
    e1 = jnp.exp2(s1 - jnp.max(s1, axis=-1, keepdims=True))
    e2 = jnp.exp2(s2 - jnp.max(s2, axis=-1, keepdims=True))
    r1 = 1.0 / jnp.sum(e1, axis=-1, keepdims=True)
    r2 = lam / jnp.sum(e2, axis=-1, keepdims=True)
    return _dot((e1 * r1 - e2 * r2).astype(BF16), v_bf)


def _diffattn_lat_kernel(lam_init, tc, q_ref, kl_ref, vl_ref, kc_ref, vc_ref, cq_ref, sq_ref, ck_ref, sk_ref,
                         lam_ref, g_ref, o_ref, k_s, v_s):
    @pl.when(pl.program_id(2) == 0)
    def _():
        k_s[0:tc, :] = kc_ref[0].astype(BF16)
        v_s[0:tc, :] = vc_ref[0].astype(BF16)
        k_s[tc:, :] = _rope128(kl_ref[0], ck_ref[...], sk_ref[...]).astype(BF16)
        v_s[tc:, :] = vl_ref[0].astype(BF16)

    q = _rope128(q_ref[0], cq_ref[...], sq_ref[...]) * (DA_DH ** -0.5 * LOG2E)
    o = _diff_scores_out(q, k_s[...], v_s[...], lam_ref[0, 0])
    o_ref[0] = (_rmsnorm_rows(o, g_ref[...]) * (1.0 - lam_init)).astype(o_ref.dtype)


def _diffattn_ctx_kernel(lam_init, q_ref, k_ref, v_ref, lam_ref, g_ref, o_ref):
    q = q_ref[0].astype(F32) * (DA_DH ** -0.5 * LOG2E)
    o = _diff_scores_out(q, k_ref[0].astype(BF16), v_ref[0].astype(BF16), lam_ref[0, 0])
    o_ref[0] = (_rmsnorm_rows(o, g_ref[...]) * (1.0 - lam_init)).astype(o_ref.dtype)


def _diff_attention(p_l, p_c, rope_c, rope_s, lam, subln_g, lam_init, need_ctx):
    b, tl, _ = p_l.shape
    tc = p_c.shape[1]
    h = DA_HEADS
    tq = min(tl, ATTN_TQ)
    lam2 = lam.reshape(1, 1).astype(F32)
    g2 = subln_g.reshape(1, DA_DV).astype(F32)
    smem = pl.BlockSpec(memory_space=pltpu.SMEM)
    o_l = pl.pallas_call(
        functools.partial(_diffattn_lat_kernel, lam_init, tc),
        grid=(b, h, tl // tq),
        in_specs=[
            pl.BlockSpec((1, tq, LANES), lambda bi, hi, i: (bi, i, EV_QA + hi)),
            pl.BlockSpec((1, tl, LANES), lambda bi, hi, i: (bi, 0, EV_KA + hi)),
            pl.BlockSpec((1, tl, LANES), lambda bi, hi, i: (bi, 0, EV_VA + hi)),
            pl.BlockSpec((1, tc, LANES), lambda bi, hi, i: (bi, 0, EV_KA + hi)),
            pl.BlockSpec((1, tc, LANES), lambda bi, hi, i: (bi, 0, EV_VA + hi)),
            pl.BlockSpec((tq, LANES), lambda bi, hi, i: (i, 0)),
            pl.BlockSpec((tq, LANES), lambda bi, hi, i: (i, 0)),
            pl.BlockSpec((tl, LANES), lambda bi, hi, i: (0, 0)),
            pl.BlockSpec((tl, LANES), lambda bi, hi, i: (0, 0)),
            smem,
            pl.BlockSpec((1, LANES), lambda bi, hi, i: (0, 0)),
        ],
        out_specs=pl.BlockSpec((1, tq, LANES), lambda bi, hi, i: (bi, i, hi)),
        out_shape=jax.ShapeDtypeStruct((b, tl, h * DA_DV), BF16),
        scratch_shapes=[pltpu.VMEM((tc + tl, LANES), BF16), pltpu.VMEM((tc + tl, LANES), BF16)],
        compiler_params=_cparams(("parallel", "parallel", "arbitrary")),
        name="diff_attention_latent",
    )(p_l, p_l, p_l, p_c, p_c, rope_c, rope_s, rope_c, rope_s, lam2, g2)
    if not need_ctx:
        return None, o_l
    o_c = pl.pallas_call(
        functools.partial(_diffattn_ctx_kernel, lam_init),
        grid=(b, h),
        in_specs=[
            pl.BlockSpec((1, tc, LANES), lambda bi, hi: (bi, 0, EV_QA + hi)),
            pl.BlockSpec((1, tc, LANES), lambda bi, hi: (bi, 0, EV_KA + hi)),
            pl.BlockSpec((1, tc, LANES), lambda bi, hi: (bi, 0, EV_VA + hi)),
            smem,
            pl.BlockSpec((1, LANES), lambda bi, hi: (0, 0)),
        ],
        out_specs=pl.BlockSpec((1, tc, LANES), lambda bi, hi: (bi, 0, hi)),
        out_shape=jax.ShapeDtypeStruct((b, tc, h * DA_DV), BF16),
        compiler_params=_cparams(("parallel", "parallel")),
        name="diff_attention_ctx",
    )(p_c, p_c, p_c, lam2, g2)
    return o_c, o_l


def _mla_q(cq, qg, wq, cs, sn):
    q = _dot(_rmsnorm_rows(cq.astype(F32), qg).astype(BF16), wq)
    if cs is not None:
        q = jnp.concatenate([q[:, :LANES], _rope128(q[:, LANES:], cs, sn)], axis=1)
    return (q * (MLA_SCALE * LOG2E)).astype(BF16)


def _mla_kv(ckv, kr, kvg, wkv, cs, sn):
    kv = _dot(_rmsnorm_rows(ckv.astype(F32), kvg).astype(BF16), wkv)
    if cs is not None:
        kr = _rope128(kr, cs, sn)
    k = jnp.concatenate([kv[:, :LANES].astype(BF16), kr.astype(BF16)], axis=1)
    return k, kv[:, LANES:].astype(BF16)


def _mla_lat_kernel(tc, cq_ref, ckvl_ref, krl_ref, ckvc_ref, krc_ref, cq_c_ref, cq_s_ref, ck_ref, sk_ref,
                    qg_ref, kvg_ref, wq_ref, wkv_ref, o_ref, k_s, v_s):
    @pl.when(pl.program_id(2) == 0)
    def _():
        kc, vc = _mla_kv(ckvc_ref[0], krc_ref[0], kvg_ref[...], wkv_ref[0], None, None)
        k_s[0:tc, :] = kc
        v_s[0:tc, :] = vc
        kl, vl = _mla_kv(ckvl_ref[0], krl_ref[0], kvg_ref[...], wkv_ref[0], ck_ref[...], sk_ref[...])
        k_s[tc:, :] = kl
        v_s[tc:, :] = vl

    q = _mla_q(cq_ref[0], qg_ref[...], wq_ref[0], cq_c_ref[...], cq_s_ref[...])
    o_ref[0] = _softmax_pv(_dot_nt(q, k_s[...]), v_s[...]).astype(o_ref.dtype)


def _mla_attention(p_l, p_c, rope_c, rope_s, q_norm_g, kv_norm_g, wq, wkv):
    b, tl, _ = p_l.shape
    tc = p_c.shape[1]
    h = MLA_HEADS
    tq = min(tl, ATTN_TQ)
    return pl.pallas_call(
        functools.partial(_mla_lat_kernel, tc),
        grid=(b, h, tl // tq),
        in_specs=[
            pl.BlockSpec((1, tq, MLA_Q_RANK), lambda bi, hi, i: (bi, i, OD_CQ)),
            pl.BlockSpec((1, tl, LANES), lambda bi, hi, i: (bi, 0, OD_CKV)),
            pl.BlockSpec((1, tl, LANES), lambda bi, hi, i: (bi, 0, OD_KR)),
            pl.BlockSpec((1, tc, LANES), lambda bi, hi, i: (bi, 0, OD_CKV)),
            pl.BlockSpec((1, tc, LANES), lambda bi, hi, i: (bi, 0, OD_KR)),
            pl.BlockSpec((tq, LANES), lambda bi, hi, i: (i, 0)),
            pl.BlockSpec((tq, LANES), lambda bi, hi, i: (i, 0)),
            pl.BlockSpec((tl, LANES), lambda bi, hi, i: (0, 0)),
            pl.BlockSpec((tl, LANES), lambda bi, hi, i: (0, 0)),
            pl.BlockSpec((1, MLA_Q_RANK), lambda bi, hi, i: (0, 0)),
            pl.BlockSpec((1, MLA_KV_RANK), lambda bi, hi, i: (0, 0)),
            pl.BlockSpec((1, MLA_Q_RANK, 2 * LANES), lambda bi, hi, i: (hi, 0, 0)),
            pl.BlockSpec((1, MLA_KV_RANK, 2 * LANES), lambda bi, hi, i: (hi, 0, 0)),
        ],
        out_specs=pl.BlockSpec((1, tq, LANES), lambda bi, hi, i: (bi, i, hi)),
        out_shape=jax.ShapeDtypeStruct((b, tl, h * MLA_DV), BF16),
        scratch_shapes=[pltpu.VMEM((tc + tl, 2 * LANES), BF16), pltpu.VMEM((tc + tl, LANES), BF16)],
        compiler_params=_cparams(("parallel", "parallel", "arbitrary")),
        name="mla_attention_latent",
    )(p_l, p_l, p_l, p_c, p_c, rope_c, rope_s, rope_c, rope_s,
      q_norm_g.reshape(1, -1), kv_norm_g.reshape(1, -1), wq, wkv)


def _mla_ctx_kernel(cq_ref, ckv_ref, kr_ref, qg_ref, kvg_ref, wq_ref, wkv_ref, o_ref):
    k, v = _mla_kv(ckv_ref[0], kr_ref[0], kvg_ref[...], wkv_ref[0], None, None)
    q = _mla_q(cq_ref[0], qg_ref[...], wq_ref[0], None, None)
    o_ref[0] = _softmax_pv(_dot_nt(q, k), v).astype(o_ref.dtype)


def _mla_attention_ctx(p_c, q_norm_g, kv_norm_g, wq, wkv):
    b, tc, _ = p_c.shape
    h = MLA_HEADS
    return pl.pallas_call(
        _mla_ctx_kernel,
        grid=(b, h),
        in_specs=[
            pl.BlockSpec((1, tc, MLA_Q_RANK), lambda bi, hi: (bi, 0, OD_CQ)),
            pl.BlockSpec((1, tc, LANES), lambda bi, hi: (bi, 0, OD_CKV)),
            pl.BlockSpec((1, tc, LANES), lambda bi, hi: (bi, 0, OD_KR)),
            pl.BlockSpec((1, MLA_Q_RANK), lambda bi, hi: (0, 0)),
            pl.BlockSpec((1, MLA_KV_RANK), lambda bi, hi: (0, 0)),
            pl.BlockSpec((1, MLA_Q_RANK, 2 * LANES), lambda bi, hi: (hi, 0, 0)),
            pl.BlockSpec((1, MLA_KV_RANK, 2 * LANES), lambda bi, hi: (hi, 0, 0)),
        ],
        out_specs=pl.BlockSpec((1, tc, LANES), lambda bi, hi: (bi, 0, hi)),
        out_shape=jax.ShapeDtypeStruct((b, tc, h * MLA_DV), BF16),
        compiler_params=_cparams(("parallel", "parallel")),
        name="mla_attention_ctx",
    )(p_c, p_c, p_c, q_norm_g.reshape(1, -1), kv_norm_g.reshape(1, -1), wq, wkv)


def _scan_levels(bt):
    levels = []
    c = SCAN_CHUNK
    while c <= bt:
        levels.append(c)
        c *= 2
    return levels


def _level_table(bt, reverse):
    i = np.arange(bt)[:, None]
    j = np.arange(bt)[None, :]
    if reverse:
        i, j = j, i
    tab = np.zeros((bt, bt), np.int32)
    for lvl, c in enumerate(_scan_levels(bt), start=1):
        same = (i // c) == (j // c)
        if lvl == 1:
            m = same & (j <= i)
        else:
            m = same & ((i % c) >= c // 2) & ((j % c) < c // 2)
        tab[m] = lvl
    return tab


def _chunk_row(a, c, r):
    bt, n = a.shape
    a3 = a.reshape(bt // c, c, n)
    return jnp.broadcast_to(a3[:, r:r + 1, :], (bt // c, c, n)).reshape(bt, n)


def _scan_block(q_s, k_s, g_s, v_s, o_s, tri_ref, lvl_ref, start, st, reverse, compute_out):
    bt = SCAN_BLOCK
    rows = pl.ds(start, bt)
    g = g_s[rows, :]
    k = k_s[rows, :]
    v = v_s[rows, :]
    g_hi = g.astype(BF16)
    g_lo = (g - g_hi.astype(F32)).astype(BF16)
    tri = tri_ref[...]
    gc = _dot(tri, g_hi) + _dot(tri, g_lo)
    g_tot = gc[0:1, :] if reverse else gc[bt - 1:bt, :]
    kd = (k * jnp.exp(g_tot - gc)).astype(BF16)
    st_new = st * jnp.exp(g_tot) + _dot_tn(v, kd)
    if compute_out:
        q = q_s[rows, :]
        o = _dot_nt((q * jnp.exp(gc)).astype(BF16), st.astype(BF16))
        lvl = lvl_ref[...]
        att = jnp.zeros((bt, bt), F32)
        for li, c in enumerate(_scan_levels(bt), start=1):
            r = _chunk_row(gc, c, c // 2 if reverse else c // 2 - 1)
            a = _dot_nt((q * jnp.exp(gc - r)).astype(BF16), (k * jnp.exp(r - gc)).astype(BF16))
            att = jnp.where(lvl == li, a, att)
        o_s[rows, :] = o + _dot(att.astype(BF16), v)
    return st_new


def _scan_all(chains, tri_refs, lvl_refs, tc, tl, need_ctx):
    bt = SCAN_BLOCK
    nc, nl = tc // bt, tl // bt

    def run(first, n, carry, compute_out):
        def body(i, carry):
            new = []
            for ci, ch in enumerate(chains):
                for d in range(2):
                    blk = first + (n - 1 - i if d else i)
                    start = pl.multiple_of(blk * bt, bt)
                    new.append(_scan_block(ch["q"], ch["k"][d], ch["g"][d], ch["v"], ch["o"][d], tri_refs[d],
                                           lvl_refs[d], start, carry[2 * ci + d], d == 1, compute_out))
            return tuple(new)
        return lax.fori_loop(0, n, body, carry)

    zero = jnp.zeros((LANES, LANES), F32)
    carry = run(0, nc, (zero,) * (2 * len(chains)), need_ctx)
    run(nc, nl, carry, True)


def _scan_finish(chain, cols, gate_c_ref, gate_l_ref, ng_ref, oc_ref, ol_ref, tc, need_ctx):
    ng = ng_ref[...]
    of_s, ob_s = chain["o"]
    if need_ctx:
        o = of_s[0:tc, :] + ob_s[0:tc, :]
        oc_ref[0, :, cols] = (_rmsnorm_rows(o, ng) * _silu(gate_c_ref[0, :, cols].astype(F32))).astype(oc_ref.dtype)
    o = of_s[tc:, :] + ob_s[tc:, :]
    ol_ref[0, :, cols] = (_rmsnorm_rows(o, ng) * _silu(gate_l_ref[0, :, cols].astype(F32))).astype(ol_ref.dtype)


SCAN_HEADS = 2
SCAN_SCRATCH = 8


def _scan_unpack(refs, n_in, need_ctx):
    ins = refs[:n_in]
    if need_ctx:
        oc_ref, ol_ref = refs[n_in:n_in + 2]
        scratch = refs[n_in + 2:]
    else:
        oc_ref, ol_ref = None, refs[n_in]
        scratch = refs[n_in + 1:]
    chains = []
    for hh in range(SCAN_HEADS):
        q_s, v_s, kf_s, kb_s, gf_s, gb_s, of_s, ob_s = scratch[SCAN_SCRATCH * hh:SCAN_SCRATCH * (hh + 1)]
        chains.append({"q": q_s, "v": v_s, "k": (kf_s, kb_s), "g": (gf_s, gb_s), "o": (of_s, ob_s)})
    return ins, oc_ref, ol_ref, chains


def _gla_scan_kernel(tc, tl, need_ctx, *refs):
    ins, oc_ref, ol_ref, chains = _scan_unpack(refs, 17, need_ctx)
    (qc_ref, kc_ref, vc_ref, gbc_ref, lrc_ref, ql_ref, kl_ref, vl_ref, gbl_ref, lrl_ref,
     wgk_ref, bgk_ref, ng_ref, trif_ref, trib_ref, lvlf_ref, lvlb_ref) = ins
    lane = lax.broadcasted_iota(jnp.int32, (1, LANES), 1)
    for hh, ch in enumerate(chains):
        cols = slice(hh * LANES, (hh + 1) * LANES)
        mine = (lane // GLA_DK) == hh
        q_s, v_s, (k_s, _), (gf_s, gb_s) = ch["q"], ch["v"], ch["k"], ch["g"]
        ch["k"] = (k_s, k_s)
        for (q_ref, k_ref, v_ref), lo, hi in (((qc_ref, kc_ref, vc_ref), 0, tc),
                                              ((ql_ref, kl_ref, vl_ref), tc, tc + tl)):
            q_s[lo:hi, :] = jnp.where(mine, q_ref[0].astype(F32), 0.0) * (GLA_DK ** -0.5)
            k_s[lo:hi, :] = jnp.where(mine, k_ref[0].astype(F32), 0.0)
            v_s[lo:hi, :] = v_ref[0, :, cols]
        for d, g_s in enumerate((gf_s, gb_s)):
            for lr_ref, lo, hi in ((lrc_ref, 0, tc), (lrl_ref, tc, tc + tl)):
                z = _dot(lr_ref[0], wgk_ref[0, d]) + bgk_ref[0, d]
                g_s[lo:hi, :] = jnp.where(mine, jax.nn.log_sigmoid(z) / GLA_NORMALIZER, 0.0)
    _scan_all(chains, (trif_ref, trib_ref), (lvlf_ref, lvlb_ref), tc, tl, need_ctx)
    for hh, ch in enumerate(chains):
        _scan_finish(ch, slice(hh * LANES, (hh + 1) * LANES), gbc_ref, gbl_ref, ng_ref, oc_ref, ol_ref, tc, need_ctx)


def _hgrn_scan_kernel(tc, tl, need_ctx, *refs):
    ins, oc_ref, ol_ref, chains = _scan_unpack(refs, 16, need_ctx)
    (qc_ref, ffc_ref, fbc_ref, vc_ref, gtc_ref, ql_ref, ffl_ref, fbl_ref, vl_ref, gtl_ref,
     lb_ref, ng_ref, trif_ref, trib_ref, lvlf_ref, lvlb_ref) = ins
    for hh, ch in enumerate(chains):
        cols = slice(hh * LANES, (hh + 1) * LANES)
        lb = lb_ref[:, cols]
        for (q_ref, v_ref), lo, hi in (((qc_ref, vc_ref), 0, tc), ((ql_ref, vl_ref), tc, tc + tl)):
            ch["q"][lo:hi, :] = q_ref[0, :, cols].astype(F32)
            ch["v"][lo:hi, :] = v_ref[0, :, cols]
        for (fc_ref, fl_ref), k_s, g_s in (((ffc_ref, ffl_ref), ch["k"][0], ch["g"][0]),
                                           ((fbc_ref, fbl_ref), ch["k"][1], ch["g"][1])):
            for f_ref, lo, hi in ((fc_ref, 0, tc), (fl_ref, tc, tc + tl)):
                f = lb + (1.0 - lb) * jax.nn.sigmoid(f_ref[0, :, cols].astype(F32))
                k_s[lo:hi, :] = 1.0 - f
                g_s[lo:hi, :] = jnp.log(f)
    _scan_all(chains, (trif_ref, trib_ref), (lvlf_ref, lvlb_ref), tc, tl, need_ctx)
    for hh, ch in enumerate(chains):
        _scan_finish(ch, slice(hh * LANES, (hh + 1) * LANES), gtc_ref, gtl_ref, ng_ref, oc_ref, ol_ref, tc, need_ctx)


def _scan_consts():
    bt = SCAN_BLOCK
    lower = np.tril(np.ones((bt, bt), np.float32))
    return (jnp.asarray(lower, BF16), jnp.asarray(lower.T, BF16),
            jnp.asarray(_level_table(bt, False)), jnp.asarray(_level_table(bt, True)))


def _scan_call(kernel_fn, name, p_c, p_l, col_specs, extra, extra_specs, need_ctx, heads):
    b, tl, _ = p_l.shape
    tc = p_c.shape[1]
    bt = SCAN_BLOCK
    t = tc + tl
    wide = SCAN_HEADS * LANES
    consts = _scan_consts()
    const_specs = [pl.BlockSpec((bt, bt), lambda bi, ji: (0, 0)) for _ in consts]
    in_specs = ([pl.BlockSpec((1, tc, g * LANES), f) for g, f in col_specs]
                + [pl.BlockSpec((1, tl, g * LANES), f) for g, f in col_specs] + extra_specs + const_specs)
    args = [p_c] * len(col_specs) + [p_l] * len(col_specs) + list(extra) + list(consts)
    out_l = jax.ShapeDtypeStruct((b, tl, heads * LANES), BF16)
    spec_l = pl.BlockSpec((1, tl, wide), lambda bi, ji: (bi, 0, ji))
    if need_ctx:
        out_shape = (jax.ShapeDtypeStruct((b, tc, heads * LANES), BF16), out_l)
        out_specs = (pl.BlockSpec((1, tc, wide), lambda bi, ji: (bi, 0, ji)), spec_l)
    else:
        out_shape, out_specs = out_l, spec_l
    per_head = [pltpu.VMEM((t, LANES), F32), pltpu.VMEM((t, LANES), BF16)] + [pltpu.VMEM((t, LANES), F32)] * 6
    res = pl.pallas_call(
        functools.partial(kernel_fn, tc, tl, need_ctx),
        grid=(b, heads // SCAN_HEADS),
        in_specs=in_specs,
        out_specs=out_specs,
        out_shape=out_shape,
        scratch_shapes=per_head * SCAN_HEADS,
        compiler_params=_cparams(("parallel", "parallel")),
        name=name,
    )(*args)
    return res if need_ctx else (None, res)


def _gla_scan(p_c, p_l, wgk, bgk, norm_g, need_ctx):
    col_specs = [
        (1, lambda bi, ji: (bi, 0, EV_QB + ji)),
        (1, lambda bi, ji: (bi, 0, EV_KB + ji)),
        (SCAN_HEADS, lambda bi, ji: (bi, 0, EV_VB // SCAN_HEADS + ji)),
        (SCAN_HEADS, lambda bi, ji: (bi, 0, EV_GB // SCAN_HEADS + ji)),
        (1, lambda bi, ji: (bi, 0, EV_LR)),
    ]
    extra_specs = [
        pl.BlockSpec((1, 2, LANES, LANES), lambda bi, ji: (ji, 0, 0, 0)),
        pl.BlockSpec((1, 2, 1, LANES), lambda bi, ji: (ji, 0, 0, 0)),
        pl.BlockSpec((1, LANES), lambda bi, ji: (0, 0)),
    ]
    return _scan_call(_gla_scan_kernel, "gla_scan", p_c, p_l, col_specs,
                      (wgk, bgk, norm_g.reshape(1, LANES)), extra_specs, need_ctx, GLA_HEADS)


def _hgrn_scan(p_c, p_l, lb, norm_g, need_ctx):
    col_specs = [(SCAN_HEADS, functools.partial(lambda base, bi, ji: (bi, 0, base // SCAN_HEADS + ji), base))
                 for base in (OD_HQ, OD_FF, OD_FB, OD_HI, OD_HG)]
    extra_specs = [
        pl.BlockSpec((1, SCAN_HEADS * LANES), lambda bi, ji: (0, ji)),
        pl.BlockSpec((1, LANES), lambda bi, ji: (0, 0)),
    ]
    return _scan_call(_hgrn_scan_kernel, "hgrn2_scan", p_c, p_l, col_specs,
                      (lb.reshape(1, -1), norm_g.reshape(1, LANES)), extra_specs, need_ctx, HG_HEADS)


def _outproj_kernel(oa_ref, ob_ref, w_ref, x_ref, gate_ref, g_ref, b_ref, o_ref):
    ka = oa_ref.shape[2]
    y = _dot(oa_ref[0], w_ref[0:ka, :]) + _dot(ob_ref[0], w_ref[ka:, :])
    z = DN_ALPHA * x_ref[0] + gate_ref[0] * y
    o_ref[0] = _layernorm_rows(z, g_ref[...], b_ref[...])


def _outproj_ln(oa, ob, w, x, gate, ln_g, ln_b):
    b, t, d = x.shape
    ka, kb = oa.shape[2], ob.shape[2]
    tm = min(t, 512)
    return pl.pallas_call(
        _outproj_kernel,
        grid=(b, t // tm),
        in_specs=[
            pl.BlockSpec((1, tm, ka), lambda bi, i: (bi, i, 0)),
            pl.BlockSpec((1, tm, kb), lambda bi, i: (bi, i, 0)),
            pl.BlockSpec((ka + kb, d), lambda bi, i: (0, 0)),
            pl.BlockSpec((1, tm, d), lambda bi, i: (bi, i, 0)),
            pl.BlockSpec((1, 1, d), lambda bi, i: (bi, 0, 0)),
            pl.BlockSpec((1, d), lambda bi, i: (0, 0)),
            pl.BlockSpec((1, d), lambda bi, i: (0, 0)),
        ],
        out_specs=pl.BlockSpec((1, tm, d), lambda bi, i: (bi, i, 0)),
        out_shape=jax.ShapeDtypeStruct((b, t, d), F32),
        compiler_params=_cparams(("parallel", "parallel")),
        name="outproj_residual_ln",
    )(oa, ob, w, x, gate, ln_g.reshape(1, d), ln_b.reshape(1, d))


def _ffn_kernel(x_ref, sc_ref, sh_ref, w1g_ref, w1u_ref, w2_ref, gate_ref, g_ref, b_ref, o_ref, h_s, acc_s):
    f = pl.program_id(2)

    @pl.when(f == 0)
    def _():
        h_s[...] = (x_ref[0] * sc_ref[0] + sh_ref[0]).astype(BF16)
        acc_s[...] = jnp.zeros_like(acc_s)

    h = h_s[...]
    a = _silu(_dot(h, w1g_ref[...].astype(BF16))) * _dot(h, w1u_ref[...].astype(BF16))
    acc_s[...] += _dot(a.astype(BF16), w2_ref[...].astype(BF16))

    @pl.when(f == pl.num_programs(2) - 1)
    def _():
        z = DN_ALPHA * x_ref[0] + gate_ref[0] * acc_s[...]
        o_ref[0] = _layernorm_rows(z, g_ref[...], b_ref[...])


def _ffn_ln(x, scale, shift, w1, w2, gate, ln_g, ln_b):
    b, t, d = x.shape
    ff = w2.shape[0]
    tf = FFN_TF
    nf = ff // tf
    tm = min(t, 1024)
    return pl.pallas_call(
        _ffn_kernel,
        grid=(b, t // tm, nf),
        in_specs=[
            pl.BlockSpec((1, tm, d), lambda bi, i, f: (bi, i, 0)),
            pl.BlockSpec((1, 1, d), lambda bi, i, f: (bi, 0, 0)),
            pl.BlockSpec((1, 1, d), lambda bi, i, f: (bi, 0, 0)),
            pl.BlockSpec((d, tf), lambda bi, i, f: (0, f)),
            pl.BlockSpec((d, tf), lambda bi, i, f: (0, nf + f)),
            pl.BlockSpec((tf, d), lambda bi, i, f: (f, 0)),
            pl.BlockSpec((1, 1, d), lambda bi, i, f: (bi, 0, 0)),
            pl.BlockSpec((1, d), lambda bi, i, f: (0, 0)),
            pl.BlockSpec((1, d), lambda bi, i, f: (0, 0)),
        ],
        out_specs=pl.BlockSpec((1, tm, d), lambda bi, i, f: (bi, i, 0)),
        out_shape=jax.ShapeDtypeStruct((b, t, d), F32),
        scratch_shapes=[pltpu.VMEM((tm, d), BF16), pltpu.VMEM((tm, d), F32)],
        compiler_params=_cparams(("parallel", "parallel", "arbitrary")),
        name="swiglu_residual_ln",
    )(x, scale, shift, w1, w1, w2, gate, ln_g.reshape(1, d), ln_b.reshape(1, d))


ROUTE_E0, ROUTE_E1, ROUTE_G0, ROUTE_G1, ROUTE_R0, ROUTE_R1 = range(6)


def _router_kernel(x_ref, sc_ref, sh_ref, wr_ref, h_ref, info_ref, cnt_ref, carry_s):
    first = (pl.program_id(0) == 0) & (pl.program_id(1) == 0)

    @pl.when(first)
    def _():
        carry_s[...] = jnp.zeros_like(carry_s)

    h = x_ref[0] * sc_ref[0] + sh_ref[0]
    h_ref[0] = h
    tm = h.shape[0]
    logits = jnp.dot(h, wr_ref[...], preferred_element_type=F32, precision=lax.Precision.HIGHEST)
    lane = lax.broadcasted_iota(jnp.int32, (tm, LANES), 1).astype(F32)
    neg = jnp.float32(-jnp.inf)
    logits = jnp.where(lane < N_EXPERTS, logits, neg)
    v0 = jnp.max(logits, axis=-1, keepdims=True)
    e0 = jnp.min(jnp.where(logits == v0, lane, float(LANES)), axis=-1, keepdims=True)
    rest = jnp.where(lane == e0, neg, logits)
    v1 = jnp.max(rest, axis=-1, keepdims=True)
    e1 = jnp.min(jnp.where(rest == v1, lane, float(LANES)), axis=-1, keepdims=True)
    d = jnp.exp(v1 - v0)
    g0 = 1.0 / (1.0 + d)
    g1 = d / (1.0 + d)
    oh0 = (lane == e0).astype(BF16)
    oh1 = (lane == e1).astype(BF16)
    ri = lax.broadcasted_iota(jnp.int32, (tm, tm), 0)
    ci = lax.broadcasted_iota(jnp.int32, (tm, tm), 1)
    before = (ci < ri).astype(BF16)
    c0 = _dot(before, oh0)
    c1 = _dot(before, oh1)
    tot0 = jnp.sum(oh0.astype(F32), axis=0, keepdims=True)
    tot1 = jnp.sum(oh1.astype(F32), axis=0, keepdims=True)
    carry = carry_s[...]
    r0 = jnp.sum(jnp.where(lane == e0, carry + c0, 0.0), axis=-1, keepdims=True)
    r1 = jnp.sum(jnp.where(lane == e1, carry + tot0 + c1, 0.0), axis=-1, keepdims=True)
    carry = carry + tot0 + tot1
    carry_s[...] = carry
    cnt_ref[...] = carry
    info = jnp.zeros((tm, LANES), F32)
    for col, val in ((ROUTE_E0, e0), (ROUTE_E1, e1), (ROUTE_G0, g0), (ROUTE_G1, g1), (ROUTE_R0, r0), (ROUTE_R1, r1)):
        info = jnp.where(lane == col, val, info)
    info_ref[0] = info


def _router(x, scale, shift, w_router):
    b, t, d = x.shape
    tm = min(t, 512)
    wr = jnp.zeros((d, LANES), F32).at[:, :N_EXPERTS].set(w_router.astype(F32))
    return pl.pallas_call(
        _router_kernel,
        grid=(b, t // tm),
        in_specs=[
            pl.BlockSpec((1, tm, d), lambda bi, i: (bi, i, 0)),
            pl.BlockSpec((1, 1, d), lambda bi, i: (bi, 0, 0)),
            pl.BlockSpec((1, 1, d), lambda bi, i: (bi, 0, 0)),
            pl.BlockSpec((d, LANES), lambda bi, i: (0, 0)),
        ],
        out_specs=(
            pl.BlockSpec((1, tm, d), lambda bi, i: (bi, i, 0)),
            pl.BlockSpec((1, tm, LANES), lambda bi, i: (bi, i, 0)),
            pl.BlockSpec((1, LANES), lambda bi, i: (0, 0)),
        ),
        out_shape=(
            jax.ShapeDtypeStruct((b, t, d), F32),
            jax.ShapeDtypeStruct((b, t, LANES), F32),
            jax.ShapeDtypeStruct((1, LANES), F32),
        ),
        scratch_shapes=[pltpu.VMEM((1, LANES), F32)],
        compiler_params=_cparams(("arbitrary", "arbitrary")),
        name="moe_router",
    )(x, scale, shift, wr)


DISPATCH_ROWS = 512


def _dispatch_kernel(pos0_ref, pos1_ref, h_ref, zero_ref, hs_ref, sem):
    del zero_ref
    base = pl.program_id(0) * DISPATCH_ROWS

    def row_copy(r, pos_ref):
        return pltpu.make_async_copy(h_ref.at[pl.ds(r, 1), :], hs_ref.at[pl.ds(pos_ref[base + r], 1), :], sem)

    def start(r, carry):
        row_copy(r, pos0_ref).start(priority=0)
        row_copy(r, pos1_ref).start(priority=1)
        return carry

    def wait(r, carry):
        row_copy(r, pos0_ref).wait()
        row_copy(r, pos1_ref).wait()
        return carry

    lax.fori_loop(0, DISPATCH_ROWS, start, 0, unroll=8)
    lax.fori_loop(0, DISPATCH_ROWS, wait, 0, unroll=8)


def _dispatch(h2, pos0, pos1, p_pad):
    n, d = h2.shape
    grid_spec = pltpu.PrefetchScalarGridSpec(
        num_scalar_prefetch=2,
        grid=(n // DISPATCH_ROWS,),
        in_specs=[pl.BlockSpec((DISPATCH_ROWS, d), lambda i, p0, p1: (i, 0)), pl.BlockSpec(memory_space=pl.ANY)],
        out_specs=pl.BlockSpec(memory_space=pl.ANY),
        scratch_shapes=[pltpu.SemaphoreType.DMA],
    )
    return pl.pallas_call(
        _dispatch_kernel,
        grid_spec=grid_spec,
        out_shape=jax.ShapeDtypeStruct((p_pad, d), F32),
        input_output_aliases={3: 0},
        compiler_params=_cparams(("arbitrary",)),
        name="moe_dispatch",
    )(pos0, pos1, h2, jnp.zeros((p_pad, d), F32))


def _expert_ffn_kernel(te_ref, nr_ref, nu_ref, hs_ref, w1g_ref, w1u_ref, w2_ref, ys_ref, h_s, acc_s):
    t = pl.program_id(0)
    f = pl.program_id(1)
    tm = hs_ref.shape[0]
    used = t < nu_ref[0]
    full = nr_ref[t] > tm // 2

    def run(rows):
        @pl.when(f == 0)
        def _():
            h_s[0:rows, :] = hs_ref[0:rows, :].astype(BF16)
            acc_s[0:rows, :] = jnp.zeros((rows, acc_s.shape[1]), F32)

        h = h_s[0:rows, :]
        a = _silu(_dot(h, w1g_ref[0].astype(BF16))) * _dot(h, w1u_ref[0].astype(BF16))
        acc_s[0:rows, :] += _dot(a.astype(BF16), w2_ref[0].astype(BF16))

        @pl.when(f == pl.num_programs(1) - 1)
        def _():
            ys_ref[0:rows, :] = acc_s[0:rows, :]
            if rows < tm:
                ys_ref[rows:, :] = jnp.zeros((tm - rows, ys_ref.shape[1]), F32)

    pl.when(used & full)(lambda: run(tm))
    pl.when(used & jnp.logical_not(full))(lambda: run(tm // 2))

    @pl.when(jnp.logical_not(used) & (f == 0))
    def _():
        ys_ref[...] = jnp.zeros_like(ys_ref)


def _expert_ffn(hs, tile_e, tile_rows, n_used, w1, w2):
    p_pad, d = hs.shape
    ff = w2.shape[1]
    tm, tf = MOE_TM, FFN_TF
    nf = ff // tf
    nt = p_pad // tm

    def tile(t, nu):
        return jnp.maximum(jnp.minimum(t, nu[0] - 1), 0)

    def ftile(t, f, nu):
        return jnp.where(t < nu[0], f, nf - 1)

    grid_spec = pltpu.PrefetchScalarGridSpec(
        num_scalar_prefetch=3,
        grid=(nt, nf),
        in_specs=[
            pl.BlockSpec((tm, d), lambda t, f, te, nr, nu: (tile(t, nu), 0)),
            pl.BlockSpec((1, d, tf), lambda t, f, te, nr, nu: (te[tile(t, nu)], 0, ftile(t, f, nu))),
            pl.BlockSpec((1, d, tf), lambda t, f, te, nr, nu: (te[tile(t, nu)], 0, nf + ftile(t, f, nu))),
            pl.BlockSpec((1, tf, d), lambda t, f, te, nr, nu: (te[tile(t, nu)], ftile(t, f, nu), 0)),
        ],
        out_specs=pl.BlockSpec((tm, d), lambda t, f, te, nr, nu: (t, 0)),
        scratch_shapes=[pltpu.VMEM((tm, d), BF16), pltpu.VMEM((tm, d), F32)],
    )
    return pl.pallas_call(
        _expert_ffn_kernel,
        grid_spec=grid_spec,
        out_shape=jax.ShapeDtypeStruct((p_pad, d), F32),
        compiler_params=_cparams(("arbitrary", "arbitrary")),
        name="moe_expert_ffn",
    )(tile_e, tile_rows, n_used, hs, w1, w1, w2)


def _combine_kernel(pos0_ref, pos1_ref, ys_ref, x_ref, info_ref, gate_ref, g_ref, b_ref, o_ref, y0_s, y1_s, sem):
    tm = y0_s.shape[0]
    base = (pl.program_id(0) * pl.num_programs(1) + pl.program_id(1)) * tm

    def row_copy(r, pos_ref, dst):
        return pltpu.make_async_copy(ys_ref.at[pl.ds(pos_ref[base + r], 1), :], dst.at[pl.ds(r, 1), :], sem)

    def start(r, carry):
        row_copy(r, pos0_ref, y0_s).start(priority=0)
        row_copy(r, pos1_ref, y1_s).start(priority=1)
        return carry

    def wait(r, carry):
        row_copy(r, pos0_ref, y0_s).wait()
        row_copy(r, pos1_ref, y1_s).wait()
        return carry

    lax.fori_loop(0, tm, start, 0, unroll=8)
    lax.fori_loop(0, tm, wait, 0, unroll=8)
    info = info_ref[0]
    g0 = info[:, ROUTE_G0:ROUTE_G0 + 1]
    g1 = info[:, ROUTE_G1:ROUTE_G1 + 1]
    f = g0 * y0_s[...] + g1 * y1_s[...]
    z = DN_ALPHA * x_ref[0] + gate_ref[0] * f
    o_ref[0] = _layernorm_rows(z, g_ref[...], b_ref[...])


def _combine_ln(ys, pos0, pos1, x, info, gate, ln_g, ln_b):
    b, t, d = x.shape
    tm = 256
    grid_spec = pltpu.PrefetchScalarGridSpec(
        num_scalar_prefetch=2,
        grid=(b, t // tm),
        in_specs=[
            pl.BlockSpec(memory_space=pl.ANY),
            pl.BlockSpec((1, tm, d), lambda bi, i, p0, p1: (bi, i, 0)),
            pl.BlockSpec((1, tm, LANES), lambda bi, i, p0, p1: (bi, i, 0)),
            pl.BlockSpec((1, 1, d), lambda bi, i, p0, p1: (bi, 0, 0)),
            pl.BlockSpec((1, d), lambda bi, i, p0, p1: (0, 0)),
            pl.BlockSpec((1, d), lambda bi, i, p0, p1: (0, 0)),
        ],
        out_specs=pl.BlockSpec((1, tm, d), lambda bi, i, p0, p1: (bi, i, 0)),
        scratch_shapes=[pltpu.VMEM((tm, d), F32), pltpu.VMEM((tm, d), F32), pltpu.SemaphoreType.DMA],
    )
    return pl.pallas_call(
        _combine_kernel,
        grid_spec=grid_spec,
        out_shape=jax.ShapeDtypeStruct((b, t, d), F32),
        compiler_params=_cparams(("arbitrary", "arbitrary")),
        name="moe_combine_ln",
    )(pos0, pos1, ys, x, info, gate, ln_g.reshape(1, d), ln_b.reshape(1, d))


def _moe_ln(x, scale, shift, w_router, w1, w2, gate, ln_g, ln_b):
    b, t, d = x.shape
    n = b * t
    tm = MOE_TM
    h, info, counts = _router(x, scale, shift, w_router)
    cnt = counts[0, :N_EXPERTS].astype(jnp.int32)
    padded = ((cnt + tm - 1) // tm) * tm
    ends = jnp.cumsum(padded)
    starts = ends - padded
    info2 = info.reshape(n, LANES)
    e0 = info2[:, ROUTE_E0].astype(jnp.int32)
    e1 = info2[:, ROUTE_E1].astype(jnp.int32)
    pos0 = starts[e0] + info2[:, ROUTE_R0].astype(jnp.int32)
    pos1 = starts[e1] + info2[:, ROUTE_R1].astype(jnp.int32)
    n_tiles = (TOP_K * n) // tm + N_EXPERTS
    tile_start = jnp.arange(n_tiles, dtype=jnp.int32) * tm
    tile_e = jnp.minimum(jnp.sum(tile_start[:, None] >= ends[None, :], axis=1), N_EXPERTS - 1).astype(jnp.int32)
    tile_rows = jnp.clip(starts[tile_e] + cnt[tile_e] - tile_start, 0, tm).astype(jnp.int32)
    n_used = (ends[-1] // tm).astype(jnp.int32).reshape(1)
    hs = _dispatch(h.reshape(n, d), pos0, pos1, n_tiles * tm)
    ys = _expert_ffn(hs, tile_e, tile_rows, n_used, w1, w2)
    return _combine_ln(ys, pos0, pos1, x, info, gate, ln_g, ln_b)


def _pair_perm(comp_offsets):
    even = np.concatenate([off + np.arange(0, ROT_DIM, 2) for off in comp_offsets])
    odd = np.concatenate([off + np.arange(1, ROT_DIM, 2) for off in comp_offsets])
    return np.concatenate([even, odd])


def _even_w_in(w_in):
    d = w_in.shape[0]
    a_qk = DA_HEADS * 2 * DA_DH
    head_perm = _pair_perm((0, DA_DH))
    qk_perm = np.concatenate([hh * 2 * DA_DH + head_perm for hh in range(DA_HEADS)])
    cols = np.concatenate([qk_perm, a_qk + qk_perm, np.arange(2 * a_qk, w_in.shape[1])])
    w = w_in[:, cols]
    return jnp.pad(w, ((0, 0), (0, EV_NPAD - w.shape[1]))).astype(BF16)


def _odd_w_in(w_in):
    o_ckv = MLA_Q_RANK
    o_kr = o_ckv + MLA_KV_RANK
    o_rest = o_kr + MLA_ROPE
    ev = o_kr + np.arange(0, MLA_ROPE, 2)
    od = o_kr + np.arange(1, MLA_ROPE, 2)
    cols = np.concatenate([np.arange(0, o_kr), ev, ev, od, od, np.arange(o_rest, w_in.shape[1])])
    return w_in[:, cols].astype(BF16)


def _mla_weights(w_uq, w_ukv):
    hq = MLA_NOPE + MLA_ROPE
    q3 = w_uq.reshape(MLA_Q_RANK, MLA_HEADS, hq).transpose(1, 0, 2)
    zeros = jnp.zeros((MLA_HEADS, MLA_Q_RANK, MLA_ROPE // 2), w_uq.dtype)
    rope = q3[:, :, MLA_NOPE:]
    wq = jnp.concatenate([q3[:, :, :MLA_NOPE], rope[:, :, 0::2], zeros, rope[:, :, 1::2], zeros], axis=-1)
    wkv = w_ukv.reshape(MLA_KV_RANK, MLA_HEADS, MLA_NOPE + MLA_DV).transpose(1, 0, 2)
    return wq.astype(BF16), wkv.astype(BF16)


def _gla_gate_weights(gk_w2, gk_b):
    pairs = GLA_HEADS // 2
    w = jnp.zeros((pairs, 2, LANES, LANES), F32)
    for d in range(2):
        blk = gk_w2[d].reshape(GLA_LR, pairs, LANES).transpose(1, 0, 2)
        w = w.at[:, d, d * GLA_LR:(d + 1) * GLA_LR, :].set(blk)
    bias = gk_b.reshape(2, pairs, 1, LANES).transpose(1, 0, 2, 3).astype(F32)
    return w.astype(BF16), bias


def _rope_tables(rows):
    n_freq = ROT_DIM // 4
    inv = ROPE_BASE ** (-jnp.arange(n_freq, dtype=F32) / n_freq)
    row = jnp.repeat(jnp.arange(rows, dtype=F32), GRID_W)
    col = jnp.tile(jnp.arange(GRID_W, dtype=F32), rows)
    ang = jnp.concatenate([row[:, None] * inv, col[:, None] * inv], axis=-1)
    cos, sin = jnp.cos(ang), jnp.sin(ang)
    return jnp.concatenate([cos] * 4, axis=-1), jnp.concatenate([-sin, -sin, sin, sin], axis=-1)


def _diff_lambda_init(layer):
    return 0.8 - 0.6 * math.exp(-0.3 * layer)


def kernel(x, c, ctx, c_ctx, ada_w, ada_b, post_ln_g, post_ln_b, lb_table, ev_w_in, ev_lam, ev_subln_g, ev_gk_w2,
           ev_gk_b, ev_gla_norm_g, ev_w_out, ev_ffn_w1, ev_ffn_w2, od_w_in, od_q_norm_g, od_kv_norm_g, od_w_uq,
           od_w_ukv, od_hg_norm_g, od_w_out, od_router, od_exp_w1, od_exp_w2):
    b, t, d = x.shape
    tc = ctx.shape[1]
    rope_c, rope_s = _rope_tables(t // GRID_W)
    lb_soft = jax.nn.softmax(lb_table.astype(F32), axis=0)
    lower_bounds = jnp.cumsum(lb_soft, axis=0) - lb_soft[0]

    n_cond = ((b + 1 + 7) // 8) * 8
    cond = jnp.zeros((n_cond, d), F32).at[:b].set(c).at[b].set(c_ctx)
    mods = _ada(cond, ada_w, ada_b).reshape(DEPTH, n_cond, 6, d)

    ctx_flat = None
    for layer in range(DEPTH):
        last = layer == DEPTH - 1
        j = layer // 2
        m_l = [mods[layer, :b, i][:, None, :] for i in range(6)]
        m_c = [jnp.broadcast_to(mods[layer, b, i][None, None, :], (b, 1, d)) for i in range(6)]
        m_c1 = [m[:1] for m in m_c]
        even = layer % 2 == 0
        w_in = _even_w_in(ev_w_in[j]) if even else _odd_w_in(od_w_in[j])
        p_l = _proj(x, 1.0 + m_l[1], m_l[0], w_in)
        p_c = _proj(ctx, 1.0 + m_c[1], m_c[0], w_in)
        need_ctx = not last
        if even:
            lam_init = _diff_lambda_init(layer)
            lv = ev_lam[j].astype(F32)
            lam = jnp.exp(jnp.sum(lv[0] * lv[1])) - jnp.exp(jnp.sum(lv[2] * lv[3])) + lam_init
            oa_c, oa_l = _diff_attention(p_l, p_c, rope_c, rope_s, lam, ev_subln_g[j], lam_init, need_ctx)
            wgk, bgk = _gla_gate_weights(ev_gk_w2[j], ev_gk_b[j])
            ob_c, ob_l = _gla_scan(p_c, p_l, wgk, bgk, ev_gla_norm_g[j], need_ctx)
            w_out = ev_w_out[j].astype(BF16)
        else:
            wq, wkv = _mla_weights(od_w_uq[j], od_w_ukv[j])
            oa_l = _mla_attention(p_l, p_c, rope_c, rope_s, od_q_norm_g[j], od_kv_norm_g[j], wq, wkv)
            oa_c = _mla_attention_ctx(p_c, od_q_norm_g[j], od_kv_norm_g[j], wq, wkv) if need_ctx else None
            ob_c, ob_l = _hgrn_scan(p_c, p_l, lower_bounds[layer], od_hg_norm_g[j], need_ctx)
            w_out = od_w_out[j].astype(BF16)
        g0, b0 = post_ln_g[layer, 0], post_ln_b[layer, 0]
        g1, b1 = post_ln_g[layer, 1], post_ln_b[layer, 1]
        x = _outproj_ln(oa_l, ob_l, w_out, x, m_l[2], g0, b0)
        if even:
            w1, w2 = ev_ffn_w1[j], ev_ffn_w2[j]
            x = _ffn_ln(x, 1.0 + m_l[4], m_l[3], w1, w2, m_l[5], g1, b1)
        else:
            w1, w2 = od_exp_w1[j], od_exp_w2[j]
            x = _moe_ln(x, 1.0 + m_l[4], m_l[3], od_router[j], w1, w2, m_l[5], g1, b1)
        if need_ctx:
            ctx = _outproj_ln(oa_c, ob_c, w_out, ctx, m_c[2], g0, b0)
            ctx_flat = ctx.reshape(1, b * tc, d)
            if even:
                ctx_flat = _ffn_ln(ctx_flat, 1.0 + m_c1[4], m_c1[3], w1, w2, m_c1[5], g1, b1)
            else:
                ctx_flat = _moe_ln(ctx_flat, 1.0 + m_c1[4], m_c1[3], od_router[j], w1, w2, m_c1[5], g1, b1)
            ctx = ctx_flat.reshape(b, tc, d)
    return x
```

```python
import functools
import math

import jax
import jax.numpy as jnp
import numpy as np
from jax import lax
from jax.experimental import pallas as pl
from jax.experimental.pallas import tpu as pltpu

F32 = jnp.float32
BF16 = jnp.bfloat16

DEPTH = 2
GRID_W = 64
ROT_DIM = 64
ROPE_BASE = 10000.0
DA_HEADS = 4
DA_DH = ROT_DIM
DA_DV = 2 * DA_DH
GLA_HEADS = 4
GLA_DK = 64
GLA_DV = 128
GLA_LR = 16
GLA_NORMALIZER = 16.0
MLA_HEADS = 4
MLA_Q_RANK = 256
MLA_KV_RANK = 128
MLA_NOPE = 128
MLA_ROPE = ROT_DIM
MLA_DV = 128
MLA_SCALE = (MLA_NOPE + MLA_ROPE) ** -0.5
HG_HEADS = 4
HG_DK = 128
HG_DV = 128
D_FF = 3584
N_EXPERTS = 8
TOP_K = 2
LN_EPS = 1e-5
RMS_EPS = 1e-6
DN_ALPHA = (2 * DEPTH) ** 0.25

LANES = 128
VMEM_LIMIT = 56 * 1024 * 1024
ATTN_TQ = 256
SCAN_BLOCK = 256
SCAN_CHUNK = 32
MOE_TM = 1024
FFN_TF = 512

EV_QA, EV_KA, EV_VA, EV_QB, EV_KB, EV_VB, EV_GB, EV_LR = 0, 4, 8, 12, 14, 16, 20, 24
EV_NPAD = 25 * LANES
OD_CQ, OD_CKV, OD_KR, OD_HQ, OD_FF, OD_FB, OD_HI, OD_HG = 0, 2, 3, 4, 8, 12, 16, 20
OD_NPAD = 24 * LANES


def _cparams(sem, flags=None):
    return pltpu.CompilerParams(dimension_semantics=sem, vmem_limit_bytes=VMEM_LIMIT, flags=flags)


def _silu(v):
    return v * jax.nn.sigmoid(v)


def _layernorm_rows(z, g, b):
    mu = jnp.mean(z, axis=-1, keepdims=True)
    zc = z - mu
    var = jnp.mean(zc * zc, axis=-1, keepdims=True)
    return zc * lax.rsqrt(var + LN_EPS) * g + b


def _rmsnorm_rows(v, g):
    return v * lax.rsqrt(jnp.mean(v * v, axis=-1, keepdims=True) + RMS_EPS) * g


def _dot_nt(a, b):
    return lax.dot_general(a, b, (((1,), (1,)), ((), ())), preferred_element_type=F32)


def _dot_tn(a, b):
    return lax.dot_general(a, b, (((0,), (0,)), ((), ())), preferred_element_type=F32)


def _dot(a, b):
    return jnp.dot(a, b, preferred_element_type=F32)


def _ada_kernel(c_ref, w_ref, b_ref, o_ref):
    s = _silu(c_ref[...]).astype(BF16)
    o_ref[0] = _dot(s, w_ref[0].astype(BF16)) + b_ref[0]


def _ada(cond, ada_w, ada_b):
    depth, d, n = ada_w.shape
    r = cond.shape[0]
    tn = n // 4
    return pl.pallas_call(
        _ada_kernel,
        grid=(depth, n // tn),
        in_specs=[
            pl.BlockSpec((r, d), lambda l, j: (0, 0)),
            pl.BlockSpec((1, d, tn), lambda l, j: (l, 0, j)),
            pl.BlockSpec((1, 1, tn), lambda l, j: (l, 0, j)),
        ],
        out_specs=pl.BlockSpec((1, r, tn), lambda l, j: (l, 0, j)),
        out_shape=jax.ShapeDtypeStruct((depth, r, n), F32),
        compiler_params=_cparams(("parallel", "parallel")),
        name="ada_modulation",
    )(cond, ada_w, ada_b.reshape(depth, 1, n))


def _proj_kernel(x_ref, sc_ref, sh_ref, w_ref, o_ref):
    h = (x_ref[0] * sc_ref[0] + sh_ref[0]).astype(BF16)
    o_ref[0] = _dot(h, w_ref[...]).astype(o_ref.dtype)


def _proj(x, scale, shift, w):
    b, t, d = x.shape
    n = w.shape[1]
    tm = min(t, 512)
    return pl.pallas_call(
        _proj_kernel,
        grid=(b, t // tm),
        in_specs=[
            pl.BlockSpec((1, tm, d), lambda bi, i: (bi, i, 0)),
            pl.BlockSpec((1, 1, d), lambda bi, i: (bi, 0, 0)),
            pl.BlockSpec((1, 1, d), lambda bi, i: (bi, 0, 0)),
            pl.BlockSpec((d, n), lambda bi, i: (0, 0)),
        ],
        out_specs=pl.BlockSpec((1, tm, n), lambda bi, i: (bi, i, 0)),
        out_shape=jax.ShapeDtypeStruct((b, t, n), BF16),
        compiler_params=_cparams(("parallel", "parallel")),
        name="mod_proj",
    )(x, scale, shift, w)


def _rope128(t, cs, sn):
    t = t.astype(F32)
    return t * cs + pltpu.roll(t, LANES // 2, axis=1) * sn


LOG2E = math.log2(math.e)


def _softmax_pv(s2, v_bf):
    m = jnp.max(s2, axis=-1, keepdims=True)
    e = jnp.exp2(s2 - m)
    return _dot(e.astype(BF16), v_bf) / jnp.sum(e, axis=-1, keepdims=True)


def _q1_lane_mask(shape):
    lane = lax.broadcasted_iota(jnp.int32, shape, 1)
    return (lane // 32) % 2 == 0


def _diff_scores_out(q, k_bf, v_bf, lam):
    m1 = _q1_lane_mask(q.shape)
    q1 = jnp.where(m1, q, 0.0).astype(BF16)
    q2 = jnp.where(m1, 0.0, q).astype(BF16)
    s1 = _dot_nt(q1, k_bf)
    s2 = _dot_nt(q2, k_bf)
    e1 = jnp.exp2(s1 - jnp.max(s1, axis=-1, keepdims=True))
    e2 = jnp.exp2(s2 - jnp.max(s2, axis=-1, keepdims=True))
    r1 = 1.0 / jnp.sum(e1, axis=-1, keepdims=True)
    r2 = lam / jnp.sum(e2, axis=-1, keepdims=True)
    return _dot((e1 * r1 - e2 * r2).astype(BF16), v_bf)


def _diffattn_lat_kernel(lam_init, tc, q_ref, kl_ref, vl_ref, kc_ref, vc_ref, cq_ref, sq_ref, ck_ref, sk_ref,
                         lam_ref, g_ref, o_ref, k_s, v_s):
    @pl.when(pl.program_id(2) == 0)
    def _():
        k_s[0:tc, :] = kc_ref[0].astype(BF16)
        v_s[0:tc, :] = vc_ref[0].astype(BF16)
        k_s[tc:, :] = _rope128(kl_ref[0], ck_ref[...], sk_ref[...]).astype(BF16)
        v_s[tc:, :] = vl_ref[0].astype(BF16)

    q = _rope128(q_ref[0], cq_ref[...], sq_ref[...]) * (DA_DH ** -0.5 * LOG2E)
    o = _diff_scores_out(q, k_s[...], v_s[...], lam_ref[0, 0])
    o_ref[0] = (_rmsnorm_rows(o, g_ref[...]) * (1.0 - lam_init)).astype(o_ref.dtype)


def _diffattn_ctx_kernel(lam_init, q_ref, k_ref, v_ref, lam_ref, g_ref, o_ref):
    q = q_ref[0].astype(F32) * (DA_DH ** -0.5 * LOG2E)
    o = _diff_scores_out(q, k_ref[0].astype(BF16), v_ref[0].astype(BF16), lam_ref[0, 0])
    o_ref[0] = (_rmsnorm_rows(o, g_ref[...]) * (1.0 - lam_init)).astype(o_ref.dtype)


def _diff_attention(p_l, p_c, rope_c, rope_s, lam, subln_g, lam_init, need_ctx):
    b, tl, _ = p_l.shape
    tc = p_c.shape[1]
    h = DA_HEADS
    tq = min(tl, ATTN_TQ)
    lam2 = lam.reshape(1, 1).astype(F32)
    g2 = subln_g.reshape(1, DA_DV).astype(F32)
    smem = pl.BlockSpec(memory_space=pltpu.SMEM)
    o_l = pl.pallas_call(
        functools.partial(_diffattn_lat_kernel, lam_init, tc),
        grid=(b, h, tl // tq),
        in_specs=[
            pl.BlockSpec((1, tq, LANES), lambda bi, hi, i: (bi, i, EV_QA + hi)),
            pl.BlockSpec((1, tl, LANES), lambda bi, hi, i: (bi, 0, EV_KA + hi)),
            pl.BlockSpec((1, tl, LANES), lambda bi, hi, i: (bi, 0, EV_VA + hi)),
            pl.BlockSpec((1, tc, LANES), lambda bi, hi, i: (bi, 0, EV_KA + hi)),
            pl.BlockSpec((1, tc, LANES), lambda bi, hi, i: (bi, 0, EV_VA + hi)),
            pl.BlockSpec((tq, LANES), lambda bi, hi, i: (i, 0)),
            pl.BlockSpec((tq, LANES), lambda bi, hi, i: (i, 0)),
            pl.BlockSpec((tl, LANES), lambda bi, hi, i: (0, 0)),
            pl.BlockSpec((tl, LANES), lambda bi, hi, i: (0, 0)),
            smem,
            pl.BlockSpec((1, LANES), lambda bi, hi, i: (0, 0)),
        ],
        out_specs=pl.BlockSpec((1, tq, LANES), lambda bi, hi, i: (bi, i, hi)),
        out_shape=jax.ShapeDtypeStruct((b, tl, h * DA_DV), BF16),
        scratch_shapes=[pltpu.VMEM((tc + tl, LANES), BF16), pltpu.VMEM((tc + tl, LANES), BF16)],
        compiler_params=_cparams(("parallel", "parallel", "arbitrary")),
        name="diff_attention_latent",
    )(p_l, p_l, p_l, p_c, p_c, rope_c, rope_s, rope_c, rope_s, lam2, g2)
    if not need_ctx:
        return None, o_l
    o_c = pl.pallas_call(
        functools.partial(_diffattn_ctx_kernel, lam_init),
        grid=(b, h),
        in_specs=[
            pl.BlockSpec((1, tc, LANES), lambda bi, hi: (bi, 0, EV_QA + hi)),
            pl.BlockSpec((1, tc, LANES), lambda bi, hi: (bi, 0, EV_KA + hi)),
            pl.BlockSpec((1, tc, LANES), lambda bi, hi: (bi, 0, EV_VA + hi)),
            smem,
            pl.BlockSpec((1, LANES), lambda bi, hi: (0, 0)),
        ],
        out_specs=pl.BlockSpec((1, tc, LANES), lambda bi, hi: (bi, 0, hi)),
        out_shape=jax.ShapeDtypeStruct((b, tc, h * DA_DV), BF16),
        compiler_params=_cparams(("parallel", "parallel")),
        name="diff_attention_ctx",
    )(p_c, p_c, p_c, lam2, g2)
    return o_c, o_l


def _mla_q(cq, qg, wq, cs, sn):
    q = _dot(_rmsnorm_rows(cq.astype(F32), qg).astype(BF16), wq)
    if cs is not None:
        q = jnp.concatenate([q[:, :LANES], _rope128(q[:, LANES:], cs, sn)], axis=1)
    return (q * (MLA_SCALE * LOG2E)).astype(BF16)


def _mla_kv(ckv, kr, kvg, wkv, cs, sn):
    kv = _dot(_rmsnorm_rows(ckv.astype(F32), kvg).astype(BF16), wkv)
    if cs is not None:
        kr = _rope128(kr, cs, sn)
    k = jnp.concatenate([kv[:, :LANES].astype(BF16), kr.astype(BF16)], axis=1)
    return k, kv[:, LANES:].astype(BF16)


def _mla_lat_kernel(tc, cq_ref, ckvl_ref, krl_ref, ckvc_ref, krc_ref, cq_c_ref, cq_s_ref, ck_ref, sk_ref,
                    qg_ref, kvg_ref, wq_ref, wkv_ref, o_ref, k_s, v_s):
    @pl.when(pl.program_id(2) == 0)
    def _():
        kc, vc = _mla_kv(ckvc_ref[0], krc_ref[0], kvg_ref[...], wkv_ref[0], None, None)
        k_s[0:tc, :] = kc
        v_s[0:tc, :] = vc
        kl, vl = _mla_kv(ckvl_ref[0], krl_ref[0], kvg_ref[...], wkv_ref[0], ck_ref[...], sk_ref[...])
        k_s[tc:, :] = kl
        v_s[tc:, :] = vl

    q = _mla_q(cq_ref[0], qg_ref[...], wq_ref[0], cq_c_ref[...], cq_s_ref[...])
    o_ref[0] = _softmax_pv(_dot_nt(q, k_s[...]), v_s[...]).astype(o_ref.dtype)


def _mla_attention(p_l, p_c, rope_c, rope_s, q_norm_g, kv_norm_g, wq, wkv):
    b, tl, _ = p_l.shape
    tc = p_c.shape[1]
    h = MLA_HEADS
    tq = min(tl, ATTN_TQ)
    return pl.pallas_call(
        functools.partial(_mla_lat_kernel, tc),
        grid=(b, h, tl // tq),
        in_specs=[
            pl.BlockSpec((1, tq, MLA_Q_RANK), lambda bi, hi, i: (bi, i, OD_CQ)),
            pl.BlockSpec((1, tl, LANES), lambda bi, hi, i: (bi, 0, OD_CKV)),
            pl.BlockSpec((1, tl, LANES), lambda bi, hi, i: (bi, 0, OD_KR)),
            pl.BlockSpec((1, tc, LANES), lambda bi, hi, i: (bi, 0, OD_CKV)),
            pl.BlockSpec((1, tc, LANES), lambda bi, hi, i: (bi, 0, OD_KR)),
            pl.BlockSpec((tq, LANES), lambda bi, hi, i: (i, 0)),
            pl.BlockSpec((tq, LANES), lambda bi, hi, i: (i, 0)),
            pl.BlockSpec((tl, LANES), lambda bi, hi, i: (0, 0)),
            pl.BlockSpec((tl, LANES), lambda bi, hi, i: (0, 0)),
            pl.BlockSpec((1, MLA_Q_RANK), lambda bi, hi, i: (0, 0)),
            pl.BlockSpec((1, MLA_KV_RANK), lambda bi, hi, i: (0, 0)),
            pl.BlockSpec((1, MLA_Q_RANK, 2 * LANES), lambda bi, hi, i: (hi, 0, 0)),
            pl.BlockSpec((1, MLA_KV_RANK, 2 * LANES), lambda bi, hi, i: (hi, 0, 0)),
        ],
        out_specs=pl.BlockSpec((1, tq, LANES), lambda bi, hi, i: (bi, i, hi)),
        out_shape=jax.ShapeDtypeStruct((b, tl, h * MLA_DV), BF16),
        scratch_shapes=[pltpu.VMEM((tc + tl, 2 * LANES), BF16), pltpu.VMEM((tc + tl, LANES), BF16)],
        compiler_params=_cparams(("parallel", "parallel", "arbitrary")),
        name="mla_attention_latent",
    )(p_l, p_l, p_l, p_c, p_c, rope_c, rope_s, rope_c, rope_s,
      q_norm_g.reshape(1, -1), kv_norm_g.reshape(1, -1), wq, wkv)


def _mla_ctx_kernel(cq_ref, ckv_ref, kr_ref, qg_ref, kvg_ref, wq_ref, wkv_ref, o_ref):
    k, v = _mla_kv(ckv_ref[0], kr_ref[0], kvg_ref[...], wkv_ref[0], None, None)
    q = _mla_q(cq_ref[0], qg_ref[...], wq_ref[0], None, None)
    o_ref[0] = _softmax_pv(_dot_nt(q, k), v).astype(o_ref.dtype)


def _mla_attention_ctx(p_c, q_norm_g, kv_norm_g, wq, wkv):
    b, tc, _ = p_c.shape
    h = MLA_HEADS
    return pl.pallas_call(
        _mla_ctx_kernel,
        grid=(b, h),
        in_specs=[
            pl.BlockSpec((1, tc, MLA_Q_RANK), lambda bi, hi: (bi, 0, OD_CQ)),
            pl.BlockSpec((1, tc, LANES), lambda bi, hi: (bi, 0, OD_CKV)),
            pl.BlockSpec((1, tc, LANES), lambda bi, hi: (bi, 0, OD_KR)),
            pl.BlockSpec((1, MLA_Q_RANK), lambda bi, hi: (0, 0)),
            pl.BlockSpec((1, MLA_KV_RANK), lambda bi, hi: (0, 0)),
            pl.BlockSpec((1, MLA_Q_RANK, 2 * LANES), lambda bi, hi: (hi, 0, 0)),
            pl.BlockSpec((1, MLA_KV_RANK, 2 * LANES), lambda bi, hi: (hi, 0, 0)),
        ],
        out_specs=pl.BlockSpec((1, tc, LANES), lambda bi, hi: (bi, 0, hi)),
        out_shape=jax.ShapeDtypeStruct((b, tc, h * MLA_DV), BF16),
        compiler_params=_cparams(("parallel", "parallel")),
        name="mla_attention_ctx",
    )(p_c, p_c, p_c, q_norm_g.reshape(1, -1), kv_norm_g.reshape(1, -1), wq, wkv)


def _scan_levels(bt):
    levels = []
    c = SCAN_CHUNK
    while c <= bt:
        levels.append(c)
        c *= 2
    return levels


def _level_table(bt, reverse):
    i = np.arange(bt)[:, None]
    j = np.arange(bt)[None, :]
    if reverse:
        i, j = j, i
    tab = np.zeros((bt, bt), np.int32)
    for lvl, c in enumerate(_scan_levels(bt), start=1):
        same = (i // c) == (j // c)
        if lvl == 1:
            m = same & (j <= i)
        else:
            m = same & ((i % c) >= c // 2) & ((j % c) < c // 2)
        tab[m] = lvl
    return tab


def _chunk_row(a, c, r):
    bt, n = a.shape
    a3 = a.reshape(bt // c, c, n)
    return jnp.broadcast_to(a3[:, r:r + 1, :], (bt // c, c, n)).reshape(bt, n)


def _scan_block(q_s, k_s, g_s, v_s, o_s, tri_ref, lvl_ref, start, st, reverse, compute_out):
    bt = SCAN_BLOCK
    rows = pl.ds(start, bt)
    g = g_s[rows, :]
    k = k_s[rows, :]
    v = v_s[rows, :]
    g_hi = g.astype(BF16)
    g_lo = (g - g_hi.astype(F32)).astype(BF16)
    tri = tri_ref[...]
    gc = _dot(tri, g_hi) + _dot(tri, g_lo)
    g_tot = gc[0:1, :] if reverse else gc[bt - 1:bt, :]
    kd = (k * jnp.exp(g_tot - gc)).astype(BF16)
    st_new = st * jnp.exp(g_tot) + _dot_tn(v, kd)
    if compute_out:
        q = q_s[rows, :]
        o = _dot_nt((q * jnp.exp(gc)).astype(BF16), st.astype(BF16))
        lvl = lvl_ref[...]
        att = jnp.zeros((bt, bt), F32)
        for li, c in enumerate(_scan_levels(bt), start=1):
            r = _chunk_row(gc, c, c // 2 if reverse else c // 2 - 1)
            a = _dot_nt((q * jnp.exp(gc - r)).astype(BF16), (k * jnp.exp(r - gc)).astype(BF16))
            att = jnp.where(lvl == li, a, att)
        o_s[rows, :] = o + _dot(att.astype(BF16), v)
    return st_new


def _scan_all(chains, tri_refs, lvl_refs, tc, tl, need_ctx):
    bt = SCAN_BLOCK
    nc, nl = tc // bt, tl // bt

    def run(first, n, carry, compute_out):
        def body(i, carry):
            new = []
            for ci, ch in enumerate(chains):
                for d in range(2):
                    blk = first + (n - 1 - i if d else i)
                    start = pl.multiple_of(blk * bt, bt)
                    new.append(_scan_block(ch["q"], ch["k"][d], ch["g"][d], ch["v"], ch["o"][d], tri_refs[d],
                                           lvl_refs[d], start, carry[2 * ci + d], d == 1, compute_out))
            return tuple(new)
        return lax.fori_loop(0, n, body, carry)

    zero = jnp.zeros((LANES, LANES), F32)
    carry = run(0, nc, (zero,) * (2 * len(chains)), need_ctx)
    run(nc, nl, carry, True)


def _scan_finish(chain, cols, gate_c_ref, gate_l_ref, ng_ref, oc_ref, ol_ref, tc, need_ctx):
    ng = ng_ref[...]
    of_s, ob_s = chain["o"]
    if need_ctx:
        o = of_s[0:tc, :] + ob_s[0:tc, :]
        oc_ref[0, :, cols] = (_rmsnorm_rows(o, ng) * _silu(gate_c_ref[0, :, cols].astype(F32))).astype(oc_ref.dtype)
    o = of_s[tc:, :] + ob_s[tc:, :]
    ol_ref[0, :, cols] = (_rmsnorm_rows(o, ng) * _silu(gate_l_ref[0, :, cols].astype(F32))).astype(ol_ref.dtype)


SCAN_HEADS = 2
SCAN_SCRATCH = 8


def _scan_unpack(refs, n_in, need_ctx):
    ins = refs[:n_in]
    if need_ctx:
        oc_ref, ol_ref = refs[n_in:n_in + 2]
        scratch = refs[n_in + 2:]
    else:
        oc_ref, ol_ref = None, refs[n_in]
        scratch = refs[n_in + 1:]
    chains = []
    for hh in range(SCAN_HEADS):
        q_s, v_s, kf_s, kb_s, gf_s, gb_s, of_s, ob_s = scratch[SCAN_SCRATCH * hh:SCAN_SCRATCH * (hh + 1)]
        chains.append({"q": q_s, "v": v_s, "k": (kf_s, kb_s), "g": (gf_s, gb_s), "o": (of_s, ob_s)})
    return ins, oc_ref, ol_ref, chains


def _gla_scan_kernel(tc, tl, need_ctx, *refs):
    ins, oc_ref, ol_ref, chains = _scan_unpack(refs, 17, need_ctx)
    (qc_ref, kc_ref, vc_ref, gbc_ref, lrc_ref, ql_ref, kl_ref, vl_ref, gbl_ref, lrl_ref,
     wgk_ref, bgk_ref, ng_ref, trif_ref, trib_ref, lvlf_ref, lvlb_ref) = ins
    lane = lax.broadcasted_iota(jnp.int32, (1, LANES), 1)
    for hh, ch in enumerate(chains):
        cols = slice(hh * LANES, (hh + 1) * LANES)
        mine = (lane // GLA_DK) == hh
        q_s, v_s, (k_s, _) = ch["q"], ch["v"], ch["k"]
        ch["k"] = (k_s, k_s)
        for (q_ref, k_ref, v_ref), lo, hi in (((qc_ref, kc_ref, vc_ref), 0, tc),
                                              ((ql_ref, kl_ref, vl_ref), tc, tc + tl)):
            q_s[lo:hi, :] = jnp.where(mine, q_ref[0].astype(F32), 0.0) * (GLA_DK ** -0.5)
            k_s[lo:hi, :] = jnp.where(mine, k_ref[0].astype(F32), 0.0)
            v_s[lo:hi, :] = v_ref[0, :, cols]
    for d in range(2):
        for lr_ref, lo, hi in ((lrc_ref, 0, tc), (lrl_ref, tc, tc + tl)):
            g_pair = jax.nn.log_sigmoid(_dot(lr_ref[0], wgk_ref[0, d]) + bgk_ref[0, d]) / GLA_NORMALIZER
            for hh, ch in enumerate(chains):
                ch["g"][d][lo:hi, :] = jnp.where((lane // GLA_DK) == hh, g_pair, 0.0)
    _scan_all(chains, (trif_ref, trib_ref), (lvlf_ref, lvlb_ref), tc, tl, need_ctx)
    for hh, ch in enumerate(chains):
        _scan_finish(ch, slice(hh * LANES, (hh + 1) * LANES), gbc_ref, gbl_ref, ng_ref, oc_ref, ol_ref, tc, need_ctx)


def _hgrn_scan_kernel(tc, tl, need_ctx, *refs):
    ins, oc_ref, ol_ref, chains = _scan_unpack(refs, 16, need_ctx)
    (qc_ref, ffc_ref, fbc_ref, vc_ref, gtc_ref, ql_ref, ffl_ref, fbl_ref, vl_ref, gtl_ref,
     lb_ref, ng_ref, trif_ref, trib_ref, lvlf_ref, lvlb_ref) = ins
    for hh, ch in enumerate(chains):
        cols = slice(hh * LANES, (hh + 1) * LANES)
        lb = lb_ref[:, cols]
        for (q_ref, v_ref), lo, hi in (((qc_ref, vc_ref), 0, tc), ((ql_ref, vl_ref), tc, tc + tl)):
            ch["q"][lo:hi, :] = q_ref[0, :, cols].astype(F32)
            ch["v"][lo:hi, :] = v_ref[0, :, cols]
        for (fc_ref, fl_ref), k_s, g_s in (((ffc_ref, ffl_ref), ch["k"][0], ch["g"][0]),
                                           ((fbc_ref, fbl_ref), ch["k"][1], ch["g"][1])):
            for f_ref, lo, hi in ((fc_ref, 0, tc), (fl_ref, tc, tc + tl)):
                f = lb + (1.0 - lb) * jax.nn.sigmoid(f_ref[0, :, cols].astype(F32))
                k_s[lo:hi, :] = 1.0 - f
                g_s[lo:hi, :] = jnp.log(f)
    _scan_all(chains, (trif_ref, trib_ref), (lvlf_ref, lvlb_ref), tc, tl, need_ctx)
    for hh, ch in enumerate(chains):
        _scan_finish(ch, slice(hh * LANES, (hh + 1) * LANES), gtc_ref, gtl_ref, ng_ref, oc_ref, ol_ref, tc, need_ctx)


def _scan_consts():
    bt = SCAN_BLOCK
    lower = np.tril(np.ones((bt, bt), np.float32))
    return (jnp.asarray(lower, BF16), jnp.asarray(lower.T, BF16),
            jnp.asarray(_level_table(bt, False)), jnp.asarray(_level_table(bt, True)))


def _scan_call(kernel_fn, name, p_c, p_l, col_specs, extra, extra_specs, need_ctx, heads):
    b, tl, _ = p_l.shape
    tc = p_c.shape[1]
    bt = SCAN_BLOCK
    t = tc + tl
    wide = SCAN_HEADS * LANES
    consts = _scan_consts()
    const_specs = [pl.BlockSpec((bt, bt), lambda bi, ji: (0, 0)) for _ in consts]
    in_specs = ([pl.BlockSpec((1, tc, g * LANES), f) for g, f in col_specs]
                + [pl.BlockSpec((1, tl, g * LANES), f) for g, f in col_specs] + extra_specs + const_specs)
    args = [p_c] * len(col_specs) + [p_l] * len(col_specs) + list(extra) + list(consts)
    out_l = jax.ShapeDtypeStruct((b, tl, heads * LANES), BF16)
    spec_l = pl.BlockSpec((1, tl, wide), lambda bi, ji: (bi, 0, ji))
    if need_ctx:
        out_shape = (jax.ShapeDtypeStruct((b, tc, heads * LANES), BF16), out_l)
        out_specs = (pl.BlockSpec((1, tc, wide), lambda bi, ji: (bi, 0, ji)), spec_l)
    else:
        out_shape, out_specs = out_l, spec_l
    per_head = [pltpu.VMEM((t, LANES), F32), pltpu.VMEM((t, LANES), BF16)] + [pltpu.VMEM((t, LANES), F32)] * 6
    res = pl.pallas_call(
        functools.partial(kernel_fn, tc, tl, need_ctx),
        grid=(b, heads // SCAN_HEADS),
        in_specs=in_specs,
        out_specs=out_specs,
        out_shape=out_shape,
        scratch_shapes=per_head * SCAN_HEADS,
        compiler_params=_cparams(("parallel", "parallel")),
        name=name,
    )(*args)
    return res if need_ctx else (None, res)


def _gla_scan(p_c, p_l, wgk, bgk, norm_g, need_ctx):
    col_specs = [
        (1, lambda bi, ji: (bi, 0, EV_QB + ji)),
        (1, lambda bi, ji: (bi, 0, EV_KB + ji)),
        (SCAN_HEADS, lambda bi, ji: (bi, 0, EV_VB // SCAN_HEADS + ji)),
        (SCAN_HEADS, lambda bi, ji: (bi, 0, EV_GB // SCAN_HEADS + ji)),
        (1, lambda bi, ji: (bi, 0, EV_LR)),
    ]
    extra_specs = [
        pl.BlockSpec((1, 2, LANES, LANES), lambda bi, ji: (ji, 0, 0, 0)),
        pl.BlockSpec((1, 2, 1, LANES), lambda bi, ji: (ji, 0, 0, 0)),
        pl.BlockSpec((1, LANES), lambda bi, ji: (0, 0)),
    ]
    return _scan_call(_gla_scan_kernel, "gla_scan", p_c, p_l, col_specs,
                      (wgk, bgk, norm_g.reshape(1, LANES)), extra_specs, need_ctx, GLA_HEADS)


def _hgrn_scan(p_c, p_l, lb, norm_g, need_ctx):
    col_specs = [(SCAN_HEADS, functools.partial(lambda base, bi, ji: (bi, 0, base // SCAN_HEADS + ji), base))
                 for base in (OD_HQ, OD_FF, OD_FB, OD_HI, OD_HG)]
    extra_specs = [
        pl.BlockSpec((1, SCAN_HEADS * LANES), lambda bi, ji: (0, ji)),
        pl.BlockSpec((1, LANES), lambda bi, ji: (0, 0)),
    ]
    return _scan_call(_hgrn_scan_kernel, "hgrn2_scan", p_c, p_l, col_specs,
                      (lb.reshape(1, -1), norm_g.reshape(1, LANES)), extra_specs, need_ctx, HG_HEADS)


def _outproj_kernel(oa_ref, ob_ref, w_ref, x_ref, gate_ref, g_ref, b_ref, o_ref):
    ka = oa_ref.shape[2]
    y = _dot(oa_ref[0], w_ref[0:ka, :]) + _dot(ob_ref[0], w_ref[ka:, :])
    z = DN_ALPHA * x_ref[0] + gate_ref[0] * y
    o_ref[0] = _layernorm_rows(z, g_ref[...], b_ref[...])


def _outproj_ln(oa, ob, w, x, gate, ln_g, ln_b):
    b, t, d = x.shape
    ka, kb = oa.shape[2], ob.shape[2]
    tm = min(t, 512)
    return pl.pallas_call(
        _outproj_kernel,
        grid=(b, t // tm),
        in_specs=[
            pl.BlockSpec((1, tm, ka), lambda bi, i: (bi, i, 0)),
            pl.BlockSpec((1, tm, kb), lambda bi, i: (bi, i, 0)),
            pl.BlockSpec((ka + kb, d), lambda bi, i: (0, 0)),
            pl.BlockSpec((1, tm, d), lambda bi, i: (bi, i, 0)),
            pl.BlockSpec((1, 1, d), lambda bi, i: (bi, 0, 0)),
            pl.BlockSpec((1, d), lambda bi, i: (0, 0)),
            pl.BlockSpec((1, d), lambda bi, i: (0, 0)),
        ],
        out_specs=pl.BlockSpec((1, tm, d), lambda bi, i: (bi, i, 0)),
        out_shape=jax.ShapeDtypeStruct((b, t, d), F32),
        compiler_params=_cparams(("parallel", "parallel")),
        name="outproj_residual_ln",
    )(oa, ob, w, x, gate, ln_g.reshape(1, d), ln_b.reshape(1, d))


def _ffn_kernel(x_ref, sc_ref, sh_ref, w1g_ref, w1u_ref, w2_ref, gate_ref, g_ref, b_ref, o_ref, h_s, acc_s):
    f = pl.program_id(2)

    @pl.when(f == 0)
    def _():
        h_s[...] = (x_ref[0] * sc_ref[0] + sh_ref[0]).astype(BF16)
        acc_s[...] = jnp.zeros_like(acc_s)

    h = h_s[...]
    a = _silu(_dot(h, w1g_ref[...].astype(BF16))) * _dot(h, w1u_ref[...].astype(BF16))
    acc_s[...] += _dot(a.astype(BF16), w2_ref[...].astype(BF16))

    @pl.when(f == pl.num_programs(2) - 1)
    def _():
        z = DN_ALPHA * x_ref[0] + gate_ref[0] * acc_s[...]
        o_ref[0] = _layernorm_rows(z, g_ref[...], b_ref[...])


def _ffn_ln(x, scale, shift, w1, w2, gate, ln_g, ln_b):
    b, t, d = x.shape
    ff = w2.shape[0]
    tf = FFN_TF
    nf = ff // tf
    tm = min(t, 1024)
    return pl.pallas_call(
        _ffn_kernel,
        grid=(b, t // tm, nf),
        in_specs=[
            pl.BlockSpec((1, tm, d), lambda bi, i, f: (bi, i, 0)),
            pl.BlockSpec((1, 1, d), lambda bi, i, f: (bi, 0, 0)),
            pl.BlockSpec((1, 1, d), lambda bi, i, f: (bi, 0, 0)),
            pl.BlockSpec((d, tf), lambda bi, i, f: (0, f)),
            pl.BlockSpec((d, tf), lambda bi, i, f: (0, nf + f)),
            pl.BlockSpec((tf, d), lambda bi, i, f: (f, 0)),
            pl.BlockSpec((1, 1, d), lambda bi, i, f: (bi, 0, 0)),
            pl.BlockSpec((1, d), lambda bi, i, f: (0, 0)),
            pl.BlockSpec((1, d), lambda bi, i, f: (0, 0)),
        ],
        out_specs=pl.BlockSpec((1, tm, d), lambda bi, i, f: (bi, i, 0)),
        out_shape=jax.ShapeDtypeStruct((b, t, d), F32),
        scratch_shapes=[pltpu.VMEM((tm, d), BF16), pltpu.VMEM((tm, d), F32)],
        compiler_params=_cparams(("parallel", "parallel", "arbitrary")),
        name="swiglu_residual_ln",
    )(x, scale, shift, w1, w1, w2, gate, ln_g.reshape(1, d), ln_b.reshape(1, d))


ROUTE_E0, ROUTE_E1, ROUTE_G0, ROUTE_G1, ROUTE_R0, ROUTE_R1 = range(6)


def _router_kernel(x_ref, sc_ref, sh_ref, wr_ref, h_ref, info_ref, cnt_ref, carry_s):
    first = (pl.program_id(0) == 0) & (pl.program_id(1) == 0)

    @pl.when(first)
    def _():
        carry_s[...] = jnp.zeros_like(carry_s)

    h = x_ref[0] * sc_ref[0] + sh_ref[0]
    h_ref[0] = h
    tm = h.shape[0]
    logits = jnp.dot(h, wr_ref[...], preferred_element_type=F32, precision=lax.Precision.HIGHEST)
    lane = lax.broadcasted_iota(jnp.int32, (tm, LANES), 1).astype(F32)
    neg = jnp.float32(-jnp.inf)
    logits = jnp.where(lane < N_EXPERTS, logits, neg)
    v0 = jnp.max(logits, axis=-1, keepdims=True)
    e0 = jnp.min(jnp.where(logits == v0, lane, float(LANES)), axis=-1, keepdims=True)
    rest = jnp.where(lane == e0, neg, logits)
    v1 = jnp.max(rest, axis=-1, keepdims=True)
    e1 = jnp.min(jnp.where(rest == v1, lane, float(LANES)), axis=-1, keepdims=True)
    d = jnp.exp(v1 - v0)
    g0 = 1.0 / (1.0 + d)
    g1 = d / (1.0 + d)
    oh0 = (lane == e0).astype(BF16)
    oh1 = (lane == e1).astype(BF16)
    ri = lax.broadcasted_iota(jnp.int32, (tm, tm), 0)
    ci = lax.broadcasted_iota(jnp.int32, (tm, tm), 1)
    before = (ci < ri).astype(BF16)
    c0 = _dot(before, oh0)
    c1 = _dot(before, oh1)
    tot0 = jnp.sum(oh0.astype(F32), axis=0, keepdims=True)
    tot1 = jnp.sum(oh1.astype(F32), axis=0, keepdims=True)
    carry = carry_s[...]
    r0 = jnp.sum(jnp.where(lane == e0, carry + c0, 0.0), axis=-1, keepdims=True)
    r1 = jnp.sum(jnp.where(lane == e1, carry + tot0 + c1, 0.0), axis=-1, keepdims=True)
    carry = carry + tot0 + tot1
    carry_s[...] = carry
    cnt_ref[...] = carry
    info = jnp.zeros((tm, LANES), F32)
    for col, val in ((ROUTE_E0, e0), (ROUTE_E1, e1), (ROUTE_G0, g0), (ROUTE_G1, g1), (ROUTE_R0, r0), (ROUTE_R1, r1)):
        info = jnp.where(lane == col, val, info)
    info_ref[0] = info


def _router(x, scale, shift, w_router):
    b, t, d = x.shape
    tm = min(t, 512)
    wr = jnp.zeros((d, LANES), F32).at[:, :N_EXPERTS].set(w_router.astype(F32))
    return pl.pallas_call(
        _router_kernel,
        grid=(b, t // tm),
        in_specs=[
            pl.BlockSpec((1, tm, d), lambda bi, i: (bi, i, 0)),
            pl.BlockSpec((1, 1, d), lambda bi, i: (bi, 0, 0)),
            pl.BlockSpec((1, 1, d), lambda bi, i: (bi, 0, 0)),
            pl.BlockSpec((d, LANES), lambda bi, i: (0, 0)),
        ],
        out_specs=(
            pl.BlockSpec((1, tm, d), lambda bi, i: (bi, i, 0)),
            pl.BlockSpec((1, tm, LANES), lambda bi, i: (bi, i, 0)),
            pl.BlockSpec((1, LANES), lambda bi, i: (0, 0)),
        ),
        out_shape=(
            jax.ShapeDtypeStruct((b, t, d), F32),
            jax.ShapeDtypeStruct((b, t, LANES), F32),
            jax.ShapeDtypeStruct((1, LANES), F32),
        ),
        scratch_shapes=[pltpu.VMEM((1, LANES), F32)],
        compiler_params=_cparams(("arbitrary", "arbitrary")),
        name="moe_router",
    )(x, scale, shift, wr)


DISPATCH_ROWS = 512


def _dispatch_kernel(pos0_ref, pos1_ref, h_ref, zero_ref, hs_ref, sem):
    del zero_ref
    base = pl.program_id(0) * DISPATCH_ROWS

    def row_copy(r, pos_ref):
        return pltpu.make_async_copy(h_ref.at[pl.ds(r, 1), :], hs_ref.at[pl.ds(pos_ref[base + r], 1), :], sem)

    def start(r, carry):
        row_copy(r, pos0_ref).start(priority=0)
        row_copy(r, pos1_ref).start(priority=1)
        return carry

    def wait(r, carry):
        row_copy(r, pos0_ref).wait()
        row_copy(r, pos1_ref).wait()
        return carry

    lax.fori_loop(0, DISPATCH_ROWS, start, 0, unroll=8)
    lax.fori_loop(0, DISPATCH_ROWS, wait, 0, unroll=8)


def _dispatch(h2, pos0, pos1, p_pad):
    n, d = h2.shape
    grid_spec = pltpu.PrefetchScalarGridSpec(
        num_scalar_prefetch=2,
        grid=(n // DISPATCH_ROWS,),
        in_specs=[pl.BlockSpec((DISPATCH_ROWS, d), lambda i, p0, p1: (i, 0)), pl.BlockSpec(memory_space=pl.ANY)],
        out_specs=pl.BlockSpec(memory_space=pl.ANY),
        scratch_shapes=[pltpu.SemaphoreType.DMA],
    )
    return pl.pallas_call(
        _dispatch_kernel,
        grid_spec=grid_spec,
        out_shape=jax.ShapeDtypeStruct((p_pad, d), F32),
        input_output_aliases={3: 0},
        compiler_params=_cparams(("arbitrary",)),
        name="moe_dispatch",
    )(pos0, pos1, h2, jnp.zeros((p_pad, d), F32))


def _expert_ffn_kernel(te_ref, nr_ref, nu_ref, hs_ref, w1g_ref, w1u_ref, w2_ref, ys_ref, h_s, acc_s):
    t = pl.program_id(0)
    f = pl.program_id(1)
    tm = hs_ref.shape[0]
    used = t < nu_ref[0]
    full = nr_ref[t] > tm // 2

    def run(rows):
        @pl.when(f == 0)
        def _():
            h_s[0:rows, :] = hs_ref[0:rows, :].astype(BF16)
            acc_s[0:rows, :] = jnp.zeros((rows, acc_s.shape[1]), F32)

        h = h_s[0:rows, :]
        a = _silu(_dot(h, w1g_ref[0].astype(BF16))) * _dot(h, w1u_ref[0].astype(BF16))
        acc_s[0:rows, :] += _dot(a.astype(BF16), w2_ref[0].astype(BF16))

        @pl.when(f == pl.num_programs(1) - 1)
        def _():
            ys_ref[0:rows, :] = acc_s[0:rows, :]
            if rows < tm:
                ys_ref[rows:, :] = jnp.zeros((tm - rows, ys_ref.shape[1]), F32)

    pl.when(used & full)(lambda: run(tm))
    pl.when(used & jnp.logical_not(full))(lambda: run(tm // 2))

    @pl.when(jnp.logical_not(used) & (f == 0))
    def _():
        ys_ref[...] = jnp.zeros_like(ys_ref)


def _expert_ffn(hs, tile_e, tile_rows, n_used, w1, w2):
    p_pad, d = hs.shape
    ff = w2.shape[1]
    tm, tf = MOE_TM, FFN_TF
    nf = ff // tf
    nt = p_pad // tm

    def tile(t, nu):
        return jnp.maximum(jnp.minimum(t, nu[0] - 1), 0)

    def ftile(t, f, nu):
        return jnp.where(t < nu[0], f, nf - 1)

    grid_spec = pltpu.PrefetchScalarGridSpec(
        num_scalar_prefetch=3,
        grid=(nt, nf),
        in_specs=[
            pl.BlockSpec((tm, d), lambda t, f, te, nr, nu: (tile(t, nu), 0)),
            pl.BlockSpec((1, d, tf), lambda t, f, te, nr, nu: (te[tile(t, nu)], 0, ftile(t, f, nu))),
            pl.BlockSpec((1, d, tf), lambda t, f, te, nr, nu: (te[tile(t, nu)], 0, nf + ftile(t, f, nu))),
            pl.BlockSpec((1, tf, d), lambda t, f, te, nr, nu: (te[tile(t, nu)], ftile(t, f, nu), 0)),
        ],
        out_specs=pl.BlockSpec((tm, d), lambda t, f, te, nr, nu: (t, 0)),
        scratch_shapes=[pltpu.VMEM((tm, d), BF16), pltpu.VMEM((tm, d), F32)],
    )
    return pl.pallas_call(
        _expert_ffn_kernel,
        grid_spec=grid_spec,
        out_shape=jax.ShapeDtypeStruct((p_pad, d), F32),
        compiler_params=_cparams(("arbitrary", "arbitrary")),
        name="moe_expert_ffn",
    )(tile_e, tile_rows, n_used, hs, w1, w1, w2)


def _combine_kernel(pos0_ref, pos1_ref, ys_ref, x_ref, info_ref, gate_ref, g_ref, b_ref, o_ref, y0_s, y1_s, sem):
    tm = y0_s.shape[0]
    base = (pl.program_id(0) * pl.num_programs(1) + pl.program_id(1)) * tm

    def row_copy(r, pos_ref, dst):
        return pltpu.make_async_copy(ys_ref.at[pl.ds(pos_ref[base + r], 1), :], dst.at[pl.ds(r, 1), :], sem)

    def start(r, carry):
        row_copy(r, pos0_ref, y0_s).start(priority=0)
        row_copy(r, pos1_ref, y1_s).start(priority=1)
        return carry

    def wait(r, carry):
        row_copy(r, pos0_ref, y0_s).wait()
        row_copy(r, pos1_ref, y1_s).wait()
        return carry

    lax.fori_loop(0, tm, start, 0, unroll=8)
    lax.fori_loop(0, tm, wait, 0, unroll=8)
    info = info_ref[0]
    g0 = info[:, ROUTE_G0:ROUTE_G0 + 1]
    g1 = info[:, ROUTE_G1:ROUTE_G1 + 1]
    f = g0 * y0_s[...] + g1 * y1_s[...]
    z = DN_ALPHA * x_ref[0] + gate_ref[0] * f
    o_ref[0] = _layernorm_rows(z, g_ref[...], b_ref[...])


def _combine_ln(ys, pos0, pos1, x, info, gate, ln_g, ln_b):
    b, t, d = x.shape
    tm = 256
    grid_spec = pltpu.PrefetchScalarGridSpec(
        num_scalar_prefetch=2,
        grid=(b, t // tm),
        in_specs=[
            pl.BlockSpec(memory_space=pl.ANY),
            pl.BlockSpec((1, tm, d), lambda bi, i, p0, p1: (bi, i, 0)),
            pl.BlockSpec((1, tm, LANES), lambda bi, i, p0, p1: (bi, i, 0)),
            pl.BlockSpec((1, 1, d), lambda bi, i, p0, p1: (bi, 0, 0)),
            pl.BlockSpec((1, d), lambda bi, i, p0, p1: (0, 0)),
            pl.BlockSpec((1, d), lambda bi, i, p0, p1: (0, 0)),
        ],
        out_specs=pl.BlockSpec((1, tm, d), lambda bi, i, p0, p1: (bi, i, 0)),
        scratch_shapes=[pltpu.VMEM((tm, d), F32), pltpu.VMEM((tm, d), F32), pltpu.SemaphoreType.DMA],
    )
    return pl.pallas_call(
        _combine_kernel,
        grid_spec=grid_spec,
        out_shape=jax.ShapeDtypeStruct((b, t, d), F32),
        compiler_params=_cparams(("arbitrary", "arbitrary")),
        name="moe_combine_ln",
    )(pos0, pos1, ys, x, info, gate, ln_g.reshape(1, d), ln_b.reshape(1, d))


def _moe_ln(x, scale, shift, w_router, w1, w2, gate, ln_g, ln_b):
    b, t, d = x.shape
    n = b * t
    tm = MOE_TM
    h, info, counts = _router(x, scale, shift, w_router)
    cnt = counts[0, :N_EXPERTS].astype(jnp.int32)
    padded = ((cnt + tm - 1) // tm) * tm
    ends = jnp.cumsum(padded)
    starts = ends - padded
    info2 = info.reshape(n, LANES)
    e0 = info2[:, ROUTE_E0].astype(jnp.int32)
    e1 = info2[:, ROUTE_E1].astype(jnp.int32)
    pos0 = starts[e0] + info2[:, ROUTE_R0].astype(jnp.int32)
    pos1 = starts[e1] + info2[:, ROUTE_R1].astype(jnp.int32)
    n_tiles = (TOP_K * n) // tm + N_EXPERTS
    tile_start = jnp.arange(n_tiles, dtype=jnp.int32) * tm
    tile_e = jnp.minimum(jnp.sum(tile_start[:, None] >= ends[None, :], axis=1), N_EXPERTS - 1).astype(jnp.int32)
    tile_rows = jnp.clip(starts[tile_e] + cnt[tile_e] - tile_start, 0, tm).astype(jnp.int32)
    n_used = (ends[-1] // tm).astype(jnp.int32).reshape(1)
    hs = _dispatch(h.reshape(n, d), pos0, pos1, n_tiles * tm)
    ys = _expert_ffn(hs, tile_e, tile_rows, n_used, w1, w2)
    return _combine_ln(ys, pos0, pos1, x, info, gate, ln_g, ln_b)


def _pair_perm(comp_offsets):
    even = np.concatenate([off + np.arange(0, ROT_DIM, 2) for off in comp_offsets])
    odd = np.concatenate([off + np.arange(1, ROT_DIM, 2) for off in comp_offsets])
    return np.concatenate([even, odd])


def _even_w_in(w_in):
    d = w_in.shape[0]
    a_qk = DA_HEADS * 2 * DA_DH
    head_perm = _pair_perm((0, DA_DH))
    qk_perm = np.concatenate([hh * 2 * DA_DH + head_perm for hh in range(DA_HEADS)])
    cols = np.concatenate([qk_perm, a_qk + qk_perm, np.arange(2 * a_qk, w_in.shape[1])])
    w = w_in[:, cols]
    return jnp.pad(w, ((0, 0), (0, EV_NPAD - w.shape[1]))).astype(BF16)


def _odd_w_in(w_in):
    o_ckv = MLA_Q_RANK
    o_kr = o_ckv + MLA_KV_RANK
    o_rest = o_kr + MLA_ROPE
    ev = o_kr + np.arange(0, MLA_ROPE, 2)
    od = o_kr + np.arange(1, MLA_ROPE, 2)
    cols = np.concatenate([np.arange(0, o_kr), ev, ev, od, od, np.arange(o_rest, w_in.shape[1])])
    return w_in[:, cols].astype(BF16)


def _mla_weights(w_uq, w_ukv):
    hq = MLA_NOPE + MLA_ROPE
    q3 = w_uq.reshape(MLA_Q_RANK, MLA_HEADS, hq).transpose(1, 0, 2)
    zeros = jnp.zeros((MLA_HEADS, MLA_Q_RANK, MLA_ROPE // 2), w_uq.dtype)
    rope = q3[:, :, MLA_NOPE:]
    wq = jnp.concatenate([q3[:, :, :MLA_NOPE], rope[:, :, 0::2], zeros, rope[:, :, 1::2], zeros], axis=-1)
    wkv = w_ukv.reshape(MLA_KV_RANK, MLA_HEADS, MLA_NOPE + MLA_DV).transpose(1, 0, 2)
    return wq.astype(BF16), wkv.astype(BF16)


def _gla_gate_weights(gk_w2, gk_b):
    pairs = GLA_HEADS // 2
    w = jnp.zeros((pairs, 2, LANES, LANES), F32)
    for d in range(2):
        blk = gk_w2[d].reshape(GLA_LR, pairs, LANES).transpose(1, 0, 2)
        w = w.at[:, d, d * GLA_LR:(d + 1) * GLA_LR, :].set(blk)
    bias = gk_b.reshape(2, pairs, 1, LANES).transpose(1, 0, 2, 3).astype(F32)
    return w.astype(BF16), bias


def _rope_tables(rows):
    n_freq = ROT_DIM // 4
    inv = ROPE_BASE ** (-jnp.arange(n_freq, dtype=F32) / n_freq)
    row = jnp.repeat(jnp.arange(rows, dtype=F32), GRID_W)
    col = jnp.tile(jnp.arange(GRID_W, dtype=F32), rows)
    ang = jnp.concatenate([row[:, None] * inv, col[:, None] * inv], axis=-1)
    cos, sin = jnp.cos(ang), jnp.sin(ang)
    return jnp.concatenate([cos] * 4, axis=-1), jnp.concatenate([-sin, -sin, sin, sin], axis=-1)


def _diff_lambda_init(layer):
    return 0.8 - 0.6 * math.exp(-0.3 * layer)


def kernel(x, c, ctx, c_ctx, ada_w, ada_b, post_ln_g, post_ln_b, lb_table, ev_w_in, ev_lam, ev_subln_g, ev_gk_w2,
           ev_gk_b, ev_gla_norm_g, ev_w_out, ev_ffn_w1, ev_ffn_w2, od_w_in, od_q_norm_g, od_kv_norm_g, od_w_uq,
           od_w_ukv, od_hg_norm_g, od_w_out, od_router, od_exp_w1, od_exp_w2):
    b, t, d = x.shape
    tc = ctx.shape[1]
    rope_c, rope_s = _rope_tables(t // GRID_W)
    lb_soft = jax.nn.softmax(lb_table.astype(F32), axis=0)
    lower_bounds = jnp.cumsum(lb_soft, axis=0) - lb_soft[0]

    n_cond = ((b + 1 + 7) // 8) * 8
    cond = jnp.zeros((n_cond, d), F32).at[:b].set(c).at[b].set(c_ctx)
    mods = _ada(cond, ada_w, ada_b).reshape(DEPTH, n_cond, 6, d)

    ctx_flat = None
    for layer in range(DEPTH):
        last = layer == DEPTH - 1
        j = layer // 2
        m_l = [mods[layer, :b, i][:, None, :] for i in range(6)]
        m_c = [jnp.broadcast_to(mods[layer, b, i][None, None, :], (b, 1, d)) for i in range(6)]
        m_c1 = [m[:1] for m in m_c]
        even = layer % 2 == 0
        w_in = _even_w_in(ev_w_in[j]) if even else _odd_w_in(od_w_in[j])
        p_l = _proj(x, 1.0 + m_l[1], m_l[0], w_in)
        p_c = _proj(ctx, 1.0 + m_c[1], m_c[0], w_in)
        need_ctx = not last
        if even:
            lam_init = _diff_lambda_init(layer)
            lv = ev_lam[j].astype(F32)
            lam = jnp.exp(jnp.sum(lv[0] * lv[1])) - jnp.exp(jnp.sum(lv[2] * lv[3])) + lam_init
            oa_c, oa_l = _diff_attention(p_l, p_c, rope_c, rope_s, lam, ev_subln_g[j], lam_init, need_ctx)
            wgk, bgk = _gla_gate_weights(ev_gk_w2[j], ev_gk_b[j])
            ob_c, ob_l = _gla_scan(p_c, p_l, wgk, bgk, ev_gla_norm_g[j], need_ctx)
            w_out = ev_w_out[j].astype(BF16)
        else:
            wq, wkv = _mla_weights(od_w_uq[j], od_w_ukv[j])
            oa_l = _mla_attention(p_l, p_c, rope_c, rope_s, od_q_norm_g[j], od_kv_norm_g[j], wq, wkv)
            oa_c = _mla_attention_ctx(p_c, od_q_norm_g[j], od_kv_norm_g[j], wq, wkv) if need_ctx else None
            ob_c, ob_l = _hgrn_scan(p_c, p_l, lower_bounds[layer], od_hg_norm_g[j], need_ctx)
            w_out = od_w_out[j].astype(BF16)
        g0, b0 = post_ln_g[layer, 0], post_ln_b[layer, 0]
        g1, b1 = post_ln_g[layer, 1], post_ln_b[layer, 1]
        x = _outproj_ln(oa_l, ob_l, w_out, x, m_l[2], g0, b0)
        if even:
            w1, w2 = ev_ffn_w1[j], ev_ffn_w2[j]
            x = _ffn_ln(x, 1.0 + m_l[4], m_l[3], w1, w2, m_l[5], g1, b1)
        else:
            w1, w2 = od_exp_w1[j], od_exp_w2[j]
            x = _moe_ln(x, 1.0 + m_l[4], m_l[3], od_router[j], w1, w2, m_l[5], g1, b1)
        if need_ctx:
            ctx = _outproj_ln(oa_c, ob_c, w_out, ctx, m_c[2], g0, b0)
            ctx_flat = ctx.reshape(1, b * tc, d)
            if even:
                ctx_flat = _ffn_ln(ctx_flat, 1.0 + m_c1[4], m_c1[3], w1, w2, m_c1[5], g1, b1)
            else:
                ctx_flat = _moe_ln(ctx_flat, 1.0 + m_c1[4], m_c1[3], od_router[j], w1, w2, m_c1[5], g1, b1)
            ctx = ctx_flat.reshape(b, tc, d)
    return x
```

```python
import functools
import math

import jax
import jax.numpy as jnp
import numpy as np
from jax import lax
from jax.experimental import pallas as pl
from jax.experimental.pallas import tpu as pltpu

F32 = jnp.float32
BF16 = jnp.bfloat16

DEPTH = 2
GRID_W = 64
ROT_DIM = 64
ROPE_BASE = 10000.0
DA_HEADS = 4
DA_DH = ROT_DIM
DA_DV = 2 * DA_DH
GLA_HEADS = 4
GLA_DK = 64
GLA_DV = 128
GLA_LR = 16
GLA_NORMALIZER = 16.0
MLA_HEADS = 4
MLA_Q_RANK = 256
MLA_KV_RANK = 128
MLA_NOPE = 128
MLA_ROPE = ROT_DIM
MLA_DV = 128
MLA_SCALE = (MLA_NOPE + MLA_ROPE) ** -0.5
HG_HEADS = 4
HG_DK = 128
HG_DV = 128
D_FF = 3584
N_EXPERTS = 8
TOP_K = 2
LN_EPS = 1e-5
RMS_EPS = 1e-6
DN_ALPHA = (2 * DEPTH) ** 0.25

LANES = 128
VMEM_LIMIT = 56 * 1024 * 1024
ATTN_TQ = 256
SCAN_BLOCK = 256
SCAN_CHUNK = 32
MOE_TM = 1024
FFN_TF = 512

EV_QA, EV_KA, EV_VA, EV_QB, EV_KB, EV_VB, EV_GB, EV_LR = 0, 4, 8, 12, 14, 16, 20, 24
EV_NPAD = 25 * LANES
OD_CQ, OD_CKV, OD_KR, OD_HQ, OD_FF, OD_FB, OD_HI, OD_HG = 0, 2, 3, 4, 8, 12, 16, 20
OD_NPAD = 24 * LANES


def _cparams(sem, flags=None):
    return pltpu.CompilerParams(dimension_semantics=sem, vmem_limit_bytes=VMEM_LIMIT, flags=flags)


def _silu(v):
    return v * jax.nn.sigmoid(v)


def _layernorm_rows(z, g, b):
    mu = jnp.mean(z, axis=-1, keepdims=True)
    zc = z - mu
    var = jnp.mean(zc * zc, axis=-1, keepdims=True)
    return zc * lax.rsqrt(var + LN_EPS) * g + b


def _rmsnorm_rows(v, g):
    return v * lax.rsqrt(jnp.mean(v * v, axis=-1, keepdims=True) + RMS_EPS) * g


def _dot_nt(a, b):
    return lax.dot_general(a, b, (((1,), (1,)), ((), ())), preferred_element_type=F32)


def _dot_tn(a, b):
    return lax.dot_general(a, b, (((0,), (0,)), ((), ())), preferred_element_type=F32)


def _dot(a, b):
    return jnp.dot(a, b, preferred_element_type=F32)


def _ada_kernel(c_ref, w_ref, b_ref, o_ref):
    s = _silu(c_ref[...]).astype(BF16)
    o_ref[0] = _dot(s, w_ref[0].astype(BF16)) + b_ref[0]


def _ada(cond, ada_w, ada_b):
    depth, d, n = ada_w.shape
    r = cond.shape[0]
    tn = n // 4
    return pl.pallas_call(
        _ada_kernel,
        grid=(depth, n // tn),
        in_specs=[
            pl.BlockSpec((r, d), lambda l, j: (0, 0)),
            pl.BlockSpec((1, d, tn), lambda l, j: (l, 0, j)),
            pl.BlockSpec((1, 1, tn), lambda l, j: (l, 0, j)),
        ],
        out_specs=pl.BlockSpec((1, r, tn), lambda l, j: (l, 0, j)),
        out_shape=jax.ShapeDtypeStruct((depth, r, n), F32),
        compiler_params=_cparams(("parallel", "parallel")),
        name="ada_modulation",
    )(cond, ada_w, ada_b.reshape(depth, 1, n))


def _proj_kernel(x_ref, sc_ref, sh_ref, w_ref, o_ref):
    h = (x_ref[0] * sc_ref[0] + sh_ref[0]).astype(BF16)
    o_ref[0] = _dot(h, w_ref[...]).astype(o_ref.dtype)


def _proj(x, scale, shift, w):
    b, t, d = x.shape
    n = w.shape[1]
    tm = min(t, 512)
    return pl.pallas_call(
        _proj_kernel,
        grid=(b, t // tm),
        in_specs=[
            pl.BlockSpec((1, tm, d), lambda bi, i: (bi, i, 0)),
            pl.BlockSpec((1, 1, d), lambda bi, i: (bi, 0, 0)),
            pl.BlockSpec((1, 1, d), lambda bi, i: (bi, 0, 0)),
            pl.BlockSpec((d, n), lambda bi, i: (0, 0)),
        ],
        out_specs=pl.BlockSpec((1, tm, n), lambda bi, i: (bi, i, 0)),
        out_shape=jax.ShapeDtypeStruct((b, t, n), BF16),
        compiler_params=_cparams(("parallel", "parallel")),
        name="mod_proj",
    )(x, scale, shift, w)


def _rope128(t, cs, sn):
    t = t.astype(F32)
    return t * cs + pltpu.roll(t, LANES // 2, axis=1) * sn


LOG2E = math.log2(math.e)


def _softmax_pv(s2, v_bf):
    m = jnp.max(s2, axis=-1, keepdims=True)
    e = jnp.exp2(s2 - m)
    return _dot(e.astype(BF16), v_bf) / jnp.sum(e, axis=-1, keepdims=True)


def _q1_lane_mask(shape):
    lane = lax.broadcasted_iota(jnp.int32, shape, 1)
    return (lane // 32) % 2 == 0


def _diff_scores_out(q, k_bf, v_bf, lam):
    m1 = _q1_lane_mask(q.shape)
    q1 = jnp.where(m1, q, 0.0).astype(BF16)
    q2 = jnp.where(m1, 0.0, q).astype(BF16)
    s1 = _dot_nt(q1, k_bf)
    s2 = _dot_nt(q2, k_bf)
    e1 = jnp.exp2(s1 - jnp.max(s1, axis=-1, keepdims=True))
    e2 = jnp.exp2(s2 - jnp.max(s2, axis=-1, keepdims=True))
    r1 = 1.0 / jnp.sum(e1, axis=-1, keepdims=True)
    r2 = lam / jnp.sum(e2, axis=-1, keepdims=True)
    return _dot((e1 * r1 - e2 * r2).astype(BF16), v_bf)


def _diffattn_lat_kernel(lam_init, tc, q_ref, kl_ref, vl_ref, kc_ref, vc_ref, cq_ref, sq_ref, ck_ref, sk_ref,
                         lam_ref, g_ref, o_ref, k_s, v_s):
    @pl.when(pl.program_id(2) == 0)
    def _():
        k_s[0:tc, :] = kc_ref[0].astype(BF16)
        v_s[0:tc, :] = vc_ref[0].astype(BF16)
        k_s[tc:, :] = _rope128(kl_ref[0], ck_ref[...], sk_ref[...]).astype(BF16)
        v_s[tc:, :] = vl_ref[0].astype(BF16)

    q = _rope128(q_ref[0], cq_ref[...], sq_ref[...]) * (DA_DH ** -0.5 * LOG2E)
    o = _diff_scores_out(q, k_s[...], v_s[...], lam_ref[0, 0])
    o_ref[0] = (_rmsnorm_rows(o, g_ref[...]) * (1.0 - lam_init)).astype(o_ref.dtype)


def _diffattn_ctx_kernel(lam_init, q_ref, k_ref, v_ref, lam_ref, g_ref, o_ref):
    q = q_ref[0].astype(F32) * (DA_DH ** -0.5 * LOG2E)
    o = _diff_scores_out(q, k_ref[0].astype(BF16), v_ref[0].astype(BF16), lam_ref[0, 0])
    o_ref[0] = (_rmsnorm_rows(o, g_ref[...]) * (1.0 - lam_init)).astype(o_ref.dtype)


def _diff_attention(p_l, p_c, rope_c, rope_s, lam, subln_g, lam_init, need_ctx):
    b, tl, _ = p_l.shape
    tc = p_c.shape[1]
    h = DA_HEADS
    tq = min(tl, ATTN_TQ)
    lam2 = lam.reshape(1, 1).astype(F32)
    g2 = subln_g.reshape(1, DA_DV).astype(F32)
    smem = pl.BlockSpec(memory_space=pltpu.SMEM)
    o_l = pl.pallas_call(
        functools.partial(_diffattn_lat_kernel, lam_init, tc),
        grid=(b, h, tl // tq),
        in_specs=[
            pl.BlockSpec((1, tq, LANES), lambda bi, hi, i: (bi, i, EV_QA + hi)),
            pl.BlockSpec((1, tl, LANES), lambda bi, hi, i: (bi, 0, EV_KA + hi)),
            pl.BlockSpec((1, tl, LANES), lambda bi, hi, i: (bi, 0, EV_VA + hi)),
            pl.BlockSpec((1, tc, LANES), lambda bi, hi, i: (bi, 0, EV_KA + hi)),
            pl.BlockSpec((1, tc, LANES), lambda bi, hi, i: (bi, 0, EV_VA + hi)),
            pl.BlockSpec((tq, LANES), lambda bi, hi, i: (i, 0)),
            pl.BlockSpec((tq, LANES), lambda bi, hi, i: (i, 0)),
            pl.BlockSpec((tl, LANES), lambda bi, hi, i: (0, 0)),
            pl.BlockSpec((tl, LANES), lambda bi, hi, i: (0, 0)),
            smem,
            pl.BlockSpec((1, LANES), lambda bi, hi, i: (0, 0)),
        ],
        out_specs=pl.BlockSpec((1, tq, LANES), lambda bi, hi, i: (bi, i, hi)),
        out_shape=jax.ShapeDtypeStruct((b, tl, h * DA_DV), BF16),
        scratch_shapes=[pltpu.VMEM((tc + tl, LANES), BF16), pltpu.VMEM((tc + tl, LANES), BF16)],
        compiler_params=_cparams(("parallel", "parallel", "arbitrary")),
        name="diff_attention_latent",
    )(p_l, p_l, p_l, p_c, p_c, rope_c, rope_s, rope_c, rope_s, lam2, g2)
    if not need_ctx:
        return None, o_l
    o_c = pl.pallas_call(
        functools.partial(_diffattn_ctx_kernel, lam_init),
        grid=(b, h),
        in_specs=[
            pl.BlockSpec((1, tc, LANES), lambda bi, hi: (bi, 0, EV_QA + hi)),
            pl.BlockSpec((1, tc, LANES), lambda bi, hi: (bi, 0, EV_KA + hi)),
            pl.BlockSpec((1, tc, LANES), lambda bi, hi: (bi, 0, EV_VA + hi)),
            smem,
            pl.BlockSpec((1, LANES), lambda bi, hi: (0, 0)),
        ],
        out_specs=pl.BlockSpec((1, tc, LANES), lambda bi, hi: (bi, 0, hi)),
        out_shape=jax.ShapeDtypeStruct((b, tc, h * DA_DV), BF16),
        compiler_params=_cparams(("parallel", "parallel")),
        name="diff_attention_ctx",
    )(p_c, p_c, p_c, lam2, g2)
    return o_c, o_l


def _mla_q(cq, qg, wq, cs, sn):
    q = _dot(_rmsnorm_rows(cq.astype(F32), qg).astype(BF16), wq)
    if cs is not None:
        q = jnp.concatenate([q[:, :LANES], _rope128(q[:, LANES:], cs, sn)], axis=1)
    return (q * (MLA_SCALE * LOG2E)).astype(BF16)


def _mla_kv(ckv, kr, kvg, wkv, cs, sn):
    kv = _dot(_rmsnorm_rows(ckv.astype(F32), kvg).astype(BF16), wkv)
    if cs is not None:
        kr = _rope128(kr, cs, sn)
    k = jnp.concatenate([kv[:, :LANES].astype(BF16), kr.astype(BF16)], axis=1)
    return k, kv[:, LANES:].astype(BF16)


def _mla_lat_kernel(tc, cq_ref, ckvl_ref, krl_ref, ckvc_ref, krc_ref, cq_c_ref, cq_s_ref, ck_ref, sk_ref,
                    qg_ref, kvg_ref, wq_ref, wkv_ref, o_ref, k_s, v_s):
    @pl.when(pl.program_id(2) == 0)
    def _():
        kc, vc = _mla_kv(ckvc_ref[0], krc_ref[0], kvg_ref[...], wkv_ref[0], None, None)
        k_s[0:tc, :] = kc
        v_s[0:tc, :] = vc
        kl, vl = _mla_kv(ckvl_ref[0], krl_ref[0], kvg_ref[...], wkv_ref[0], ck_ref[...], sk_ref[...])
        k_s[tc:, :] = kl
        v_s[tc:, :] = vl

    q = _mla_q(cq_ref[0], qg_ref[...], wq_ref[0], cq_c_ref[...], cq_s_ref[...])
    o_ref[0] = _softmax_pv(_dot_nt(q, k_s[...]), v_s[...]).astype(o_ref.dtype)


def _mla_attention(p_l, p_c, rope_c, rope_s, q_norm_g, kv_norm_g, wq, wkv):
    b, tl, _ = p_l.shape
    tc = p_c.shape[1]
    h = MLA_HEADS
    tq = min(tl, ATTN_TQ)
    return pl.pallas_call(
        functools.partial(_mla_lat_kernel, tc),
        grid=(b, h, tl // tq),
        in_specs=[
            pl.BlockSpec((1, tq, MLA_Q_RANK), lambda bi, hi, i: (bi, i, OD_CQ)),
            pl.BlockSpec((1, tl, LANES), lambda bi, hi, i: (bi, 0, OD_CKV)),
            pl.BlockSpec((1, tl, LANES), lambda bi, hi, i: (bi, 0, OD_KR)),
            pl.BlockSpec((1, tc, LANES), lambda bi, hi, i: (bi, 0, OD_CKV)),
            pl.BlockSpec((1, tc, LANES), lambda bi, hi, i: (bi, 0, OD_KR)),
            pl.BlockSpec((tq, LANES), lambda bi, hi, i: (i, 0)),
            pl.BlockSpec((tq, LANES), lambda bi, hi, i: (i, 0)),
            pl.BlockSpec((tl, LANES), lambda bi, hi, i: (0, 0)),
            pl.BlockSpec((tl, LANES), lambda bi, hi, i: (0, 0)),
            pl.BlockSpec((1, MLA_Q_RANK), lambda bi, hi, i: (0, 0)),
            pl.BlockSpec((1, MLA_KV_RANK), lambda bi, hi, i: (0, 0)),
            pl.BlockSpec((1, MLA_Q_RANK, 2 * LANES), lambda bi, hi, i: (hi, 0, 0)),
            pl.BlockSpec((1, MLA_KV_RANK, 2 * LANES), lambda bi, hi, i: (hi, 0, 0)),
        ],
        out_specs=pl.BlockSpec((1, tq, LANES), lambda bi, hi, i: (bi, i, hi)),
        out_shape=jax.ShapeDtypeStruct((b, tl, h * MLA_DV), BF16),
        scratch_shapes=[pltpu.VMEM((tc + tl, 2 * LANES), BF16), pltpu.VMEM((tc + tl, LANES), BF16)],
        compiler_params=_cparams(("parallel", "parallel", "arbitrary")),
        name="mla_attention_latent",
    )(p_l, p_l, p_l, p_c, p_c, rope_c, rope_s, rope_c, rope_s,
      q_norm_g.reshape(1, -1), kv_norm_g.reshape(1, -1), wq, wkv)


def _mla_ctx_kernel(cq_ref, ckv_ref, kr_ref, qg_ref, kvg_ref, wq_ref, wkv_ref, o_ref):
    k, v = _mla_kv(ckv_ref[0], kr_ref[0], kvg_ref[...], wkv_ref[0], None, None)
    q = _mla_q(cq_ref[0], qg_ref[...], wq_ref[0], None, None)
    o_ref[0] = _softmax_pv(_dot_nt(q, k), v).astype(o_ref.dtype)


def _mla_attention_ctx(p_c, q_norm_g, kv_norm_g, wq, wkv):
    b, tc, _ = p_c.shape
    h = MLA_HEADS
    return pl.pallas_call(
        _mla_ctx_kernel,
        grid=(b, h),
        in_specs=[
            pl.BlockSpec((1, tc, MLA_Q_RANK), lambda bi, hi: (bi, 0, OD_CQ)),
            pl.BlockSpec((1, tc, LANES), lambda bi, hi: (bi, 0, OD_CKV)),
            pl.BlockSpec((1, tc, LANES), lambda bi, hi: (bi, 0, OD_KR)),
            pl.BlockSpec((1, MLA_Q_RANK), lambda bi, hi: (0, 0)),
            pl.BlockSpec((1, MLA_KV_RANK), lambda bi, hi: (0, 0)),
            pl.BlockSpec((1, MLA_Q_RANK, 2 * LANES), lambda bi, hi: (hi, 0, 0)),
            pl.BlockSpec((1, MLA_KV_RANK, 2 * LANES), lambda bi, hi: (hi, 0, 0)),
        ],
        out_specs=pl.BlockSpec((1, tc, LANES), lambda bi, hi: (bi, 0, hi)),
        out_shape=jax.ShapeDtypeStruct((b, tc, h * MLA_DV), BF16),
        compiler_params=_cparams(("parallel", "parallel")),
        name="mla_attention_ctx",
    )(p_c, p_c, p_c, q_norm_g.reshape(1, -1), kv_norm_g.reshape(1, -1), wq, wkv)


def _scan_levels(bt):
    levels = []
    c = SCAN_CHUNK
    while c <= bt:
        levels.append(c)
        c *= 2
    return levels


def _level_table(bt, reverse):
    i = np.arange(bt)[:, None]
    j = np.arange(bt)[None, :]
    if reverse:
        i, j = j, i
    tab = np.zeros((bt, bt), np.int32)
    for lvl, c in enumerate(_scan_levels(bt), start=1):
        same = (i // c) == (j // c)
        if lvl == 1:
            m = same & (j <= i)
        else:
            m = same & ((i % c) >= c // 2) & ((j % c) < c // 2)
        tab[m] = lvl
    return tab


def _chunk_row(a, c, r):
    bt, n = a.shape
    a3 = a.reshape(bt // c, c, n)
    return jnp.broadcast_to(a3[:, r:r + 1, :], (bt // c, c, n)).reshape(bt, n)


def _scan_block(q_s, k_s, g_s, v_s, o_s, tri_ref, lvl_ref, start, st, reverse, compute_out):
    bt = SCAN_BLOCK
    rows = pl.ds(start, bt)
    g = g_s[rows, :]
    k = k_s[rows, :]
    v = v_s[rows, :]
    g_hi = g.astype(BF16)
    g_lo = (g - g_hi.astype(F32)).astype(BF16)
    tri = tri_ref[...]
    gc = _dot(tri, g_hi) + _dot(tri, g_lo)
    g_tot = gc[0:1, :] if reverse else gc[bt - 1:bt, :]
    kd = (k * jnp.exp(g_tot - gc)).astype(BF16)
    st_new = st * jnp.exp(g_tot) + _dot_tn(v, kd)
    if compute_out:
        q = q_s[rows, :]
        o = _dot_nt((q * jnp.exp(gc)).astype(BF16), st.astype(BF16))
        lvl = lvl_ref[...]
        att = jnp.zeros((bt, bt), F32)
        for li, c in enumerate(_scan_levels(bt), start=1):
            r = _chunk_row(gc, c, c // 2 if reverse else c // 2 - 1)
            a = _dot_nt((q * jnp.exp(gc - r)).astype(BF16), (k * jnp.exp(r - gc)).astype(BF16))
            att = jnp.where(lvl == li, a, att)
        o_s[rows, :] = o + _dot(att.astype(BF16), v)
    return st_new


def _scan_all(chains, tri_refs, lvl_refs, tc, tl, need_ctx):
    bt = SCAN_BLOCK
    nc, nl = tc // bt, tl // bt

    def run(first, n, carry, compute_out):
        def body(i, carry):
            new = []
            for ci, ch in enumerate(chains):
                for d in range(2):
                    blk = first + (n - 1 - i if d else i)
                    start = pl.multiple_of(blk * bt, bt)
                    new.append(_scan_block(ch["q"], ch["k"][d], ch["g"][d], ch["v"], ch["o"][d], tri_refs[d],
                                           lvl_refs[d], start, carry[2 * ci + d], d == 1, compute_out))
            return tuple(new)
        return lax.fori_loop(0, n, body, carry)

    zero = jnp.zeros((LANES, LANES), F32)
    carry = run(0, nc, (zero,) * (2 * len(chains)), need_ctx)
    run(nc, nl, carry, True)


def _scan_finish(chain, cols, gate_c_ref, gate_l_ref, ng_ref, oc_ref, ol_ref, tc, need_ctx):
    ng = ng_ref[...]
    of_s, ob_s = chain["o"]
    if need_ctx:
        o = of_s[0:tc, :] + ob_s[0:tc, :]
        oc_ref[0, :, cols] = (_rmsnorm_rows(o, ng) * _silu(gate_c_ref[0, :, cols].astype(F32))).astype(oc_ref.dtype)
    o = of_s[tc:, :] + ob_s[tc:, :]
    ol_ref[0, :, cols] = (_rmsnorm_rows(o, ng) * _silu(gate_l_ref[0, :, cols].astype(F32))).astype(ol_ref.dtype)


SCAN_HEADS = 2
SCAN_SCRATCH = 8


def _scan_unpack(refs, n_in, need_ctx):
    ins = refs[:n_in]
    if need_ctx:
        oc_ref, ol_ref = refs[n_in:n_in + 2]
        scratch = refs[n_in + 2:]
    else:
        oc_ref, ol_ref = None, refs[n_in]
        scratch = refs[n_in + 1:]
    chains = []
    for hh in range(SCAN_HEADS):
        q_s, v_s, kf_s, kb_s, gf_s, gb_s, of_s, ob_s = scratch[SCAN_SCRATCH * hh:SCAN_SCRATCH * (hh + 1)]
        chains.append({"q": q_s, "v": v_s, "k": (kf_s, kb_s), "g": (gf_s, gb_s), "o": (of_s, ob_s)})
    return ins, oc_ref, ol_ref, chains


def _gla_scan_kernel(tc, tl, need_ctx, *refs):
    ins, oc_ref, ol_ref, chains = _scan_unpack(refs, 17, need_ctx)
    (qc_ref, kc_ref, vc_ref, gbc_ref, lrc_ref, ql_ref, kl_ref, vl_ref, gbl_ref, lrl_ref,
     wgk_ref, bgk_ref, ng_ref, trif_ref, trib_ref, lvlf_ref, lvlb_ref) = ins
    lane = lax.broadcasted_iota(jnp.int32, (1, LANES), 1)
    for hh, ch in enumerate(chains):
        cols = slice(hh * LANES, (hh + 1) * LANES)
        mine = (lane // GLA_DK) == hh
        q_s, v_s, (k_s, _) = ch["q"], ch["v"], ch["k"]
        ch["k"] = (k_s, k_s)
        for (q_ref, k_ref, v_ref), lo, hi in (((qc_ref, kc_ref, vc_ref), 0, tc),
                                              ((ql_ref, kl_ref, vl_ref), tc, tc + tl)):
            q_s[lo:hi, :] = jnp.where(mine, q_ref[0].astype(F32), 0.0) * (GLA_DK ** -0.5)
            k_s[lo:hi, :] = jnp.where(mine, k_ref[0].astype(F32), 0.0)
            v_s[lo:hi, :] = v_ref[0, :, cols]
    for d in range(2):
        for lr_ref, lo, hi in ((lrc_ref, 0, tc), (lrl_ref, tc, tc + tl)):
            g_pair = jax.nn.log_sigmoid(_dot(lr_ref[0], wgk_ref[0, d]) + bgk_ref[0, d]) / GLA_NORMALIZER
            for hh, ch in enumerate(chains):
                ch["g"][d][lo:hi, :] = jnp.where((lane // GLA_DK) == hh, g_pair, 0.0)
    _scan_all(chains, (trif_ref, trib_ref), (lvlf_ref, lvlb_ref), tc, tl, need_ctx)
    for hh, ch in enumerate(chains):
        _scan_finish(ch, slice(hh * LANES, (hh + 1) * LANES), gbc_ref, gbl_ref, ng_ref, oc_ref, ol_ref, tc, need_ctx)


def _hgrn_scan_kernel(tc, tl, need_ctx, *refs):
    ins, oc_ref, ol_ref, chains = _scan_unpack(refs, 16, need_ctx)
    (qc_ref, ffc_ref, fbc_ref, vc_ref, gtc_ref, ql_ref, ffl_ref, fbl_ref, vl_ref, gtl_ref,
     lb_ref, ng_ref, trif_ref, trib_ref, lvlf_ref, lvlb_ref) = ins
    for hh, ch in enumerate(chains):
        cols = slice(hh * LANES, (hh + 1) * LANES)
        lb = lb_ref[:, cols]
        for (q_ref, v_ref), lo, hi in (((qc_ref, vc_ref), 0, tc), ((ql_ref, vl_ref), tc, tc + tl)):
            ch["q"][lo:hi, :] = q_ref[0, :, cols].astype(F32)
            ch["v"][lo:hi, :] = v_ref[0, :, cols]
        for (fc_ref, fl_ref), k_s, g_s in (((ffc_ref, ffl_ref), ch["k"][0], ch["g"][0]),
                                           ((fbc_ref, fbl_ref), ch["k"][1], ch["g"][1])):
            for f_ref, lo, hi in ((fc_ref, 0, tc), (fl_ref, tc, tc + tl)):
                f = lb + (1.0 - lb) * jax.nn.sigmoid(f_ref[0, :, cols].astype(F32))
                k_s[lo:hi, :] = 1.0 - f
                g_s[lo:hi, :] = jnp.log(f)
    _scan_all(chains, (trif_ref, trib_ref), (lvlf_ref, lvlb_ref), tc, tl, need_ctx)
    for hh, ch in enumerate(chains):
        _scan_finish(ch, slice(hh * LANES, (hh + 1) * LANES), gtc_ref, gtl_ref, ng_ref, oc_ref, ol_ref, tc, need_ctx)


def _scan_consts():
    bt = SCAN_BLOCK
    lower = np.tril(np.ones((bt, bt), np.float32))
    return (jnp.asarray(lower, BF16), jnp.asarray(lower.T, BF16),
            jnp.asarray(_level_table(bt, False)), jnp.asarray(_level_table(bt, True)))


def _scan_call(kernel_fn, name, p_c, p_l, col_specs, extra, extra_specs, need_ctx, heads):
    b, tl, _ = p_l.shape
    tc = p_c.shape[1]
    bt = SCAN_BLOCK
    t = tc + tl
    wide = SCAN_HEADS * LANES
    consts = _scan_consts()
    const_specs = [pl.BlockSpec((bt, bt), lambda bi, ji: (0, 0)) for _ in consts]
    in_specs = ([pl.BlockSpec((1, tc, g * LANES), f) for g, f in col_specs]
                + [pl.BlockSpec((1, tl, g * LANES), f) for g, f in col_specs] + extra_specs + const_specs)
    args = [p_c] * len(col_specs) + [p_l] * len(col_specs) + list(extra) + list(consts)
    out_l = jax.ShapeDtypeStruct((b, tl, heads * LANES), BF16)
    spec_l = pl.BlockSpec((1, tl, wide), lambda bi, ji: (bi, 0, ji))
    if need_ctx:
        out_shape = (jax.ShapeDtypeStruct((b, tc, heads * LANES), BF16), out_l)
        out_specs = (pl.BlockSpec((1, tc, wide), lambda bi, ji: (bi, 0, ji)), spec_l)
    else:
        out_shape, out_specs = out_l, spec_l
    per_head = [pltpu.VMEM((t, LANES), F32), pltpu.VMEM((t, LANES), BF16)] + [pltpu.VMEM((t, LANES), F32)] * 6
    res = pl.pallas_call(
        functools.partial(kernel_fn, tc, tl, need_ctx),
        grid=(b, heads // SCAN_HEADS),
        in_specs=in_specs,
        out_specs=out_specs,
        out_shape=out_shape,
        scratch_shapes=per_head * SCAN_HEADS,
        compiler_params=_cparams(("parallel", "parallel")),
        name=name,
    )(*args)
    return res if need_ctx else (None, res)


def _gla_scan(p_c, p_l, wgk, bgk, norm_g, need_ctx):
    col_specs = [
        (1, lambda bi, ji: (bi, 0, EV_QB + ji)),
        (1, lambda bi, ji: (bi, 0, EV_KB + ji)),
        (SCAN_HEADS, lambda bi, ji: (bi, 0, EV_VB // SCAN_HEADS + ji)),
        (SCAN_HEADS, lambda bi, ji: (bi, 0, EV_GB // SCAN_HEADS + ji)),
        (1, lambda bi, ji: (bi, 0, EV_LR)),
    ]
    extra_specs = [
        pl.BlockSpec((1, 2, LANES, LANES), lambda bi, ji: (ji, 0, 0, 0)),
        pl.BlockSpec((1, 2, 1, LANES), lambda bi, ji: (ji, 0, 0, 0)),
        pl.BlockSpec((1, LANES), lambda bi, ji: (0, 0)),
    ]
    return _scan_call(_gla_scan_kernel, "gla_scan", p_c, p_l, col_specs,
                      (wgk, bgk, norm_g.reshape(1, LANES)), extra_specs, need_ctx, GLA_HEADS)


def _hgrn_scan(p_c, p_l, lb, norm_g, need_ctx):
    col_specs = [(SCAN_HEADS, functools.partial(lambda base, bi, ji: (bi, 0, base // SCAN_HEADS + ji), base))
                 for base in (OD_HQ, OD_FF, OD_FB, OD_HI, OD_HG)]
    extra_specs = [
        pl.BlockSpec((1, SCAN_HEADS * LANES), lambda bi, ji: (0, ji)),
        pl.BlockSpec((1, LANES), lambda bi, ji: (0, 0)),
    ]
    return _scan_call(_hgrn_scan_kernel, "hgrn2_scan", p_c, p_l, col_specs,
                      (lb.reshape(1, -1), norm_g.reshape(1, LANES)), extra_specs, need_ctx, HG_HEADS)


def _outproj_kernel(oa_ref, ob_ref, w_ref, x_ref, gate_ref, g_ref, b_ref, o_ref):
    ka = oa_ref.shape[2]
    y = _dot(oa_ref[0], w_ref[0:ka, :]) + _dot(ob_ref[0], w_ref[ka:, :])
    z = DN_ALPHA * x_ref[0] + gate_ref[0] * y
    o_ref[0] = _layernorm_rows(z, g_ref[...], b_ref[...])


def _outproj_ln(oa, ob, w, x, gate, ln_g, ln_b):
    b, t, d = x.shape
    ka, kb = oa.shape[2], ob.shape[2]
    tm = min(t, 512)
    return pl.pallas_call(
        _outproj_kernel,
        grid=(b, t // tm),
        in_specs=[
            pl.BlockSpec((1, tm, ka), lambda bi, i: (bi, i, 0)),
            pl.BlockSpec((1, tm, kb), lambda bi, i: (bi, i, 0)),
            pl.BlockSpec((ka + kb, d), lambda bi, i: (0, 0)),
            pl.BlockSpec((1, tm, d), lambda bi, i: (bi, i, 0)),
            pl.BlockSpec((1, 1, d), lambda bi, i: (bi, 0, 0)),
            pl.BlockSpec((1, d), lambda bi, i: (0, 0)),
            pl.BlockSpec((1, d), lambda bi, i: (0, 0)),
        ],
        out_specs=pl.BlockSpec((1, tm, d), lambda bi, i: (bi, i, 0)),
        out_shape=jax.ShapeDtypeStruct((b, t, d), F32),
        compiler_params=_cparams(("parallel", "parallel")),
        name="outproj_residual_ln",
    )(oa, ob, w, x, gate, ln_g.reshape(1, d), ln_b.reshape(1, d))


def _ffn_kernel(x_ref, sc_ref, sh_ref, w1g_ref, w1u_ref, w2_ref, gate_ref, g_ref, b_ref, o_ref, h_s, acc_s):
    f = pl.program_id(2)

    @pl.when(f == 0)
    def _():
        h_s[...] = (x_ref[0] * sc_ref[0] + sh_ref[0]).astype(BF16)
        acc_s[...] = jnp.zeros_like(acc_s)

    h = h_s[...]
    a = _silu(_dot(h, w1g_ref[...].astype(BF16))) * _dot(h, w1u_ref[...].astype(BF16))
    acc_s[...] += _dot(a.astype(BF16), w2_ref[...].astype(BF16))

    @pl.when(f == pl.num_programs(2) - 1)
    def _():
        z = DN_ALPHA * x_ref[0] + gate_ref[0] * acc_s[...]
        o_ref[0] = _layernorm_rows(z, g_ref[...], b_ref[...])


def _ffn_ln(x, scale, shift, w1, w2, gate, ln_g, ln_b):
    b, t, d = x.shape
    ff = w2.shape[0]
    tf = FFN_TF
    nf = ff // tf
    tm = min(t, 1024)
    return pl.pallas_call(
        _ffn_kernel,
        grid=(b, t // tm, nf),
        in_specs=[
            pl.BlockSpec((1, tm, d), lambda bi, i, f: (bi, i, 0)),
            pl.BlockSpec((1, 1, d), lambda bi, i, f: (bi, 0, 0)),
            pl.BlockSpec((1, 1, d), lambda bi, i, f: (bi, 0, 0)),
            pl.BlockSpec((d, tf), lambda bi, i, f: (0, f)),
            pl.BlockSpec((d, tf), lambda bi, i, f: (0, nf + f)),
            pl.BlockSpec((tf, d), lambda bi, i, f: (f, 0)),
            pl.BlockSpec((1, 1, d), lambda bi, i, f: (bi, 0, 0)),
            pl.BlockSpec((1, d), lambda bi, i, f: (0, 0)),
            pl.BlockSpec((1, d), lambda bi, i, f: (0, 0)),
        ],
        out_specs=pl.BlockSpec((1, tm, d), lambda bi, i, f: (bi, i, 0)),
        out_shape=jax.ShapeDtypeStruct((b, t, d), F32),
        scratch_shapes=[pltpu.VMEM((tm, d), BF16), pltpu.VMEM((tm, d), F32)],
        compiler_params=_cparams(("parallel", "parallel", "arbitrary")),
        name="swiglu_residual_ln",
    )(x, scale, shift, w1, w1, w2, gate, ln_g.reshape(1, d), ln_b.reshape(1, d))


ROUTE_E0, ROUTE_E1, ROUTE_G0, ROUTE_G1, ROUTE_R0, ROUTE_R1 = range(6)


def _router_kernel(x_ref, sc_ref, sh_ref, wr_ref, h_ref, info_ref, cnt_ref, carry_s):
    first = (pl.program_id(0) == 0) & (pl.program_id(1) == 0)

    @pl.when(first)
    def _():
        carry_s[...] = jnp.zeros_like(carry_s)

    h = x_ref[0] * sc_ref[0] + sh_ref[0]
    h_ref[0] = h
    tm = h.shape[0]
    logits = jnp.dot(h, wr_ref[...], preferred_element_type=F32, precision=lax.Precision.HIGHEST)
    lane = lax.broadcasted_iota(jnp.int32, (tm, LANES), 1).astype(F32)
    neg = jnp.float32(-jnp.inf)
    logits = jnp.where(lane < N_EXPERTS, logits, neg)
    v0 = jnp.max(logits, axis=-1, keepdims=True)
    e0 = jnp.min(jnp.where(logits == v0, lane, float(LANES)), axis=-1, keepdims=True)
    rest = jnp.where(lane == e0, neg, logits)
    v1 = jnp.max(rest, axis=-1, keepdims=True)
    e1 = jnp.min(jnp.where(rest == v1, lane, float(LANES)), axis=-1, keepdims=True)
    d = jnp.exp(v1 - v0)
    g0 = 1.0 / (1.0 + d)
    g1 = d / (1.0 + d)
    oh0 = (lane == e0).astype(BF16)
    oh1 = (lane == e1).astype(BF16)
    ri = lax.broadcasted_iota(jnp.int32, (tm, tm), 0)
    ci = lax.broadcasted_iota(jnp.int32, (tm, tm), 1)
    before = (ci < ri).astype(BF16)
    c0 = _dot(before, oh0)
    c1 = _dot(before, oh1)
    tot0 = jnp.sum(oh0.astype(F32), axis=0, keepdims=True)
    tot1 = jnp.sum(oh1.astype(F32), axis=0, keepdims=True)
    carry = carry_s[...]
    r0 = jnp.sum(jnp.where(lane == e0, carry + c0, 0.0), axis=-1, keepdims=True)
    r1 = jnp.sum(jnp.where(lane == e1, carry + tot0 + c1, 0.0), axis=-1, keepdims=True)
    carry = carry + tot0 + tot1
    carry_s[...] = carry
    cnt_ref[...] = carry
    info = jnp.zeros((tm, LANES), F32)
    for col, val in ((ROUTE_E0, e0), (ROUTE_E1, e1), (ROUTE_G0, g0), (ROUTE_G1, g1), (ROUTE_R0, r0), (ROUTE_R1, r1)):
        info = jnp.where(lane == col, val, info)
    info_ref[0] = info


def _router(x, scale, shift, w_router):
    b, t, d = x.shape
    tm = min(t, 512)
    wr = jnp.zeros((d, LANES), F32).at[:, :N_EXPERTS].set(w_router.astype(F32))
    return pl.pallas_call(
        _router_kernel,
        grid=(b, t // tm),
        in_specs=[
            pl.BlockSpec((1, tm, d), lambda bi, i: (bi, i, 0)),
            pl.BlockSpec((1, 1, d), lambda bi, i: (bi, 0, 0)),
            pl.BlockSpec((1, 1, d), lambda bi, i: (bi, 0, 0)),
            pl.BlockSpec((d, LANES), lambda bi, i: (0, 0)),
        ],
        out_specs=(
            pl.BlockSpec((1, tm, d), lambda bi, i: (bi, i, 0)),
            pl.BlockSpec((1, tm, LANES), lambda bi, i: (bi, i, 0)),
            pl.BlockSpec((1, LANES), lambda bi, i: (0, 0)),
        ),
        out_shape=(
            jax.ShapeDtypeStruct((b, t, d), F32),
            jax.ShapeDtypeStruct((b, t, LANES), F32),
            jax.ShapeDtypeStruct((1, LANES), F32),
        ),
        scratch_shapes=[pltpu.VMEM((1, LANES), F32)],
        compiler_params=_cparams(("arbitrary", "arbitrary")),
        name="moe_router",
    )(x, scale, shift, wr)


DISPATCH_ROWS = 1024
COMBINE_ROWS = 512


def _dispatch_kernel(pos0_ref, pos1_ref, h_ref, zero_ref, hs_ref, sem):
    del zero_ref
    base = pl.program_id(0) * DISPATCH_ROWS

    def row_copy(r, pos_ref):
        return pltpu.make_async_copy(h_ref.at[pl.ds(r, 1), :], hs_ref.at[pl.ds(pos_ref[base + r], 1), :], sem)

    def start(r, carry):
        row_copy(r, pos0_ref).start(priority=0)
        row_copy(r, pos1_ref).start(priority=1)
        return carry

    def wait(r, carry):
        row_copy(r, pos0_ref).wait()
        row_copy(r, pos1_ref).wait()
        return carry

    lax.fori_loop(0, DISPATCH_ROWS, start, 0, unroll=8)
    lax.fori_loop(0, DISPATCH_ROWS, wait, 0, unroll=8)


def _dispatch(h2, pos0, pos1, p_pad):
    n, d = h2.shape
    grid_spec = pltpu.PrefetchScalarGridSpec(
        num_scalar_prefetch=2,
        grid=(n // DISPATCH_ROWS,),
        in_specs=[pl.BlockSpec((DISPATCH_ROWS, d), lambda i, p0, p1: (i, 0)), pl.BlockSpec(memory_space=pl.ANY)],
        out_specs=pl.BlockSpec(memory_space=pl.ANY),
        scratch_shapes=[pltpu.SemaphoreType.DMA],
    )
    return pl.pallas_call(
        _dispatch_kernel,
        grid_spec=grid_spec,
        out_shape=jax.ShapeDtypeStruct((p_pad, d), F32),
        input_output_aliases={3: 0},
        compiler_params=_cparams(("arbitrary",)),
        name="moe_dispatch",
    )(pos0, pos1, h2, jnp.zeros((p_pad, d), F32))


def _expert_ffn_kernel(te_ref, nr_ref, nu_ref, hs_ref, w1g_ref, w1u_ref, w2_ref, ys_ref, h_s, acc_s):
    t = pl.program_id(0)
    f = pl.program_id(1)
    tm = hs_ref.shape[0]
    used = t < nu_ref[0]
    full = nr_ref[t] > tm // 2

    def run(rows):
        @pl.when(f == 0)
        def _():
            h_s[0:rows, :] = hs_ref[0:rows, :].astype(BF16)
            acc_s[0:rows, :] = jnp.zeros((rows, acc_s.shape[1]), F32)

        h = h_s[0:rows, :]
        a = _silu(_dot(h, w1g_ref[0].astype(BF16))) * _dot(h, w1u_ref[0].astype(BF16))
        acc_s[0:rows, :] += _dot(a.astype(BF16), w2_ref[0].astype(BF16))

        @pl.when(f == pl.num_programs(1) - 1)
        def _():
            ys_ref[0:rows, :] = acc_s[0:rows, :]
            if rows < tm:
                ys_ref[rows:, :] = jnp.zeros((tm - rows, ys_ref.shape[1]), F32)

    pl.when(used & full)(lambda: run(tm))
    pl.when(used & jnp.logical_not(full))(lambda: run(tm // 2))

    @pl.when(jnp.logical_not(used) & (f == 0))
    def _():
        ys_ref[...] = jnp.zeros_like(ys_ref)


def _expert_ffn(hs, tile_e, tile_rows, n_used, w1, w2):
    p_pad, d = hs.shape
    ff = w2.shape[1]
    tm, tf = MOE_TM, FFN_TF
    nf = ff // tf
    nt = p_pad // tm

    def tile(t, nu):
        return jnp.maximum(jnp.minimum(t, nu[0] - 1), 0)

    def ftile(t, f, nu):
        return jnp.where(t < nu[0], f, nf - 1)

    grid_spec = pltpu.PrefetchScalarGridSpec(
        num_scalar_prefetch=3,
        grid=(nt, nf),
        in_specs=[
            pl.BlockSpec((tm, d), lambda t, f, te, nr, nu: (tile(t, nu), 0)),
            pl.BlockSpec((1, d, tf), lambda t, f, te, nr, nu: (te[tile(t, nu)], 0, ftile(t, f, nu))),
            pl.BlockSpec((1, d, tf), lambda t, f, te, nr, nu: (te[tile(t, nu)], 0, nf + ftile(t, f, nu))),
            pl.BlockSpec((1, tf, d), lambda t, f, te, nr, nu: (te[tile(t, nu)], ftile(t, f, nu), 0)),
        ],
        out_specs=pl.BlockSpec((tm, d), lambda t, f, te, nr, nu: (t, 0)),
        scratch_shapes=[pltpu.VMEM((tm, d), BF16), pltpu.VMEM((tm, d), F32)],
    )
    return pl.pallas_call(
        _expert_ffn_kernel,
        grid_spec=grid_spec,
        out_shape=jax.ShapeDtypeStruct((p_pad, d), F32),
        compiler_params=_cparams(("arbitrary", "arbitrary")),
        name="moe_expert_ffn",
    )(tile_e, tile_rows, n_used, hs, w1, w1, w2)


def _combine_kernel(pos0_ref, pos1_ref, ys_ref, x_ref, info_ref, gate_ref, g_ref, b_ref, o_ref, y0_s, y1_s, sem):
    tm = y0_s.shape[0]
    base = (pl.program_id(0) * pl.num_programs(1) + pl.program_id(1)) * tm

    def row_copy(r, pos_ref, dst):
        return pltpu.make_async_copy(ys_ref.at[pl.ds(pos_ref[base + r], 1), :], dst.at[pl.ds(r, 1), :], sem)

    def start(r, carry):
        row_copy(r, pos0_ref, y0_s).start(priority=0)
        row_copy(r, pos1_ref, y1_s).start(priority=1)
        return carry

    def wait(r, carry):
        row_copy(r, pos0_ref, y0_s).wait()
        row_copy(r, pos1_ref, y1_s).wait()
        return carry

    lax.fori_loop(0, tm, start, 0, unroll=8)
    lax.fori_loop(0, tm, wait, 0, unroll=8)
    info = info_ref[0]
    g0 = info[:, ROUTE_G0:ROUTE_G0 + 1]
    g1 = info[:, ROUTE_G1:ROUTE_G1 + 1]
    f = g0 * y0_s[...] + g1 * y1_s[...]
    z = DN_ALPHA * x_ref[0] + gate_ref[0] * f
    o_ref[0] = _layernorm_rows(z, g_ref[...], b_ref[...])


def _combine_ln(ys, pos0, pos1, x, info, gate, ln_g, ln_b):
    b, t, d = x.shape
    tm = min(t, COMBINE_ROWS)
    grid_spec = pltpu.PrefetchScalarGridSpec(
        num_scalar_prefetch=2,
        grid=(b, t // tm),
        in_specs=[
            pl.BlockSpec(memory_space=pl.ANY),
            pl.BlockSpec((1, tm, d), lambda bi, i, p0, p1: (bi, i, 0)),
            pl.BlockSpec((1, tm, LANES), lambda bi, i, p0, p1: (bi, i, 0)),
            pl.BlockSpec((1, 1, d), lambda bi, i, p0, p1: (bi, 0, 0)),
            pl.BlockSpec((1, d), lambda bi, i, p0, p1: (0, 0)),
            pl.BlockSpec((1, d), lambda bi, i, p0, p1: (0, 0)),
        ],
        out_specs=pl.BlockSpec((1, tm, d), lambda bi, i, p0, p1: (bi, i, 0)),
        scratch_shapes=[pltpu.VMEM((tm, d), F32), pltpu.VMEM((tm, d), F32), pltpu.SemaphoreType.DMA],
    )
    return pl.pallas_call(
        _combine_kernel,
        grid_spec=grid_spec,
        out_shape=jax.ShapeDtypeStruct((b, t, d), F32),
        compiler_params=_cparams(("arbitrary", "arbitrary")),
        name="moe_combine_ln",
    )(pos0, pos1, ys, x, info, gate, ln_g.reshape(1, d), ln_b.reshape(1, d))


def _moe_ln(x, scale, shift, w_router, w1, w2, gate, ln_g, ln_b):
    b, t, d = x.shape
    n = b * t
    tm = MOE_TM
    h, info, counts = _router(x, scale, shift, w_router)
    cnt = counts[0, :N_EXPERTS].astype(jnp.int32)
    padded = ((cnt + tm - 1) // tm) * tm
    ends = jnp.cumsum(padded)
    starts = ends - padded
    info2 = info.reshape(n, LANES)
    e0 = info2[:, ROUTE_E0].astype(jnp.int32)
    e1 = info2[:, ROUTE_E1].astype(jnp.int32)
    pos0 = starts[e0] + info2[:, ROUTE_R0].astype(jnp.int32)
    pos1 = starts[e1] + info2[:, ROUTE_R1].astype(jnp.int32)
    n_tiles = (TOP_K * n) // tm + N_EXPERTS
    tile_start = jnp.arange(n_tiles, dtype=jnp.int32) * tm
    tile_e = jnp.minimum(jnp.sum(tile_start[:, None] >= ends[None, :], axis=1), N_EXPERTS - 1).astype(jnp.int32)
    tile_rows = jnp.clip(starts[tile_e] + cnt[tile_e] - tile_start, 0, tm).astype(jnp.int32)
    n_used = (ends[-1] // tm).astype(jnp.int32).reshape(1)
    hs = _dispatch(h.reshape(n, d), pos0, pos1, n_tiles * tm)
    ys = _expert_ffn(hs, tile_e, tile_rows, n_used, w1, w2)
    return _combine_ln(ys, pos0, pos1, x, info, gate, ln_g, ln_b)


def _pair_perm(comp_offsets):
    even = np.concatenate([off + np.arange(0, ROT_DIM, 2) for off in comp_offsets])
    odd = np.concatenate([off + np.arange(1, ROT_DIM, 2) for off in comp_offsets])
    return np.concatenate([even, odd])


def _even_w_in(w_in):
    d = w_in.shape[0]
    a_qk = DA_HEADS * 2 * DA_DH
    head_perm = _pair_perm((0, DA_DH))
    qk_perm = np.concatenate([hh * 2 * DA_DH + head_perm for hh in range(DA_HEADS)])
    cols = np.concatenate([qk_perm, a_qk + qk_perm, np.arange(2 * a_qk, w_in.shape[1])])
    w = w_in[:, cols]
    return jnp.pad(w, ((0, 0), (0, EV_NPAD - w.shape[1]))).astype(BF16)


def _odd_w_in(w_in):
    o_ckv = MLA_Q_RANK
    o_kr = o_ckv + MLA_KV_RANK
    o_rest = o_kr + MLA_ROPE
    ev = o_kr + np.arange(0, MLA_ROPE, 2)
    od = o_kr + np.arange(1, MLA_ROPE, 2)
    cols = np.concatenate([np.arange(0, o_kr), ev, ev, od, od, np.arange(o_rest, w_in.shape[1])])
    return w_in[:, cols].astype(BF16)


def _mla_weights(w_uq, w_ukv):
    hq = MLA_NOPE + MLA_ROPE
    q3 = w_uq.reshape(MLA_Q_RANK, MLA_HEADS, hq).transpose(1, 0, 2)
    zeros = jnp.zeros((MLA_HEADS, MLA_Q_RANK, MLA_ROPE // 2), w_uq.dtype)
    rope = q3[:, :, MLA_NOPE:]
    wq = jnp.concatenate([q3[:, :, :MLA_NOPE], rope[:, :, 0::2], zeros, rope[:, :, 1::2], zeros], axis=-1)
    wkv = w_ukv.reshape(MLA_KV_RANK, MLA_HEADS, MLA_NOPE + MLA_DV).transpose(1, 0, 2)
    return wq.astype(BF16), wkv.astype(BF16)


def _gla_gate_weights(gk_w2, gk_b):
    pairs = GLA_HEADS // 2
    w = jnp.zeros((pairs, 2, LANES, LANES), F32)
    for d in range(2):
        blk = gk_w2[d].reshape(GLA_LR, pairs, LANES).transpose(1, 0, 2)
        w = w.at[:, d, d * GLA_LR:(d + 1) * GLA_LR, :].set(blk)
    bias = gk_b.reshape(2, pairs, 1, LANES).transpose(1, 0, 2, 3).astype(F32)
    return w.astype(BF16), bias


def _rope_tables(rows):
    n_freq = ROT_DIM // 4
    inv = ROPE_BASE ** (-jnp.arange(n_freq, dtype=F32) / n_freq)
    row = jnp.repeat(jnp.arange(rows, dtype=F32), GRID_W)
    col = jnp.tile(jnp.arange(GRID_W, dtype=F32), rows)
    ang = jnp.concatenate([row[:, None] * inv, col[:, None] * inv], axis=-1)
    cos, sin = jnp.cos(ang), jnp.sin(ang)
    return jnp.concatenate([cos] * 4, axis=-1), jnp.concatenate([-sin, -sin, sin, sin], axis=-1)


def _diff_lambda_init(layer):
    return 0.8 - 0.6 * math.exp(-0.3 * layer)


def kernel(x, c, ctx, c_ctx, ada_w, ada_b, post_ln_g, post_ln_b, lb_table, ev_w_in, ev_lam, ev_subln_g, ev_gk_w2,
           ev_gk_b, ev_gla_norm_g, ev_w_out, ev_ffn_w1, ev_ffn_w2, od_w_in, od_q_norm_g, od_kv_norm_g, od_w_uq,
           od_w_ukv, od_hg_norm_g, od_w_out, od_router, od_exp_w1, od_exp_w2):
    b, t, d = x.shape
    tc = ctx.shape[1]
    rope_c, rope_s = _rope_tables(t // GRID_W)
    lb_soft = jax.nn.softmax(lb_table.astype(F32), axis=0)
    lower_bounds = jnp.cumsum(lb_soft, axis=0) - lb_soft[0]

    n_cond = ((b + 1 + 7) // 8) * 8
    cond = jnp.zeros((n_cond, d), F32).at[:b].set(c).at[b].set(c_ctx)
    mods = _ada(cond, ada_w, ada_b).reshape(DEPTH, n_cond, 6, d)

    ctx_flat = None
    for layer in range(DEPTH):
        last = layer == DEPTH - 1
        j = layer // 2
        m_l = [mods[layer, :b, i][:, None, :] for i in range(6)]
        m_c = [jnp.broadcast_to(mods[layer, b, i][None, None, :], (b, 1, d)) for i in range(6)]
        m_c1 = [m[:1] for m in m_c]
        even = layer % 2 == 0
        w_in = _even_w_in(ev_w_in[j]) if even else _odd_w_in(od_w_in[j])
        p_l = _proj(x, 1.0 + m_l[1], m_l[0], w_in)
        p_c = _proj(ctx, 1.0 + m_c[1], m_c[0], w_in)
        need_ctx = not last
        if even:
            lam_init = _diff_lambda_init(layer)
            lv = ev_lam[j].astype(F32)
            lam = jnp.exp(jnp.sum(lv[0] * lv[1])) - jnp.exp(jnp.sum(lv[2] * lv[3])) + lam_init
            oa_c, oa_l = _diff_attention(p_l, p_c, rope_c, rope_s, lam, ev_subln_g[j], lam_init, need_ctx)
            wgk, bgk = _gla_gate_weights(ev_gk_w2[j], ev_gk_b[j])
            ob_c, ob_l = _gla_scan(p_c, p_l, wgk, bgk, ev_gla_norm_g[j], need_ctx)
            w_out = ev_w_out[j].astype(BF16)
        else:
            wq, wkv = _mla_weights(od_w_uq[j], od_w_ukv[j])
            oa_l = _mla_attention(p_l, p_c, rope_c, rope_s, od_q_norm_g[j], od_kv_norm_g[j], wq, wkv)
            oa_c = _mla_attention_ctx(p_c, od_q_norm_g[j], od_kv_norm_g[j], wq, wkv) if need_ctx else None
            ob_c, ob_l = _hgrn_scan(p_c, p_l, lower_bounds[layer], od_hg_norm_g[j], need_ctx)
            w_out = od_w_out[j].astype(BF16)
        g0, b0 = post_ln_g[layer, 0], post_ln_b[layer, 0]
        g1, b1 = post_ln_g[layer, 1], post_ln_b[layer, 1]
        x = _outproj_ln(oa_l, ob_l, w_out, x, m_l[2], g0, b0)
        if even:
            w1, w2 = ev_ffn_w1[j], ev_ffn_w2[j]
            x = _ffn_ln(x, 1.0 + m_l[4], m_l[3], w1, w2, m_l[5], g1, b1)
        else:
            w1, w2 = od_exp_w1[j], od_exp_w2[j]
            x = _moe_ln(x, 1.0 + m_l[4], m_l[3], od_router[j], w1, w2, m_l[5], g1, b1)
        if need_ctx:
            ctx = _outproj_ln(oa_c, ob_c, w_out, ctx, m_c[2], g0, b0)
            ctx_flat = ctx.reshape(1, b * tc, d)
            if even:
                ctx_flat = _ffn_ln(ctx_flat, 1.0 + m_c1[4], m_c1[3], w1, w2, m_c1[5], g1, b1)
            else:
                ctx_flat = _moe_ln(ctx_flat, 1.0 + m_c1[4], m_c1[3], od_router[j], w1, w2, m_c1[5], g1, b1)
            ctx = ctx_flat.reshape(b, tc, d)
    return x
```

```python
import functools
import math

import jax
import jax.numpy as jnp
import numpy as np
from jax import lax
from jax.experimental import pallas as pl
from jax.experimental.pallas import tpu as pltpu

F32 = jnp.float32
BF16 = jnp.bfloat16

DEPTH = 2
GRID_W = 64
ROT_DIM = 64
ROPE_BASE = 10000.0
DA_HEADS = 4
DA_DH = ROT_DIM
DA_DV = 2 * DA_DH
GLA_HEADS = 4
GLA_DK = 64
GLA_DV = 128
GLA_LR = 16
GLA_NORMALIZER = 16.0
MLA_HEADS = 4
MLA_Q_RANK = 256
MLA_KV_RANK = 128
MLA_NOPE = 128
MLA_ROPE = ROT_DIM
MLA_DV = 128
MLA_SCALE = (MLA_NOPE + MLA_ROPE) ** -0.5
HG_HEADS = 4
HG_DK = 128
HG_DV = 128
D_FF = 3584
N_EXPERTS = 8
TOP_K = 2
LN_EPS = 1e-5
RMS_EPS = 1e-6
DN_ALPHA = (2 * DEPTH) ** 0.25

LANES = 128
VMEM_LIMIT = 56 * 1024 * 1024
ATTN_TQ = 256
SCAN_BLOCK = 256
SCAN_CHUNK = 32
MOE_TM = 1024
FFN_TF = 512

EV_QA, EV_KA, EV_VA, EV_QB, EV_KB, EV_VB, EV_GB, EV_LR = 0, 4, 8, 12, 14, 16, 20, 24
EV_NPAD = 25 * LANES
OD_CQ, OD_CKV, OD_KR, OD_HQ, OD_FF, OD_FB, OD_HI, OD_HG = 0, 2, 3, 4, 8, 12, 16, 20
OD_NPAD = 24 * LANES


def _cparams(sem, flags=None):
    return pltpu.CompilerParams(dimension_semantics=sem, vmem_limit_bytes=VMEM_LIMIT, flags=flags)


def _silu(v):
    return v * jax.nn.sigmoid(v)


def _layernorm_rows(z, g, b):
    mu = jnp.mean(z, axis=-1, keepdims=True)
    zc = z - mu
    var = jnp.mean(zc * zc, axis=-1, keepdims=True)
    return zc * lax.rsqrt(var + LN_EPS) * g + b


def _rmsnorm_rows(v, g):
    return v * lax.rsqrt(jnp.mean(v * v, axis=-1, keepdims=True) + RMS_EPS) * g


def _dot_nt(a, b):
    return lax.dot_general(a, b, (((1,), (1,)), ((), ())), preferred_element_type=F32)


def _dot_tn(a, b):
    return lax.dot_general(a, b, (((0,), (0,)), ((), ())), preferred_element_type=F32)


def _dot(a, b):
    return jnp.dot(a, b, preferred_element_type=F32)


def _ada_kernel(c_ref, w_ref, b_ref, o_ref):
    s = _silu(c_ref[...]).astype(BF16)
    o_ref[0] = _dot(s, w_ref[0].astype(BF16)) + b_ref[0]


def _ada(cond, ada_w, ada_b):
    depth, d, n = ada_w.shape
    r = cond.shape[0]
    tn = n // 4
    return pl.pallas_call(
        _ada_kernel,
        grid=(depth, n // tn),
        in_specs=[
            pl.BlockSpec((r, d), lambda l, j: (0, 0)),
            pl.BlockSpec((1, d, tn), lambda l, j: (l, 0, j)),
            pl.BlockSpec((1, 1, tn), lambda l, j: (l, 0, j)),
        ],
        out_specs=pl.BlockSpec((1, r, tn), lambda l, j: (l, 0, j)),
        out_shape=jax.ShapeDtypeStruct((depth, r, n), F32),
        compiler_params=_cparams(("parallel", "parallel")),
        name="ada_modulation",
    )(cond, ada_w, ada_b.reshape(depth, 1, n))


def _proj_kernel(x_ref, sc_ref, sh_ref, w_ref, o_ref):
    h = (x_ref[0] * sc_ref[0] + sh_ref[0]).astype(BF16)
    o_ref[0] = _dot(h, w_ref[...]).astype(o_ref.dtype)


def _proj(x, scale, shift, w):
    b, t, d = x.shape
    n = w.shape[1]
    tm = min(t, 512)
    return pl.pallas_call(
        _proj_kernel,
        grid=(b, t // tm),
        in_specs=[
            pl.BlockSpec((1, tm, d), lambda bi, i: (bi, i, 0)),
            pl.BlockSpec((1, 1, d), lambda bi, i: (bi, 0, 0)),
            pl.BlockSpec((1, 1, d), lambda bi, i: (bi, 0, 0)),
            pl.BlockSpec((d, n), lambda bi, i: (0, 0)),
        ],
        out_specs=pl.BlockSpec((1, tm, n), lambda bi, i: (bi, i, 0)),
        out_shape=jax.ShapeDtypeStruct((b, t, n), BF16),
        compiler_params=_cparams(("parallel", "parallel")),
        name="mod_proj",
    )(x, scale, shift, w)


def _rope128(t, cs, sn):
    t = t.astype(F32)
    return t * cs + pltpu.roll(t, LANES // 2, axis=1) * sn


LOG2E = math.log2(math.e)


def _softmax_pv(s2, v_bf):
    m = jnp.max(s2, axis=-1, keepdims=True)
    e = jnp.exp2(s2 - m)
    return _dot(e.astype(BF16), v_bf) / jnp.sum(e, axis=-1, keepdims=True)


def _q1_lane_mask(shape):
    lane = lax.broadcasted_iota(jnp.int32, shape, 1)
    return (lane // 32) % 2 == 0


def _diff_scores_out(q, k_bf, v_bf, lam):
    m1 = _q1_lane_mask(q.shape)
    q1 = jnp.where(m1, q, 0.0).astype(BF16)
    q2 = jnp.where(m1, 0.0, q).astype(BF16)
    s1 = _dot_nt(q1, k_bf)
    s2 = _dot_nt(q2, k_bf)
    e1 = jnp.exp2(s1 - jnp.max(s1, axis=-1, keepdims=True))
    e2 = jnp.exp2(s2 - jnp.max(s2, axis=-1, keepdims=True))
    r1 = 1.0 / jnp.sum(e1, axis=-1, keepdims=True)
    r2 = lam / jnp.sum(e2, axis=-1, keepdims=True)
    return _dot((e1 * r1 - e2 * r2).astype(BF16), v_bf)


def _diffattn_lat_kernel(lam_init, tc, q_ref, kl_ref, vl_ref, kc_ref, vc_ref, cq_ref, sq_ref, ck_ref, sk_ref,
                         lam_ref, g_ref, o_ref, k_s, v_s):
    @pl.when(pl.program_id(2) == 0)
    def _():
        k_s[0:tc, :] = kc_ref[0].astype(BF16)
        v_s[0:tc, :] = vc_ref[0].astype(BF16)
        k_s[tc:, :] = _rope128(kl_ref[0], ck_ref[...], sk_ref[...]).astype(BF16)
        v_s[tc:, :] = vl_ref[0].astype(BF16)

    q = _rope128(q_ref[0], cq_ref[...], sq_ref[...]) * (DA_DH ** -0.5 * LOG2E)
    o = _diff_scores_out(q, k_s[...], v_s[...], lam_ref[0, 0])
    o_ref[0] = (_rmsnorm_rows(o, g_ref[...]) * (1.0 - lam_init)).astype(o_ref.dtype)


def _diffattn_ctx_kernel(lam_init, q_ref, k_ref, v_ref, lam_ref, g_ref, o_ref):
    q = q_ref[0].astype(F32) * (DA_DH ** -0.5 * LOG2E)
    o = _diff_scores_out(q, k_ref[0].astype(BF16), v_ref[0].astype(BF16), lam_ref[0, 0])
    o_ref[0] = (_rmsnorm_rows(o, g_ref[...]) * (1.0 - lam_init)).astype(o_ref.dtype)


def _diff_attention(p_l, p_c, rope_c, rope_s, lam, subln_g, lam_init, need_ctx):
    b, tl, _ = p_l.shape
    tc = p_c.shape[1]
    h = DA_HEADS
    tq = min(tl, ATTN_TQ)
    lam2 = lam.reshape(1, 1).astype(F32)
    g2 = subln_g.reshape(1, DA_DV).astype(F32)
    smem = pl.BlockSpec(memory_space=pltpu.SMEM)
    o_l = pl.pallas_call(
        functools.partial(_diffattn_lat_kernel, lam_init, tc),
        grid=(b, h, tl // tq),
        in_specs=[
            pl.BlockSpec((1, tq, LANES), lambda bi, hi, i: (bi, i, EV_QA + hi)),
            pl.BlockSpec((1, tl, LANES), lambda bi, hi, i: (bi, 0, EV_KA + hi)),
            pl.BlockSpec((1, tl, LANES), lambda bi, hi, i: (bi, 0, EV_VA + hi)),
            pl.BlockSpec((1, tc, LANES), lambda bi, hi, i: (bi, 0, EV_KA + hi)),
            pl.BlockSpec((1, tc, LANES), lambda bi, hi, i: (bi, 0, EV_VA + hi)),
            pl.BlockSpec((tq, LANES), lambda bi, hi, i: (i, 0)),
            pl.BlockSpec((tq, LANES), lambda bi, hi, i: (i, 0)),
            pl.BlockSpec((tl, LANES), lambda bi, hi, i: (0, 0)),
            pl.BlockSpec((tl, LANES), lambda bi, hi, i: (0, 0)),
            smem,
            pl.BlockSpec((1, LANES), lambda bi, hi, i: (0, 0)),
        ],
        out_specs=pl.BlockSpec((1, tq, LANES), lambda bi, hi, i: (bi, i, hi)),
        out_shape=jax.ShapeDtypeStruct((b, tl, h * DA_DV), BF16),
        scratch_shapes=[pltpu.VMEM((tc + tl, LANES), BF16), pltpu.VMEM((tc + tl, LANES), BF16)],
        compiler_params=_cparams(("parallel", "parallel", "arbitrary")),
        name="diff_attention_latent",
    )(p_l, p_l, p_l, p_c, p_c, rope_c, rope_s, rope_c, rope_s, lam2, g2)
    if not need_ctx:
        return None, o_l
    o_c = pl.pallas_call(
        functools.partial(_diffattn_ctx_kernel, lam_init),
        grid=(b, h),
        in_specs=[
            pl.BlockSpec((1, tc, LANES), lambda bi, hi: (bi, 0, EV_QA + hi)),
            pl.BlockSpec((1, tc, LANES), lambda bi, hi: (bi, 0, EV_KA + hi)),
            pl.BlockSpec((1, tc, LANES), lambda bi, hi: (bi, 0, EV_VA + hi)),
            smem,
            pl.BlockSpec((1, LANES), lambda bi, hi: (0, 0)),
        ],
        out_specs=pl.BlockSpec((1, tc, LANES), lambda bi, hi: (bi, 0, hi)),
        out_shape=jax.ShapeDtypeStruct((b, tc, h * DA_DV), BF16),
        compiler_params=_cparams(("parallel", "parallel")),
        name="diff_attention_ctx",
    )(p_c, p_c, p_c, lam2, g2)
    return o_c, o_l


def _mla_q(cq, qg, wq, cs, sn):
    q = _dot(_rmsnorm_rows(cq.astype(F32), qg).astype(BF16), wq)
    if cs is not None:
        q = jnp.concatenate([q[:, :LANES], _rope128(q[:, LANES:], cs, sn)], axis=1)
    return (q * (MLA_SCALE * LOG2E)).astype(BF16)


def _mla_kv(ckv, kr, kvg, wkv, cs, sn):
    kv = _dot(_rmsnorm_rows(ckv.astype(F32), kvg).astype(BF16), wkv)
    if cs is not None:
        kr = _rope128(kr, cs, sn)
    k = jnp.concatenate([kv[:, :LANES].astype(BF16), kr.astype(BF16)], axis=1)
    return k, kv[:, LANES:].astype(BF16)


def _mla_lat_kernel(tc, cq_ref, ckvl_ref, krl_ref, ckvc_ref, krc_ref, cq_c_ref, cq_s_ref, ck_ref, sk_ref,
                    qg_ref, kvg_ref, wq_ref, wkv_ref, o_ref, k_s, v_s):
    @pl.when(pl.program_id(2) == 0)
    def _():
        kc, vc = _mla_kv(ckvc_ref[0], krc_ref[0], kvg_ref[...], wkv_ref[0], None, None)
        k_s[0:tc, :] = kc
        v_s[0:tc, :] = vc
        kl, vl = _mla_kv(ckvl_ref[0], krl_ref[0], kvg_ref[...], wkv_ref[0], ck_ref[...], sk_ref[...])
        k_s[tc:, :] = kl
        v_s[tc:, :] = vl

    q = _mla_q(cq_ref[0], qg_ref[...], wq_ref[0], cq_c_ref[...], cq_s_ref[...])
    o_ref[0] = _softmax_pv(_dot_nt(q, k_s[...]), v_s[...]).astype(o_ref.dtype)


def _mla_attention(p_l, p_c, rope_c, rope_s, q_norm_g, kv_norm_g, wq, wkv):
    b, tl, _ = p_l.shape
    tc = p_c.shape[1]
    h = MLA_HEADS
    tq = min(tl, ATTN_TQ)
    return pl.pallas_call(
        functools.partial(_mla_lat_kernel, tc),
        grid=(b, h, tl // tq),
        in_specs=[
            pl.BlockSpec((1, tq, MLA_Q_RANK), lambda bi, hi, i: (bi, i, OD_CQ)),
            pl.BlockSpec((1, tl, LANES), lambda bi, hi, i: (bi, 0, OD_CKV)),
            pl.BlockSpec((1, tl, LANES), lambda bi, hi, i: (bi, 0, OD_KR)),
            pl.BlockSpec((1, tc, LANES), lambda bi, hi, i: (bi, 0, OD_CKV)),
            pl.BlockSpec((1, tc, LANES), lambda bi, hi, i: (bi, 0, OD_KR)),
            pl.BlockSpec((tq, LANES), lambda bi, hi, i: (i, 0)),
            pl.BlockSpec((tq, LANES), lambda bi, hi, i: (i, 0)),
            pl.BlockSpec((tl, LANES), lambda bi, hi, i: (0, 0)),
            pl.BlockSpec((tl, LANES), lambda bi, hi, i: (0, 0)),
            pl.BlockSpec((1, MLA_Q_RANK), lambda bi, hi, i: (0, 0)),
            pl.BlockSpec((1, MLA_KV_RANK), lambda bi, hi, i: (0, 0)),
            pl.BlockSpec((1, MLA_Q_RANK, 2 * LANES), lambda bi, hi, i: (hi, 0, 0)),
            pl.BlockSpec((1, MLA_KV_RANK, 2 * LANES), lambda bi, hi, i: (hi, 0, 0)),
        ],
        out_specs=pl.BlockSpec((1, tq, LANES), lambda bi, hi, i: (bi, i, hi)),
        out_shape=jax.ShapeDtypeStruct((b, tl, h * MLA_DV), BF16),
        scratch_shapes=[pltpu.VMEM((tc + tl, 2 * LANES), BF16), pltpu.VMEM((tc + tl, LANES), BF16)],
        compiler_params=_cparams(("parallel", "parallel", "arbitrary")),
        name="mla_attention_latent",
    )(p_l, p_l, p_l, p_c, p_c, rope_c, rope_s, rope_c, rope_s,
      q_norm_g.reshape(1, -1), kv_norm_g.reshape(1, -1), wq, wkv)


def _mla_ctx_kernel(cq_ref, ckv_ref, kr_ref, qg_ref, kvg_ref, wq_ref, wkv_ref, o_ref):
    k, v = _mla_kv(ckv_ref[0], kr_ref[0], kvg_ref[...], wkv_ref[0], None, None)
    q = _mla_q(cq_ref[0], qg_ref[...], wq_ref[0], None, None)
    o_ref[0] = _softmax_pv(_dot_nt(q, k), v).astype(o_ref.dtype)


def _mla_attention_ctx(p_c, q_norm_g, kv_norm_g, wq, wkv):
    b, tc, _ = p_c.shape
    h = MLA_HEADS
    return pl.pallas_call(
        _mla_ctx_kernel,
        grid=(b, h),
        in_specs=[
            pl.BlockSpec((1, tc, MLA_Q_RANK), lambda bi, hi: (bi, 0, OD_CQ)),
            pl.BlockSpec((1, tc, LANES), lambda bi, hi: (bi, 0, OD_CKV)),
            pl.BlockSpec((1, tc, LANES), lambda bi, hi: (bi, 0, OD_KR)),
            pl.BlockSpec((1, MLA_Q_RANK), lambda bi, hi: (0, 0)),
            pl.BlockSpec((1, MLA_KV_RANK), lambda bi, hi: (0, 0)),
            pl.BlockSpec((1, MLA_Q_RANK, 2 * LANES), lambda bi, hi: (hi, 0, 0)),
            pl.BlockSpec((1, MLA_KV_RANK, 2 * LANES), lambda bi, hi: (hi, 0, 0)),
        ],
        out_specs=pl.BlockSpec((1, tc, LANES), lambda bi, hi: (bi, 0, hi)),
        out_shape=jax.ShapeDtypeStruct((b, tc, h * MLA_DV), BF16),
        compiler_params=_cparams(("parallel", "parallel")),
        name="mla_attention_ctx",
    )(p_c, p_c, p_c, q_norm_g.reshape(1, -1), kv_norm_g.reshape(1, -1), wq, wkv)


def _scan_levels(bt):
    levels = []
    c = SCAN_CHUNK
    while c <= bt:
        levels.append(c)
        c *= 2
    return levels


def _level_table(bt, reverse):
    i = np.arange(bt)[:, None]
    j = np.arange(bt)[None, :]
    if reverse:
        i, j = j, i
    tab = np.zeros((bt, bt), np.int32)
    for lvl, c in enumerate(_scan_levels(bt), start=1):
        same = (i // c) == (j // c)
        if lvl == 1:
            m = same & (j <= i)
        else:
            m = same & ((i % c) >= c // 2) & ((j % c) < c // 2)
        tab[m] = lvl
    return tab


def _chunk_row(a, c, r):
    bt, n = a.shape
    a3 = a.reshape(bt // c, c, n)
    return jnp.broadcast_to(a3[:, r:r + 1, :], (bt // c, c, n)).reshape(bt, n)


def _scan_block(q_s, k_s, g_s, v_s, o_s, tri_ref, lvl_ref, start, st, reverse, compute_out):
    bt = SCAN_BLOCK
    rows = pl.ds(start, bt)
    g = g_s[rows, :]
    k = k_s[rows, :]
    v = v_s[rows, :]
    g_hi = g.astype(BF16)
    g_lo = (g - g_hi.astype(F32)).astype(BF16)
    tri = tri_ref[...]
    gc = _dot(tri, g_hi) + _dot(tri, g_lo)
    g_tot = gc[0:1, :] if reverse else gc[bt - 1:bt, :]
    kd = (k * jnp.exp(g_tot - gc)).astype(BF16)
    st_new = st * jnp.exp(g_tot) + _dot_tn(v, kd)
    if compute_out:
        q = q_s[rows, :]
        o = _dot_nt((q * jnp.exp(gc)).astype(BF16), st.astype(BF16))
        lvl = lvl_ref[...]
        att = jnp.zeros((bt, bt), F32)
        for li, c in enumerate(_scan_levels(bt), start=1):
            r = _chunk_row(gc, c, c // 2 if reverse else c // 2 - 1)
            a = _dot_nt((q * jnp.exp(gc - r)).astype(BF16), (k * jnp.exp(r - gc)).astype(BF16))
            att = jnp.where(lvl == li, a, att)
        o_s[rows, :] = o + _dot(att.astype(BF16), v)
    return st_new


def _scan_all(chains, tri_refs, lvl_refs, tc, tl, need_ctx):
    bt = SCAN_BLOCK
    nc, nl = tc // bt, tl // bt

    def run(first, n, carry, compute_out):
        def body(i, carry):
            new = []
            for ci, ch in enumerate(chains):
                for d in range(2):
                    blk = first + (n - 1 - i if d else i)
                    start = pl.multiple_of(blk * bt, bt)
                    new.append(_scan_block(ch["q"], ch["k"][d], ch["g"][d], ch["v"], ch["o"][d], tri_refs[d],
                                           lvl_refs[d], start, carry[2 * ci + d], d == 1, compute_out))
            return tuple(new)
        return lax.fori_loop(0, n, body, carry)

    zero = jnp.zeros((LANES, LANES), F32)
    carry = run(0, nc, (zero,) * (2 * len(chains)), need_ctx)
    run(nc, nl, carry, True)


def _scan_finish(chain, cols, gate_c_ref, gate_l_ref, ng_ref, oc_ref, ol_ref, tc, need_ctx):
    ng = ng_ref[...]
    of_s, ob_s = chain["o"]
    if need_ctx:
        o = of_s[0:tc, :] + ob_s[0:tc, :]
        oc_ref[0, :, cols] = (_rmsnorm_rows(o, ng) * _silu(gate_c_ref[0, :, cols].astype(F32))).astype(oc_ref.dtype)
    o = of_s[tc:, :] + ob_s[tc:, :]
    ol_ref[0, :, cols] = (_rmsnorm_rows(o, ng) * _silu(gate_l_ref[0, :, cols].astype(F32))).astype(ol_ref.dtype)


SCAN_HEADS = 2
SCAN_SCRATCH = 8


def _scan_unpack(refs, n_in, need_ctx):
    ins = refs[:n_in]
    if need_ctx:
        oc_ref, ol_ref = refs[n_in:n_in + 2]
        scratch = refs[n_in + 2:]
    else:
        oc_ref, ol_ref = None, refs[n_in]
        scratch = refs[n_in + 1:]
    chains = []
    for hh in range(SCAN_HEADS):
        q_s, v_s, kf_s, kb_s, gf_s, gb_s, of_s, ob_s = scratch[SCAN_SCRATCH * hh:SCAN_SCRATCH * (hh + 1)]
        chains.append({"q": q_s, "v": v_s, "k": (kf_s, kb_s), "g": (gf_s, gb_s), "o": (of_s, ob_s)})
    return ins, oc_ref, ol_ref, chains


def _gla_scan_kernel(tc, tl, need_ctx, *refs):
    ins, oc_ref, ol_ref, chains = _scan_unpack(refs, 17, need_ctx)
    (qc_ref, kc_ref, vc_ref, gbc_ref, lrc_ref, ql_ref, kl_ref, vl_ref, gbl_ref, lrl_ref,
     wgk_ref, bgk_ref, ng_ref, trif_ref, trib_ref, lvlf_ref, lvlb_ref) = ins
    lane = lax.broadcasted_iota(jnp.int32, (1, LANES), 1)
    for hh, ch in enumerate(chains):
        cols = slice(hh * LANES, (hh + 1) * LANES)
        mine = (lane // GLA_DK) == hh
        q_s, v_s, (k_s, _) = ch["q"], ch["v"], ch["k"]
        ch["k"] = (k_s, k_s)
        for (q_ref, k_ref, v_ref), lo, hi in (((qc_ref, kc_ref, vc_ref), 0, tc),
                                              ((ql_ref, kl_ref, vl_ref), tc, tc + tl)):
            q_s[lo:hi, :] = jnp.where(mine, q_ref[0].astype(F32), 0.0) * (GLA_DK ** -0.5)
            k_s[lo:hi, :] = jnp.where(mine, k_ref[0].astype(F32), 0.0)
            v_s[lo:hi, :] = v_ref[0, :, cols]
    for d in range(2):
        for lr_ref, lo, hi in ((lrc_ref, 0, tc), (lrl_ref, tc, tc + tl)):
            g_pair = jax.nn.log_sigmoid(_dot(lr_ref[0], wgk_ref[0, d]) + bgk_ref[0, d]) / GLA_NORMALIZER
            for hh, ch in enumerate(chains):
                ch["g"][d][lo:hi, :] = jnp.where((lane // GLA_DK) == hh, g_pair, 0.0)
    _scan_all(chains, (trif_ref, trib_ref), (lvlf_ref, lvlb_ref), tc, tl, need_ctx)
    for hh, ch in enumerate(chains):
        _scan_finish(ch, slice(hh * LANES, (hh + 1) * LANES), gbc_ref, gbl_ref, ng_ref, oc_ref, ol_ref, tc, need_ctx)


def _hgrn_scan_kernel(tc, tl, need_ctx, *refs):
    ins, oc_ref, ol_ref, chains = _scan_unpack(refs, 16, need_ctx)
    (qc_ref, ffc_ref, fbc_ref, vc_ref, gtc_ref, ql_ref, ffl_ref, fbl_ref, vl_ref, gtl_ref,
     lb_ref, ng_ref, trif_ref, trib_ref, lvlf_ref, lvlb_ref) = ins
    for hh, ch in enumerate(chains):
        cols = slice(hh * LANES, (hh + 1) * LANES)
        lb = lb_ref[:, cols]
        for (q_ref, v_ref), lo, hi in (((qc_ref, vc_ref), 0, tc), ((ql_ref, vl_ref), tc, tc + tl)):
            ch["q"][lo:hi, :] = q_ref[0, :, cols].astype(F32)
            ch["v"][lo:hi, :] = v_ref[0, :, cols]
        for (fc_ref, fl_ref), k_s, g_s in (((ffc_ref, ffl_ref), ch["k"][0], ch["g"][0]),
                                           ((fbc_ref, fbl_ref), ch["k"][1], ch["g"][1])):
            for f_ref, lo, hi in ((fc_ref, 0, tc), (fl_ref, tc, tc + tl)):
                f = lb + (1.0 - lb) * jax.nn.sigmoid(f_ref[0, :, cols].astype(F32))
                k_s[lo:hi, :] = 1.0 - f
                g_s[lo:hi, :] = jnp.log(f)
    _scan_all(chains, (trif_ref, trib_ref), (lvlf_ref, lvlb_ref), tc, tl, need_ctx)
    for hh, ch in enumerate(chains):
        _scan_finish(ch, slice(hh * LANES, (hh + 1) * LANES), gtc_ref, gtl_ref, ng_ref, oc_ref, ol_ref, tc, need_ctx)


def _scan_consts():
    bt = SCAN_BLOCK
    lower = np.tril(np.ones((bt, bt), np.float32))
    return (jnp.asarray(lower, BF16), jnp.asarray(lower.T, BF16),
            jnp.asarray(_level_table(bt, False)), jnp.asarray(_level_table(bt, True)))


def _scan_call(kernel_fn, name, p_c, p_l, col_specs, extra, extra_specs, need_ctx, heads):
    b, tl, _ = p_l.shape
    tc = p_c.shape[1]
    bt = SCAN_BLOCK
    t = tc + tl
    wide = SCAN_HEADS * LANES
    consts = _scan_consts()
    const_specs = [pl.BlockSpec((bt, bt), lambda bi, ji: (0, 0)) for _ in consts]
    in_specs = ([pl.BlockSpec((1, tc, g * LANES), f) for g, f in col_specs]
                + [pl.BlockSpec((1, tl, g * LANES), f) for g, f in col_specs] + extra_specs + const_specs)
    args = [p_c] * len(col_specs) + [p_l] * len(col_specs) + list(extra) + list(consts)
    out_l = jax.ShapeDtypeStruct((b, tl, heads * LANES), BF16)
    spec_l = pl.BlockSpec((1, tl, wide), lambda bi, ji: (bi, 0, ji))
    if need_ctx:
        out_shape = (jax.ShapeDtypeStruct((b, tc, heads * LANES), BF16), out_l)
        out_specs = (pl.BlockSpec((1, tc, wide), lambda bi, ji: (bi, 0, ji)), spec_l)
    else:
        out_shape, out_specs = out_l, spec_l
    per_head = [pltpu.VMEM((t, LANES), F32), pltpu.VMEM((t, LANES), BF16)] + [pltpu.VMEM((t, LANES), F32)] * 6
    res = pl.pallas_call(
        functools.partial(kernel_fn, tc, tl, need_ctx),
        grid=(b, heads // SCAN_HEADS),
        in_specs=in_specs,
        out_specs=out_specs,
        out_shape=out_shape,
        scratch_shapes=per_head * SCAN_HEADS,
        compiler_params=_cparams(("parallel", "parallel")),
        name=name,
    )(*args)
    return res if need_ctx else (None, res)


def _gla_scan(p_c, p_l, wgk, bgk, norm_g, need_ctx):
    col_specs = [
        (1, lambda bi, ji: (bi, 0, EV_QB + ji)),
        (1, lambda bi, ji: (bi, 0, EV_KB + ji)),
        (SCAN_HEADS, lambda bi, ji: (bi, 0, EV_VB // SCAN_HEADS + ji)),
        (SCAN_HEADS, lambda bi, ji: (bi, 0, EV_GB // SCAN_HEADS + ji)),
        (1, lambda bi, ji: (bi, 0, EV_LR)),
    ]
    extra_specs = [
        pl.BlockSpec((1, 2, LANES, LANES), lambda bi, ji: (ji, 0, 0, 0)),
        pl.BlockSpec((1, 2, 1, LANES), lambda bi, ji: (ji, 0, 0, 0)),
        pl.BlockSpec((1, LANES), lambda bi, ji: (0, 0)),
    ]
    return _scan_call(_gla_scan_kernel, "gla_scan", p_c, p_l, col_specs,
                      (wgk, bgk, norm_g.reshape(1, LANES)), extra_specs, need_ctx, GLA_HEADS)


def _hgrn_scan(p_c, p_l, lb, norm_g, need_ctx):
    col_specs = [(SCAN_HEADS, functools.partial(lambda base, bi, ji: (bi, 0, base // SCAN_HEADS + ji), base))
                 for base in (OD_HQ, OD_FF, OD_FB, OD_HI, OD_HG)]
    extra_specs = [
        pl.BlockSpec((1, SCAN_HEADS * LANES), lambda bi, ji: (0, ji)),
        pl.BlockSpec((1, LANES), lambda bi, ji: (0, 0)),
    ]
    return _scan_call(_hgrn_scan_kernel, "hgrn2_scan", p_c, p_l, col_specs,
                      (lb.reshape(1, -1), norm_g.reshape(1, LANES)), extra_specs, need_ctx, HG_HEADS)


def _outproj_kernel(oa_ref, ob_ref, w_ref, x_ref, gate_ref, g_ref, b_ref, o_ref):
    ka = oa_ref.shape[2]
    y = _dot(oa_ref[0], w_ref[0:ka, :]) + _dot(ob_ref[0], w_ref[ka:, :])
    z = DN_ALPHA * x_ref[0] + gate_ref[0] * y
    o_ref[0] = _layernorm_rows(z, g_ref[...], b_ref[...])


def _outproj_ln(oa, ob, w, x, gate, ln_g, ln_b):
    b, t, d = x.shape
    ka, kb = oa.shape[2], ob.shape[2]
    tm = min(t, 512)
    return pl.pallas_call(
        _outproj_kernel,
        grid=(b, t // tm),
        in_specs=[
            pl.BlockSpec((1, tm, ka), lambda bi, i: (bi, i, 0)),
            pl.BlockSpec((1, tm, kb), lambda bi, i: (bi, i, 0)),
            pl.BlockSpec((ka + kb, d), lambda bi, i: (0, 0)),
            pl.BlockSpec((1, tm, d), lambda bi, i: (bi, i, 0)),
            pl.BlockSpec((1, 1, d), lambda bi, i: (bi, 0, 0)),
            pl.BlockSpec((1, d), lambda bi, i: (0, 0)),
            pl.BlockSpec((1, d), lambda bi, i: (0, 0)),
        ],
        out_specs=pl.BlockSpec((1, tm, d), lambda bi, i: (bi, i, 0)),
        out_shape=jax.ShapeDtypeStruct((b, t, d), F32),
        compiler_params=_cparams(("parallel", "parallel")),
        name="outproj_residual_ln",
    )(oa, ob, w, x, gate, ln_g.reshape(1, d), ln_b.reshape(1, d))


def _ffn_kernel(x_ref, sc_ref, sh_ref, w1g_ref, w1u_ref, w2_ref, gate_ref, g_ref, b_ref, o_ref, h_s, acc_s):
    f = pl.program_id(2)

    @pl.when(f == 0)
    def _():
        h_s[...] = (x_ref[0] * sc_ref[0] + sh_ref[0]).astype(BF16)
        acc_s[...] = jnp.zeros_like(acc_s)

    h = h_s[...]
    a = _silu(_dot(h, w1g_ref[...].astype(BF16))) * _dot(h, w1u_ref[...].astype(BF16))
    acc_s[...] += _dot(a.astype(BF16), w2_ref[...].astype(BF16))

    @pl.when(f == pl.num_programs(2) - 1)
    def _():
        z = DN_ALPHA * x_ref[0] + gate_ref[0] * acc_s[...]
        o_ref[0] = _layernorm_rows(z, g_ref[...], b_ref[...])


def _ffn_ln(x, scale, shift, w1, w2, gate, ln_g, ln_b):
    b, t, d = x.shape
    ff = w2.shape[0]
    tf = FFN_TF
    nf = ff // tf
    tm = min(t, 1024)
    return pl.pallas_call(
        _ffn_kernel,
        grid=(b, t // tm, nf),
        in_specs=[
            pl.BlockSpec((1, tm, d), lambda bi, i, f: (bi, i, 0)),
            pl.BlockSpec((1, 1, d), lambda bi, i, f: (bi, 0, 0)),
            pl.BlockSpec((1, 1, d), lambda bi, i, f: (bi, 0, 0)),
            pl.BlockSpec((d, tf), lambda bi, i, f: (0, f)),
            pl.BlockSpec((d, tf), lambda bi, i, f: (0, nf + f)),
            pl.BlockSpec((tf, d), lambda bi, i, f: (f, 0)),
            pl.BlockSpec((1, 1, d), lambda bi, i, f: (bi, 0, 0)),
            pl.BlockSpec((1, d), lambda bi, i, f: (0, 0)),
            pl.BlockSpec((1, d), lambda bi, i, f: (0, 0)),
        ],
        out_specs=pl.BlockSpec((1, tm, d), lambda bi, i, f: (bi, i, 0)),
        out_shape=jax.ShapeDtypeStruct((b, t, d), F32),
        scratch_shapes=[pltpu.VMEM((tm, d), BF16), pltpu.VMEM((tm, d), F32)],
        compiler_params=_cparams(("parallel", "parallel", "arbitrary")),
        name="swiglu_residual_ln",
    )(x, scale, shift, w1, w1, w2, gate, ln_g.reshape(1, d), ln_b.reshape(1, d))


ROUTE_E0, ROUTE_E1, ROUTE_G0, ROUTE_G1, ROUTE_R0, ROUTE_R1 = range(6)


def _router_kernel(x_ref, sc_ref, sh_ref, wr_ref, h_ref, info_ref, cnt_ref, carry_s):
    first = (pl.program_id(0) == 0) & (pl.program_id(1) == 0)

    @pl.when(first)
    def _():
        carry_s[...] = jnp.zeros_like(carry_s)

    h = x_ref[0] * sc_ref[0] + sh_ref[0]
    h_ref[0] = h
    tm = h.shape[0]
    logits = jnp.dot(h, wr_ref[...], preferred_element_type=F32, precision=lax.Precision.HIGHEST)
    lane = lax.broadcasted_iota(jnp.int32, (tm, LANES), 1).astype(F32)
    neg = jnp.float32(-jnp.inf)
    logits = jnp.where(lane < N_EXPERTS, logits, neg)
    v0 = jnp.max(logits, axis=-1, keepdims=True)
    e0 = jnp.min(jnp.where(logits == v0, lane, float(LANES)), axis=-1, keepdims=True)
    rest = jnp.where(lane == e0, neg, logits)
    v1 = jnp.max(rest, axis=-1, keepdims=True)
    e1 = jnp.min(jnp.where(rest == v1, lane, float(LANES)), axis=-1, keepdims=True)
    d = jnp.exp(v1 - v0)
    g0 = 1.0 / (1.0 + d)
    g1 = d / (1.0 + d)
    oh0 = (lane == e0).astype(BF16)
    oh1 = (lane == e1).astype(BF16)
    ri = lax.broadcasted_iota(jnp.int32, (tm, tm), 0)
    ci = lax.broadcasted_iota(jnp.int32, (tm, tm), 1)
    before = (ci < ri).astype(BF16)
    c0 = _dot(before, oh0)
    c1 = _dot(before, oh1)
    tot0 = jnp.sum(oh0.astype(F32), axis=0, keepdims=True)
    tot1 = jnp.sum(oh1.astype(F32), axis=0, keepdims=True)
    carry = carry_s[...]
    r0 = jnp.sum(jnp.where(lane == e0, carry + c0, 0.0), axis=-1, keepdims=True)
    r1 = jnp.sum(jnp.where(lane == e1, carry + tot0 + c1, 0.0), axis=-1, keepdims=True)
    carry = carry + tot0 + tot1
    carry_s[...] = carry
    cnt_ref[...] = carry
    info = jnp.zeros((tm, LANES), F32)
    for col, val in ((ROUTE_E0, e0), (ROUTE_E1, e1), (ROUTE_G0, g0), (ROUTE_G1, g1), (ROUTE_R0, r0), (ROUTE_R1, r1)):
        info = jnp.where(lane == col, val, info)
    info_ref[0] = info


def _router(x, scale, shift, w_router):
    b, t, d = x.shape
    tm = min(t, 512)
    wr = jnp.zeros((d, LANES), F32).at[:, :N_EXPERTS].set(w_router.astype(F32))
    return pl.pallas_call(
        _router_kernel,
        grid=(b, t // tm),
        in_specs=[
            pl.BlockSpec((1, tm, d), lambda bi, i: (bi, i, 0)),
            pl.BlockSpec((1, 1, d), lambda bi, i: (bi, 0, 0)),
            pl.BlockSpec((1, 1, d), lambda bi, i: (bi, 0, 0)),
            pl.BlockSpec((d, LANES), lambda bi, i: (0, 0)),
        ],
        out_specs=(
            pl.BlockSpec((1, tm, d), lambda bi, i: (bi, i, 0)),
            pl.BlockSpec((1, tm, LANES), lambda bi, i: (bi, i, 0)),
            pl.BlockSpec((1, LANES), lambda bi, i: (0, 0)),
        ),
        out_shape=(
            jax.ShapeDtypeStruct((b, t, d), F32),
            jax.ShapeDtypeStruct((b, t, LANES), F32),
            jax.ShapeDtypeStruct((1, LANES), F32),
        ),
        scratch_shapes=[pltpu.VMEM((1, LANES), F32)],
        compiler_params=_cparams(("arbitrary", "arbitrary")),
        name="moe_router",
    )(x, scale, shift, wr)


DISPATCH_ROWS = 1024
COMBINE_ROWS = 1024


def _dispatch_kernel(pos0_ref, pos1_ref, h_ref, zero_ref, hs_ref, sem):
    del zero_ref
    base = pl.program_id(0) * DISPATCH_ROWS

    def row_copy(r, pos_ref):
        return pltpu.make_async_copy(h_ref.at[pl.ds(r, 1), :], hs_ref.at[pl.ds(pos_ref[base + r], 1), :], sem)

    def start(r, carry):
        row_copy(r, pos0_ref).start(priority=0)
        row_copy(r, pos1_ref).start(priority=1)
        return carry

    def wait(r, carry):
        row_copy(r, pos0_ref).wait()
        row_copy(r, pos1_ref).wait()
        return carry

    lax.fori_loop(0, DISPATCH_ROWS, start, 0, unroll=8)
    lax.fori_loop(0, DISPATCH_ROWS, wait, 0, unroll=8)


def _dispatch(h2, pos0, pos1, p_pad):
    n, d = h2.shape
    grid_spec = pltpu.PrefetchScalarGridSpec(
        num_scalar_prefetch=2,
        grid=(n // DISPATCH_ROWS,),
        in_specs=[pl.BlockSpec((DISPATCH_ROWS, d), lambda i, p0, p1: (i, 0)), pl.BlockSpec(memory_space=pl.ANY)],
        out_specs=pl.BlockSpec(memory_space=pl.ANY),
        scratch_shapes=[pltpu.SemaphoreType.DMA],
    )
    return pl.pallas_call(
        _dispatch_kernel,
        grid_spec=grid_spec,
        out_shape=jax.ShapeDtypeStruct((p_pad, d), F32),
        input_output_aliases={3: 0},
        compiler_params=_cparams(("arbitrary",)),
        name="moe_dispatch",
    )(pos0, pos1, h2, jnp.zeros((p_pad, d), F32))


def _expert_ffn_kernel(te_ref, nr_ref, nu_ref, hs_ref, w1g_ref, w1u_ref, w2_ref, ys_ref, h_s, acc_s):
    t = pl.program_id(0)
    f = pl.program_id(1)
    tm = hs_ref.shape[0]
    used = t < nu_ref[0]
    full = nr_ref[t] > tm // 2

    def run(rows):
        @pl.when(f == 0)
        def _():
            h_s[0:rows, :] = hs_ref[0:rows, :].astype(BF16)
            acc_s[0:rows, :] = jnp.zeros((rows, acc_s.shape[1]), F32)

        h = h_s[0:rows, :]
        a = _silu(_dot(h, w1g_ref[0].astype(BF16))) * _dot(h, w1u_ref[0].astype(BF16))
        acc_s[0:rows, :] += _dot(a.astype(BF16), w2_ref[0].astype(BF16))

        @pl.when(f == pl.num_programs(1) - 1)
        def _():
            ys_ref[0:rows, :] = acc_s[0:rows, :]
            if rows < tm:
                ys_ref[rows:, :] = jnp.zeros((tm - rows, ys_ref.shape[1]), F32)

    pl.when(used & full)(lambda: run(tm))
    pl.when(used & jnp.logical_not(full))(lambda: run(tm // 2))

    @pl.when(jnp.logical_not(used) & (f == 0))
    def _():
        ys_ref[...] = jnp.zeros_like(ys_ref)


def _expert_ffn(hs, tile_e, tile_rows, n_used, w1, w2):
    p_pad, d = hs.shape
    ff = w2.shape[1]
    tm, tf = MOE_TM, FFN_TF
    nf = ff // tf
    nt = p_pad // tm

    def tile(t, nu):
        return jnp.maximum(jnp.minimum(t, nu[0] - 1), 0)

    def ftile(t, f, nu):
        return jnp.where(t < nu[0], f, nf - 1)

    grid_spec = pltpu.PrefetchScalarGridSpec(
        num_scalar_prefetch=3,
        grid=(nt, nf),
        in_specs=[
            pl.BlockSpec((tm, d), lambda t, f, te, nr, nu: (tile(t, nu), 0)),
            pl.BlockSpec((1, d, tf), lambda t, f, te, nr, nu: (te[tile(t, nu)], 0, ftile(t, f, nu))),
            pl.BlockSpec((1, d, tf), lambda t, f, te, nr, nu: (te[tile(t, nu)], 0, nf + ftile(t, f, nu))),
            pl.BlockSpec((1, tf, d), lambda t, f, te, nr, nu: (te[tile(t, nu)], ftile(t, f, nu), 0)),
        ],
        out_specs=pl.BlockSpec((tm, d), lambda t, f, te, nr, nu: (t, 0)),
        scratch_shapes=[pltpu.VMEM((tm, d), BF16), pltpu.VMEM((tm, d), F32)],
    )
    return pl.pallas_call(
        _expert_ffn_kernel,
        grid_spec=grid_spec,
        out_shape=jax.ShapeDtypeStruct((p_pad, d), F32),
        compiler_params=_cparams(("arbitrary", "arbitrary")),
        name="moe_expert_ffn",
    )(tile_e, tile_rows, n_used, hs, w1, w1, w2)


def _combine_kernel(pos0_ref, pos1_ref, ys_ref, x_ref, info_ref, gate_ref, g_ref, b_ref, o_ref, y0_s, y1_s, sem):
    tm = y0_s.shape[0]
    base = (pl.program_id(0) * pl.num_programs(1) + pl.program_id(1)) * tm

    def row_copy(r, pos_ref, dst):
        return pltpu.make_async_copy(ys_ref.at[pl.ds(pos_ref[base + r], 1), :], dst.at[pl.ds(r, 1), :], sem)

    def start(r, carry):
        row_copy(r, pos0_ref, y0_s).start(priority=0)
        row_copy(r, pos1_ref, y1_s).start(priority=1)
        return carry

    def wait(r, carry):
        row_copy(r, pos0_ref, y0_s).wait()
        row_copy(r, pos1_ref, y1_s).wait()
        return carry

    lax.fori_loop(0, tm, start, 0, unroll=8)
    lax.fori_loop(0, tm, wait, 0, unroll=8)
    info = info_ref[0]
    g0 = info[:, ROUTE_G0:ROUTE_G0 + 1]
    g1 = info[:, ROUTE_G1:ROUTE_G1 + 1]
    f = g0 * y0_s[...] + g1 * y1_s[...]
    z = DN_ALPHA * x_ref[0] + gate_ref[0] * f
    o_ref[0] = _layernorm_rows(z, g_ref[...], b_ref[...])


def _combine_ln(ys, pos0, pos1, x, info, gate, ln_g, ln_b):
    b, t, d = x.shape
    tm = min(t, COMBINE_ROWS)
    grid_spec = pltpu.PrefetchScalarGridSpec(
        num_scalar_prefetch=2,
        grid=(b, t // tm),
        in_specs=[
            pl.BlockSpec(memory_space=pl.ANY),
            pl.BlockSpec((1, tm, d), lambda bi, i, p0, p1: (bi, i, 0)),
            pl.BlockSpec((1, tm, LANES), lambda bi, i, p0, p1: (bi, i, 0)),
            pl.BlockSpec((1, 1, d), lambda bi, i, p0, p1: (bi, 0, 0)),
            pl.BlockSpec((1, d), lambda bi, i, p0, p1: (0, 0)),
            pl.BlockSpec((1, d), lambda bi, i, p0, p1: (0, 0)),
        ],
        out_specs=pl.BlockSpec((1, tm, d), lambda bi, i, p0, p1: (bi, i, 0)),
        scratch_shapes=[pltpu.VMEM((tm, d), F32), pltpu.VMEM((tm, d), F32), pltpu.SemaphoreType.DMA],
    )
    return pl.pallas_call(
        _combine_kernel,
        grid_spec=grid_spec,
        out_shape=jax.ShapeDtypeStruct((b, t, d), F32),
        compiler_params=_cparams(("arbitrary", "arbitrary")),
        name="moe_combine_ln",
    )(pos0, pos1, ys, x, info, gate, ln_g.reshape(1, d), ln_b.reshape(1, d))


def _moe_ln(x, scale, shift, w_router, w1, w2, gate, ln_g, ln_b):
    b, t, d = x.shape
    n = b * t
    tm = MOE_TM
    h, info, counts = _router(x, scale, shift, w_router)
    cnt = counts[0, :N_EXPERTS].astype(jnp.int32)
    padded = ((cnt + tm - 1) // tm) * tm
    ends = jnp.cumsum(padded)
    starts = ends - padded
    info2 = info.reshape(n, LANES)
    e0 = info2[:, ROUTE_E0].astype(jnp.int32)
    e1 = info2[:, ROUTE_E1].astype(jnp.int32)
    pos0 = starts[e0] + info2[:, ROUTE_R0].astype(jnp.int32)
    pos1 = starts[e1] + info2[:, ROUTE_R1].astype(jnp.int32)
    n_tiles = (TOP_K * n) // tm + N_EXPERTS
    tile_start = jnp.arange(n_tiles, dtype=jnp.int32) * tm
    tile_e = jnp.minimum(jnp.sum(tile_start[:, None] >= ends[None, :], axis=1), N_EXPERTS - 1).astype(jnp.int32)
    tile_rows = jnp.clip(starts[tile_e] + cnt[tile_e] - tile_start, 0, tm).astype(jnp.int32)
    n_used = (ends[-1] // tm).astype(jnp.int32).reshape(1)
    hs = _dispatch(h.reshape(n, d), pos0, pos1, n_tiles * tm)
    ys = _expert_ffn(hs, tile_e, tile_rows, n_used, w1, w2)
    return _combine_ln(ys, pos0, pos1, x, info, gate, ln_g, ln_b)


def _pair_perm(comp_offsets):
    even = np.concatenate([off + np.arange(0, ROT_DIM, 2) for off in comp_offsets])
    odd = np.concatenate([off + np.arange(1, ROT_DIM, 2) for off in comp_offsets])
    return np.concatenate([even, odd])


def _even_w_in(w_in):
    d = w_in.shape[0]
    a_qk = DA_HEADS * 2 * DA_DH
    head_perm = _pair_perm((0, DA_DH))
    qk_perm = np.concatenate([hh * 2 * DA_DH + head_perm for hh in range(DA_HEADS)])
    cols = np.concatenate([qk_perm, a_qk + qk_perm, np.arange(2 * a_qk, w_in.shape[1])])
    w = w_in[:, cols]
    return jnp.pad(w, ((0, 0), (0, EV_NPAD - w.shape[1]))).astype(BF16)


def _odd_w_in(w_in):
    o_ckv = MLA_Q_RANK
    o_kr = o_ckv + MLA_KV_RANK
    o_rest = o_kr + MLA_ROPE
    ev = o_kr + np.arange(0, MLA_ROPE, 2)
    od = o_kr + np.arange(1, MLA_ROPE, 2)
    cols = np.concatenate([np.arange(0, o_kr), ev, ev, od, od, np.arange(o_rest, w_in.shape[1])])
    return w_in[:, cols].astype(BF16)


def _mla_weights(w_uq, w_ukv):
    hq = MLA_NOPE + MLA_ROPE
    q3 = w_uq.reshape(MLA_Q_RANK, MLA_HEADS, hq).transpose(1, 0, 2)
    zeros = jnp.zeros((MLA_HEADS, MLA_Q_RANK, MLA_ROPE // 2), w_uq.dtype)
    rope = q3[:, :, MLA_NOPE:]
    wq = jnp.concatenate([q3[:, :, :MLA_NOPE], rope[:, :, 0::2], zeros, rope[:, :, 1::2], zeros], axis=-1)
    wkv = w_ukv.reshape(MLA_KV_RANK, MLA_HEADS, MLA_NOPE + MLA_DV).transpose(1, 0, 2)
    return wq.astype(BF16), wkv.astype(BF16)


def _gla_gate_weights(gk_w2, gk_b):
    pairs = GLA_HEADS // 2
    w = jnp.zeros((pairs, 2, LANES, LANES), F32)
    for d in range(2):
        blk = gk_w2[d].reshape(GLA_LR, pairs, LANES).transpose(1, 0, 2)
        w = w.at[:, d, d * GLA_LR:(d + 1) * GLA_LR, :].set(blk)
    bias = gk_b.reshape(2, pairs, 1, LANES).transpose(1, 0, 2, 3).astype(F32)
    return w.astype(BF16), bias


def _rope_tables(rows):
    n_freq = ROT_DIM // 4
    inv = ROPE_BASE ** (-jnp.arange(n_freq, dtype=F32) / n_freq)
    row = jnp.repeat(jnp.arange(rows, dtype=F32), GRID_W)
    col = jnp.tile(jnp.arange(GRID_W, dtype=F32), rows)
    ang = jnp.concatenate([row[:, None] * inv, col[:, None] * inv], axis=-1)
    cos, sin = jnp.cos(ang), jnp.sin(ang)
    return jnp.concatenate([cos] * 4, axis=-1), jnp.concatenate([-sin, -sin, sin, sin], axis=-1)


def _diff_lambda_init(layer):
    return 0.8 - 0.6 * math.exp(-0.3 * layer)


def kernel(x, c, ctx, c_ctx, ada_w, ada_b, post_ln_g, post_ln_b, lb_table, ev_w_in, ev_lam, ev_subln_g, ev_gk_w2,
           ev_gk_b, ev_gla_norm_g, ev_w_out, ev_ffn_w1, ev_ffn_w2, od_w_in, od_q_norm_g, od_kv_norm_g, od_w_uq,
           od_w_ukv, od_hg_norm_g, od_w_out, od_router, od_exp_w1, od_exp_w2):
    b, t, d = x.shape
    tc = ctx.shape[1]
    rope_c, rope_s = _rope_tables(t // GRID_W)
    lb_soft = jax.nn.softmax(lb_table.astype(F32), axis=0)
    lower_bounds = jnp.cumsum(lb_soft, axis=0) - lb_soft[0]

    n_cond = ((b + 1 + 7) // 8) * 8
    cond = jnp.zeros((n_cond, d), F32).at[:b].set(c).at[b].set(c_ctx)
    mods = _ada(cond, ada_w, ada_b).reshape(DEPTH, n_cond, 6, d)

    ctx_flat = None
    for layer in range(DEPTH):
        last = layer == DEPTH - 1
        j = layer // 2
        m_l = [mods[layer, :b, i][:, None, :] for i in range(6)]
        m_c = [jnp.broadcast_to(mods[layer, b, i][None, None, :], (b, 1, d)) for i in range(6)]
        m_c1 = [m[:1] for m in m_c]
        even = layer % 2 == 0
        w_in = _even_w_in(ev_w_in[j]) if even else _odd_w_in(od_w_in[j])
        p_l = _proj(x, 1.0 + m_l[1], m_l[0], w_in)
        p_c = _proj(ctx, 1.0 + m_c[1], m_c[0], w_in)
        need_ctx = not last
        if even:
            lam_init = _diff_lambda_init(layer)
            lv = ev_lam[j].astype(F32)
            lam = jnp.exp(jnp.sum(lv[0] * lv[1])) - jnp.exp(jnp.sum(lv[2] * lv[3])) + lam_init
            oa_c, oa_l = _diff_attention(p_l, p_c, rope_c, rope_s, lam, ev_subln_g[j], lam_init, need_ctx)
            wgk, bgk = _gla_gate_weights(ev_gk_w2[j], ev_gk_b[j])
            ob_c, ob_l = _gla_scan(p_c, p_l, wgk, bgk, ev_gla_norm_g[j], need_ctx)
            w_out = ev_w_out[j].astype(BF16)
        else:
            wq, wkv = _mla_weights(od_w_uq[j], od_w_ukv[j])
            oa_l = _mla_attention(p_l, p_c, rope_c, rope_s, od_q_norm_g[j], od_kv_norm_g[j], wq, wkv)
            oa_c = _mla_attention_ctx(p_c, od_q_norm_g[j], od_kv_norm_g[j], wq, wkv) if need_ctx else None
            ob_c, ob_l = _hgrn_scan(p_c, p_l, lower_bounds[layer], od_hg_norm_g[j], need_ctx)
            w_out = od_w_out[j].astype(BF16)
        g0, b0 = post_ln_g[layer, 0], post_ln_b[layer, 0]
        g1, b1 = post_ln_g[layer, 1], post_ln_b[layer, 1]
        x = _outproj_ln(oa_l, ob_l, w_out, x, m_l[2], g0, b0)
        if even:
            w1, w2 = ev_ffn_w1[j], ev_ffn_w2[j]
            x = _ffn_ln(x, 1.0 + m_l[4], m_l[3], w1, w2, m_l[5], g1, b1)
        else:
            w1, w2 = od_exp_w1[j], od_exp_w2[j]
            x = _moe_ln(x, 1.0 + m_l[4], m_l[3], od_router[j], w1, w2, m_l[5], g1, b1)
        if need_ctx:
            ctx = _outproj_ln(oa_c, ob_c, w_out, ctx, m_c[2], g0, b0)
            ctx_flat = ctx.reshape(1, b * tc, d)
            if even:
                ctx_flat = _ffn_ln(ctx_flat, 1.0 + m_c1[4], m_c1[3], w1, w2, m_c1[5], g1, b1)
            else:
                ctx_flat = _moe_ln(ctx_flat, 1.0 + m_c1[4], m_c1[3], od_router[j], w1, w2, m_c1[5], g1, b1)
            ctx = ctx_flat.reshape(b, tc, d)
    return x
```
